```python
import jax, jax.numpy as jnp
from jax import lax
import numpy as np

D_MODEL = 1024
BATCH = 1
SEQ = 16384
DEPTH = 1

GRID_W = 64
CTX_LEN = 256
MLSTM_HEADS = 4
MLSTM_DIM = 1024
MLSTM_HEAD_DIM = MLSTM_DIM // MLSTM_HEADS
MLSTM_CHUNK = 64
QK_CONV = 3
SGU_GROUPS = 4
SGU_DIM = 1024
SGU_GROUP_DIM = SGU_DIM // SGU_GROUPS
SGU_CHUNK = 128
D_FF = 2816
FFN_CONV = 3
N_BRANCH = 2
N_MOD = 6
N_GATES = 4 * MLSTM_HEADS
N_SCAN_IN = 3 * MLSTM_DIM + N_GATES
N_IN = N_SCAN_IN + MLSTM_DIM + 2 * SGU_DIM + N_BRANCH * D_MODEL
SCAN_SPLITS = (MLSTM_DIM, 2 * MLSTM_DIM, 3 * MLSTM_DIM)
REST_SPLITS = (MLSTM_DIM, MLSTM_DIM + SGU_DIM, MLSTM_DIM + 2 * SGU_DIM, MLSTM_DIM + 2 * SGU_DIM + D_MODEL)
EPS = 1e-6
M_INIT = -1e30

kernel_name = "hybrid_mlstm_sgu_convffn_dit_block"


def rmsnorm(x, g):
    xf = x.astype(jnp.float32)
    y = xf * lax.rsqrt(jnp.mean(xf * xf, axis=-1, keepdims=True) + EPS)
    return (y * g.astype(jnp.float32)).astype(x.dtype)


def modulate(h, shift, scale):
    return h * (1.0 + scale) + shift


def dwconv1d(x, w):
    return lax.conv_general_dilated(x, w[:, None, :].astype(x.dtype), window_strides=(1,), padding='SAME',
                                    dimension_numbers=('NWC', 'WIO', 'NWC'), feature_group_count=x.shape[-1])


def dwconv2d(x, w):
    return lax.conv_general_dilated(x, w[:, :, None, :].astype(x.dtype), window_strides=(1, 1), padding='SAME',
                                    dimension_numbers=('NHWC', 'HWIO', 'NHWC'), feature_group_count=x.shape[-1])


def mlstm_init_state(b):
    return (jnp.zeros((b, MLSTM_HEADS, MLSTM_HEAD_DIM, MLSTM_HEAD_DIM), jnp.float32),
            jnp.zeros((b, MLSTM_HEADS, MLSTM_HEAD_DIM), jnp.float32),
            jnp.full((b, MLSTM_HEADS), M_INIT, jnp.float32))


def mlstm_chunk_stats(k, v, log_i, log_f):
    b = jnp.cumsum(log_f, axis=-1)
    g = b[..., -1]
    a = g[..., None] - b + log_i
    m_loc = jnp.max(a, axis=-1)
    kw = k * jnp.exp(a - m_loc[..., None])[..., None]
    kv = jnp.einsum('bnhld,bnhle->bnhde', kw, v)
    kn = jnp.sum(kw, axis=3)
    return b, g, m_loc, kv, kn


def mlstm_state_scan(g, m_loc, kv, kn, state0):
    def step(carry, inp):
        c_st, n_st, m_st = carry
        g_c, m_c, kv_c, kn_c = inp
        m_new = jnp.maximum(g_c + m_st, m_c)
        a_old = jnp.exp(g_c + m_st - m_new)
        a_new = jnp.exp(m_c - m_new)
        c_new = a_old[..., None, None] * c_st + a_new[..., None, None] * kv_c
        n_new = a_old[..., None] * n_st + a_new[..., None] * kn_c
        return (c_new, n_new, m_new), (c_st, n_st, m_st)
    xs = (jnp.moveaxis(g, 1, 0), jnp.moveaxis(m_loc, 1, 0), jnp.moveaxis(kv, 1, 0), jnp.moveaxis(kn, 1, 0))
    final, prev = lax.scan(step, state0, xs)
    prev = (jnp.moveaxis(prev[0], 0, 1), jnp.moveaxis(prev[1], 0, 1), jnp.moveaxis(prev[2], 0, 1))
    return prev, final


def mlstm_chunk_out(q, k, v, log_i, b, prev):
    c_prev, n_prev, m_prev = prev
    L = q.shape[3]
    inter = b + m_prev[..., None]
    d = b[..., :, None] - b[..., None, :] + log_i[..., None, :]
    scan_order = jnp.tril(jnp.ones((L, L), dtype=bool))
    d = jnp.where(scan_order, d, -jnp.inf)
    m_row = jnp.maximum(inter, jnp.max(d, axis=-1))
    w_inter = jnp.exp(inter - m_row)
    s = jnp.einsum('bnhld,bnhsd->bnhls', q, k) * jnp.exp(d - m_row[..., None])
    num = jnp.einsum('bnhls,bnhse->bnhle', s, v) + w_inter[..., None] * jnp.einsum('bnhld,bnhde->bnhle', q, c_prev)
    den = jnp.sum(s, axis=-1) + w_inter * jnp.einsum('bnhld,bnhd->bnhl', q, n_prev)
    return num / jnp.maximum(jnp.abs(den), jnp.exp(-m_row))[..., None]


def mlstm_direction(q, k, v, i_pre, f_pre, state0, with_out):
    bsz, t_len = q.shape[0], q.shape[1]
    n_chunks = t_len // MLSTM_CHUNK

    def chunks(t):
        return jnp.swapaxes(t.reshape((bsz, n_chunks, MLSTM_CHUNK) + t.shape[2:]), 2, 3)

    qc, kc, vc = chunks(q), chunks(k), chunks(v)
    log_i = chunks(i_pre)
    log_f = chunks(jax.nn.log_sigmoid(f_pre))
    b, g, m_loc, kv, kn = mlstm_chunk_stats(kc, vc, log_i, log_f)
    prev, final = mlstm_state_scan(g, m_loc, kv, kn, state0)
    if not with_out:
        return None, final
    h = mlstm_chunk_out(qc, kc, vc, log_i, b, prev)
    return jnp.swapaxes(h, 2, 3).reshape(bsz, t_len, MLSTM_HEADS, MLSTM_HEAD_DIM), final


def mlstm_bidir(q, k, v, gates, state_f, state_b, with_out):
    flip = lambda t: jnp.flip(t, axis=1)
    h_f, fin_f = mlstm_direction(q, k, v, gates[:, :, 0], gates[:, :, 1], state_f, with_out)
    h_b, fin_b = mlstm_direction(flip(q), flip(k), flip(v), flip(gates[:, :, 2]), flip(gates[:, :, 3]), state_b, with_out)
    h = h_f + flip(h_b) if with_out else None
    return h, fin_f, fin_b


def spatial_gating(u, v, ln_g, ln_b, w_s, b_s):
    bsz, t_len, _ = u.shape
    vf = v.astype(jnp.float32)
    vf = vf - jnp.mean(vf, axis=-1, keepdims=True)
    vn = vf * lax.rsqrt(jnp.mean(vf * vf, axis=-1, keepdims=True) + EPS) * ln_g + ln_b
    vn = vn.reshape(bsz, t_len // SGU_CHUNK, SGU_CHUNK, SGU_GROUPS, SGU_GROUP_DIM)
    mixed = jnp.einsum('gpq,bnqgd->bnpgd', w_s.astype(jnp.float32), vn) + b_s.T[:, :, None]
    return u * mixed.reshape(bsz, t_len, SGU_DIM).astype(u.dtype)


def token_mixer(hn, w_in_l, b_gate_l, conv_qk_l, head_norm_g_l, sgu_ln_g_l, sgu_ln_b_l, w_s_l, b_s_l,
                w_branch_mlstm_l, w_branch_sgu_l, w_out_l, state_f, state_b, with_out):
    bsz, t_len, _ = hn.shape
    n_cols = N_IN if with_out else N_SCAN_IN
    z = hn @ w_in_l[:, :n_cols]
    z_q, z_k, z_v, z_g = jnp.split(z[..., :N_SCAN_IN], SCAN_SPLITS, axis=-1)
    qk = jax.nn.silu(dwconv1d(jnp.concatenate([z_q, z_k], axis=-1), conv_qk_l)).astype(jnp.float32)
    q, k = jnp.split(qk, 2, axis=-1)
    heads = lambda t: t.reshape(bsz, t_len, MLSTM_HEADS, MLSTM_HEAD_DIM)
    q = heads(q) * (MLSTM_HEAD_DIM ** -0.5)
    gates = (z_g + b_gate_l).astype(jnp.float32).reshape(bsz, t_len, 4, MLSTM_HEADS)
    h_m, fin_f, fin_b = mlstm_bidir(q, heads(k), heads(z_v.astype(jnp.float32)), gates, state_f, state_b, with_out)
    if not with_out:
        return None, fin_f, fin_b
    z_o, z_u, z_vg, z_gm, z_gg = jnp.split(z[..., N_SCAN_IN:], REST_SPLITS, axis=-1)
    h_m = h_m * lax.rsqrt(jnp.mean(h_m * h_m, axis=-1, keepdims=True) + EPS)
    h_m = (h_m * head_norm_g_l.astype(jnp.float32).reshape(MLSTM_HEADS, MLSTM_HEAD_DIM)).reshape(bsz, t_len, MLSTM_DIM)
    y_m = (jax.nn.sigmoid(z_o.astype(jnp.float32)) * h_m).astype(hn.dtype)
    y_s = spatial_gating(jax.nn.gelu(z_u), jax.nn.gelu(z_vg), sgu_ln_g_l, sgu_ln_b_l, w_s_l, b_s_l)
    y = jax.nn.sigmoid(z_gm) * (y_m @ w_branch_mlstm_l) + jax.nn.sigmoid(z_gg) * (y_s @ w_branch_sgu_l)
    return y @ w_out_l, fin_f, fin_b


def conv_ffn(hn, w_up_l, w_conv_l, w_down_l, grid_rows):
    a, b = jnp.split(hn @ w_up_l, 2, axis=-1)
    bsz, t_len, _ = a.shape
    if grid_rows is None:
        a = dwconv1d(a, w_conv_l[FFN_CONV // 2])
    else:
        a = dwconv2d(a.reshape(bsz, grid_rows, GRID_W, D_FF), w_conv_l).reshape(bsz, t_len, D_FF)
    return (jax.nn.silu(a) * b) @ w_down_l


def setup_inputs(seed: int = 0) -> dict:
    key = jax.random.key(seed)
    ks = iter(jax.random.split(key, 40))
    nrm = lambda shape, std: std * jax.random.normal(next(ks), shape, jnp.float32)
    L, D = DEPTH, D_MODEL
    b_i = nrm((L, 2, MLSTM_HEADS), 0.1)
    b_f = jax.random.uniform(next(ks), (L, 2, MLSTM_HEADS), jnp.float32, 3.0, 6.0)
    b_gate = jnp.stack([b_i, b_f], axis=2).reshape(L, N_GATES)
    return {
        "x": nrm((BATCH, SEQ, D), 1.0),
        "c": nrm((BATCH, D), 1.0),
        "ctx": nrm((BATCH, CTX_LEN, D), 1.0),
        "c_ctx": nrm((D,), 1.0),
        "w_mod": nrm((L, D, N_MOD * D), 0.02),
        "b_mod": nrm((L, N_MOD * D), 0.02),
        "norm1_g": 1.0 + nrm((L, D), 0.02),
        "w_in": nrm((L, D, N_IN), D ** -0.5),
        "b_gate": b_gate,
        "conv_qk": nrm((L, QK_CONV, 2 * MLSTM_DIM), QK_CONV ** -0.5),
        "head_norm_g": 1.0 + nrm((L, MLSTM_DIM), 0.02),
        "sgu_ln_g": 1.0 + nrm((L, SGU_DIM), 0.02),
        "sgu_ln_b": nrm((L, SGU_DIM), 0.02),
        "w_s": nrm((L, SGU_GROUPS, SGU_CHUNK, SGU_CHUNK), SGU_CHUNK ** -0.5),
        "b_s": 1.0 + nrm((L, SGU_GROUPS, SGU_CHUNK), 0.1),
        "w_branch_mlstm": nrm((L, MLSTM_DIM, D), MLSTM_DIM ** -0.5),
        "w_branch_sgu": nrm((L, SGU_DIM, D), SGU_DIM ** -0.5),
        "w_out": nrm((L, D, D), D ** -0.5),
        "norm2_g": 1.0 + nrm((L, D), 0.02),
        "w_up": nrm((L, D, 2 * D_FF), D ** -0.5),
        "w_ffn_conv": nrm((L, FFN_CONV, FFN_CONV, D_FF), 1.0 / FFN_CONV),
        "w_down": nrm((L, D_FF, D), D_FF ** -0.5),
        "final_g": 1.0 + nrm((D,), 0.02),
    }


def reference(x, c, ctx, c_ctx, w_mod, b_mod, norm1_g, w_in, b_gate, conv_qk, head_norm_g, sgu_ln_g, sgu_ln_b,
              w_s, b_s, w_branch_mlstm, w_branch_sgu, w_out, norm2_g, w_up, w_ffn_conv, w_down, final_g):
    bsz, t_len, _ = x.shape
    rows = t_len // GRID_W
    h_x, h_c = x, ctx
    for l in range(DEPTH):
        last = l == DEPTH - 1
        mx = jnp.split((jax.nn.silu(c) @ w_mod[l] + b_mod[l])[:, None, :], N_MOD, axis=-1)
        mc = jnp.split(jax.nn.silu(c_ctx) @ w_mod[l] + b_mod[l], N_MOD, axis=-1)
        mixer_w = (w_in[l], b_gate[l], conv_qk[l], head_norm_g[l], sgu_ln_g[l], sgu_ln_b[l], w_s[l], b_s[l],
                   w_branch_mlstm[l], w_branch_sgu[l], w_out[l])
        hn_c = modulate(rmsnorm(h_c, norm1_g[l]), mc[0], mc[1])
        init = mlstm_init_state(bsz)
        out_c, st_f, st_b = token_mixer(hn_c, *mixer_w, init, init, not last)
        hn_x = modulate(rmsnorm(h_x, norm1_g[l]), mx[0], mx[1])
        out_x, _, _ = token_mixer(hn_x, *mixer_w, st_f, st_b, True)
        h_x = h_x + mx[2] * out_x
        h_x = h_x + mx[5] * conv_ffn(modulate(rmsnorm(h_x, norm2_g[l]), mx[3], mx[4]), w_up[l], w_ffn_conv[l], w_down[l], rows)
        if not last:
            h_c = h_c + mc[2] * out_c
            h_c = h_c + mc[5] * conv_ffn(modulate(rmsnorm(h_c, norm2_g[l]), mc[3], mc[4]), w_up[l], w_ffn_conv[l], w_down[l], None)
    return rmsnorm(h_x, final_g)
```

```python
import functools

import jax
import jax.numpy as jnp
from jax import lax
from jax.experimental import pallas as pl
from jax.experimental.pallas import tpu as pltpu

F32 = jnp.float32
BF16 = jnp.bfloat16

D_MODEL = 1024
GRID_W = 64
HEADS = 4
HEAD_DIM = 256
N_MOD = 6
D_FF = 2816
EPS = 1e-6
M_INIT = -1e30

_L = 256
_HALO = 16
_FF_CHUNK = 256
_N_FF_CHUNKS = D_FF // _FF_CHUNK
_GATE_LANES = 128
_PAD = 8
_VMEM_LIMIT_V7X = 56 * 1024 * 1024

_GROUP_DIR = (0, 0, 1, 1, 0, 1, 0, 1, 0, 1)
_ROW_GROUP0 = 4


def _dot(a, b):
    return jnp.dot(a, b, preferred_element_type=F32)


def _dot_nt(a, b):
    return lax.dot_general(a, b, (((1,), (1,)), ((), ())), preferred_element_type=F32)


def _dot_tn(a, b):
    return lax.dot_general(a, b, (((0,), (0,)), ((), ())), preferred_element_type=F32)


def _split3(x):
    h1 = x.astype(BF16)
    r1 = x - h1.astype(F32)
    h2 = r1.astype(BF16)
    h3 = (r1 - h2.astype(F32)).astype(BF16)
    return h1, h2, h3


def _sigmoid(x):
    return 1.0 / (1.0 + jnp.exp(-x))


def _silu(x):
    return x * _sigmoid(x)


def _gelu_tanh(x):
    return 0.5 * x * (1.0 + jnp.tanh(0.7978845608028654 * (x + 0.044715 * (x * x * x))))


def _log_sigmoid(x):
    return jnp.minimum(x, 0.0) - jnp.log(1.0 + jnp.exp(-jnp.abs(x)))


def _rms(x):
    return x * lax.rsqrt(jnp.mean(x * x, axis=-1, keepdims=True) + EPS)


def _const_spec(shape):
    nd = len(shape)
    return pl.BlockSpec(shape, lambda *_: (0,) * nd, pipeline_mode=pl.Buffered(1))


def _mod_kernel(cc_ref, w_ref, b_ref, o_ref):
    s = _silu(cc_ref[...])
    sh, sm, sl = _split3(s)
    wh, wm, wl = _split3(w_ref[...])
    acc = _dot(sl, wh) + _dot(sh, wl) + _dot(sm, wm)
    acc = acc + _dot(sm, wh) + _dot(sh, wm)
    o_ref[...] = acc + _dot(sh, wh) + b_ref[...]


def _modulation(cc, w_mod, b_mod):
    n_out = w_mod.shape[1]
    nb = D_MODEL
    return pl.pallas_call(
        _mod_kernel,
        grid=(n_out // nb,),
        in_specs=[pl.BlockSpec((8, D_MODEL), lambda j: (0, 0)),
                  pl.BlockSpec((D_MODEL, nb), lambda j: (0, j)),
                  pl.BlockSpec((1, nb), lambda j: (0, j))],
        out_specs=pl.BlockSpec((8, nb), lambda j: (0, j)),
        out_shape=jax.ShapeDtypeStruct((8, n_out), F32),
        compiler_params=pltpu.CompilerParams(dimension_semantics=("arbitrary",)),
        name="modulation",
    )(cc, w_mod, b_mod)


def _gate_table(gi, lf):
    row = lax.broadcasted_iota(jnp.int32, (_L, _L), 0)
    col = lax.broadcasted_iota(jnp.int32, (_L, _L), 1)
    ltri = (col <= row).astype(BF16)
    l1, l2, l3 = _split3(lf)
    prefix = _dot(ltri, l3) + _dot(ltri, l2) + _dot(ltri, l1)
    total = prefix[_L - 1:_L, :]
    suffix = total - prefix + lf
    group = lax.broadcasted_iota(jnp.int32, (1, _GATE_LANES), 1) // HEADS
    is_bwd = functools.reduce(jnp.logical_or, [group == k for k, d in enumerate(_GROUP_DIR) if d == 1])
    cum = jnp.where(is_bwd, suffix, prefix)
    a = total - cum + gi
    mloc = jnp.max(a, axis=0, keepdims=True)
    table = jnp.where(group <= 3, jnp.where(group % 2 == 0, cum, a),
                      jnp.where(group <= 5, gi - cum,
                                jnp.where(group <= 7, jnp.broadcast_to(total, a.shape),
                                          jnp.broadcast_to(mloc, a.shape))))
    return table


def _proj_kernel(x_ref, xp_ref, xn_ref, shift_ref, scale_ref, g1_ref, wqkv_ref, wgi_ref, wgf_ref, bgi_ref, bgf_ref,
                 cw_ref, wrest_ref, lng_ref, lnb_ref, wsd_ref, bs_ref, wbs_ref,
                 q_ref, k_ref, v_ref, gc_ref, gr_ref, so_ref, sgm_ref, s_ref,
                 hn_s, z_s, ys_s, *, tm):
    i = pl.program_id(0)
    last = pl.num_programs(0) - 1
    scale1 = 1.0 + scale_ref[...]

    def norm_mod(xb):
        return ((_rms(xb) * g1_ref[...]) * scale1 + shift_ref[...]).astype(BF16)

    hn_s[0:_HALO, :] = norm_mod(xp_ref[...])
    hn_s[_HALO:_HALO + tm, :] = norm_mod(x_ref[...])
    hn_s[_HALO + tm:, :] = norm_mod(xn_ref[...])
    hn = hn_s[_HALO:_HALO + tm, :]

    for c, out_ref in enumerate((q_ref, k_ref)):
        z_s[...] = _dot(hn_s[...], wqkv_ref[:, c * D_MODEL:(c + 1) * D_MODEL])

        @pl.when(i == 0)
        def _():
            z_s[0:_HALO, :] = jnp.zeros((_HALO, D_MODEL), F32)

        @pl.when(i == last)
        def _():
            z_s[_HALO + tm:, :] = jnp.zeros((_HALO, D_MODEL), F32)

        cw = cw_ref[:, c * D_MODEL:(c + 1) * D_MODEL]
        y = (cw[0:1, :] * z_s[_HALO - 1:_HALO - 1 + tm, :] + cw[1:2, :] * z_s[_HALO:_HALO + tm, :]
             + cw[2:3, :] * z_s[_HALO + 1:_HALO + 1 + tm, :])
        y = _silu(y)
        if c == 0:
            y = y * (HEAD_DIM ** -0.5)
        out_ref[...] = y.astype(BF16)

    v_ref[...] = _dot(hn, wqkv_ref[:, 2 * D_MODEL:3 * D_MODEL]).astype(BF16)

    gi = _dot(hn, wgi_ref[...]) + bgi_ref[...]
    lf = _log_sigmoid(_dot(hn, wgf_ref[...]) + bgf_ref[...])
    for j in range(tm // _L):
        table = _gate_table(gi[j * _L:(j + 1) * _L, :], lf[j * _L:(j + 1) * _L, :])
        gc_ref[j * _L:(j + 1) * _L, :] = table
        gr_ref[:, j * _L:(j + 1) * _L] = table.T[_ROW_GROUP0 * HEADS:(_ROW_GROUP0 + 2) * HEADS, :]

    so_ref[...] = _sigmoid(_dot(hn, wrest_ref[:, 0:D_MODEL])).astype(BF16)
    sgm_ref[...] = _sigmoid(_dot(hn, wrest_ref[:, 3 * D_MODEL:4 * D_MODEL])).astype(BF16)

    u = _gelu_tanh(_dot(hn, wrest_ref[:, D_MODEL:2 * D_MODEL]))
    vg = _gelu_tanh(_dot(hn, wrest_ref[:, 2 * D_MODEL:3 * D_MODEL]))
    vg = vg - jnp.mean(vg, axis=-1, keepdims=True)
    vn = (vg * lax.rsqrt(jnp.mean(vg * vg, axis=-1, keepdims=True) + EPS) * lng_ref[...] + lnb_ref[...]).astype(BF16)
    for p in range(tm // _L):
        for g in range(HEADS):
            rows = slice(p * _L, (p + 1) * _L)
            cols = slice(g * HEAD_DIM, (g + 1) * HEAD_DIM)
            mixed = _dot(wsd_ref[g], vn[rows, cols]) + bs_ref[:, g:g + 1]
            ys_s[rows, cols] = (u[rows, cols] * mixed).astype(BF16)
    gg = _sigmoid(_dot(hn, wrest_ref[:, 4 * D_MODEL:5 * D_MODEL]))
    s_ref[...] = (gg * _dot(ys_s[...], wbs_ref[...])).astype(BF16)


def _projection(x, shift, scale, g1, wqkv, wgi, wgf, bgi, bgf, cw, wrest, lng, lnb, wsd, bs2, wbs, *, tm):
    t = x.shape[0]
    nt = t // tm
    hb = tm // _HALO
    nhb = t // _HALO
    tile = lambda w: pl.BlockSpec((tm, w), lambda i: (i, 0))
    in_specs = [
        tile(D_MODEL),
        pl.BlockSpec((_HALO, D_MODEL), lambda i: (jnp.maximum(i * hb - 1, 0), 0)),
        pl.BlockSpec((_HALO, D_MODEL), lambda i: (jnp.minimum((i + 1) * hb, nhb - 1), 0)),
    ] + [_const_spec(a.shape) for a in (shift, scale, g1, wqkv, wgi, wgf, bgi, bgf, cw, wrest, lng, lnb, wsd, bs2, wbs)]
    out_specs = [tile(D_MODEL), tile(D_MODEL), tile(D_MODEL), tile(_GATE_LANES),
                 pl.BlockSpec((2 * HEADS, tm), lambda i: (0, i)),
                 tile(D_MODEL), tile(D_MODEL), tile(D_MODEL)]
    act = jax.ShapeDtypeStruct((t, D_MODEL), BF16)
    out_shape = [act, act, act, jax.ShapeDtypeStruct((t, _GATE_LANES), F32),
                 jax.ShapeDtypeStruct((2 * HEADS, t), F32), act, act, act]
    return pl.pallas_call(
        functools.partial(_proj_kernel, tm=tm),
        grid=(nt,),
        in_specs=in_specs,
        out_specs=out_specs,
        out_shape=out_shape,
        scratch_shapes=[pltpu.VMEM((tm + 2 * _HALO, D_MODEL), BF16),
                        pltpu.VMEM((tm + 2 * _HALO, D_MODEL), F32),
                        pltpu.VMEM((tm, D_MODEL), BF16)],
        compiler_params=pltpu.CompilerParams(dimension_semantics=("parallel",), vmem_limit_bytes=_VMEM_LIMIT_V7X),
        name="projection",
    )(x, x, x, shift, scale, g1, wqkv, wgi, wgf, bgi, bgf, cw, wrest, lng, lnb, wsd, bs2, wbs)


def _mlstm_head(qh, kh, vh, bcol, acol, crow, g, mloc, m_prev, c_prev, n_prev, mask):
    inter = bcol + m_prev
    dmat = jnp.where(mask, bcol + crow, -jnp.inf)
    m_row = jnp.maximum(inter, jnp.max(dmat, axis=1, keepdims=True))
    s = _dot_nt(qh, kh) * jnp.exp(dmat - m_row)
    w_inter = jnp.exp(inter - m_row)
    num = _dot(s.astype(BF16), vh) + w_inter * _dot(qh, c_prev.astype(BF16))
    den = jnp.sum(s, axis=1, keepdims=True) + w_inter * jnp.sum(qh.astype(F32) * n_prev, axis=1, keepdims=True)
    h = num / jnp.maximum(jnp.abs(den), jnp.exp(-m_row))
    m_new = jnp.maximum(g + m_prev, mloc)
    a_old = jnp.exp(g + m_prev - m_new)
    kw = kh.astype(F32) * jnp.exp(acol - m_new)
    c_new = a_old * c_prev + _dot_tn(kw.astype(BF16), vh)
    n_new = a_old * n_prev + jnp.sum(kw, axis=0, keepdims=True)
    return h, c_new, n_new, m_new


def _mlstm_kernel(qf_ref, kf_ref, vf_ref, gcf_ref, grf_ref, qb_ref, kb_ref, vb_ref, gcb_ref, grb_ref,
                  c0_ref, n0_ref, m0_ref, hf_ref, hb_ref, c_ref, n_ref, m_ref):
    @pl.when(pl.program_id(0) == 0)
    def _():
        c_ref[...] = c0_ref[...]
        n_ref[...] = n0_ref[...]
        m_ref[...] = m0_ref[...]

    row = lax.broadcasted_iota(jnp.int32, (_L, _L), 0)
    col = lax.broadcasted_iota(jnp.int32, (_L, _L), 1)
    dirs = ((qf_ref, kf_ref, vf_ref, gcf_ref, grf_ref, hf_ref, col <= row),
            (qb_ref, kb_ref, vb_ref, gcb_ref, grb_ref, hb_ref, col >= row))
    for d, (q_ref, k_ref, v_ref, gc_ref, gr_ref, h_ref, mask) in enumerate(dirs):
        gc = gc_ref[...]
        gr = gr_ref[...]
        for hd in range(HEADS):
            lane = lambda grp: slice((grp + d) * HEADS + hd, (grp + d) * HEADS + hd + 1)
            cols = slice(hd * HEAD_DIM, (hd + 1) * HEAD_DIM)
            st = d * HEADS + hd
            h, c_new, n_new, m_new = _mlstm_head(
                q_ref[:, cols], k_ref[:, cols], v_ref[:, cols],
                gc[:, (2 * d) * HEADS + hd:(2 * d) * HEADS + hd + 1],
                gc[:, (2 * d + 1) * HEADS + hd:(2 * d + 1) * HEADS + hd + 1],
                gr[st:st + 1, :],
                gc[0:1, lane(6)], gc[0:1, lane(8)],
                m_ref[st:st + 1, 0:1], c_ref[st], n_ref[st:st + 1, :], mask)
            h_ref[:, cols] = h.astype(BF16)
            c_ref[st] = c_new
            n_ref[st:st + 1, :] = n_new
            m_ref[st:st + 1, :] = jnp.broadcast_to(m_new, (1, _GATE_LANES))


def _mlstm(q, k, v, gc, gr, c0, n0, m0):
    t = q.shape[0]
    nc = t // _L
    fwd = lambda w: pl.BlockSpec((_L, w), lambda i: (i, 0))
    bwd = lambda w: pl.BlockSpec((_L, w), lambda i: (nc - 1 - i, 0))
    in_specs = [fwd(D_MODEL), fwd(D_MODEL), fwd(D_MODEL), fwd(_GATE_LANES),
                pl.BlockSpec((2 * HEADS, _L), lambda i: (0, i)),
                bwd(D_MODEL), bwd(D_MODEL), bwd(D_MODEL), bwd(_GATE_LANES),
                pl.BlockSpec((2 * HEADS, _L), lambda i: (0, nc - 1 - i)),
                _const_spec(c0.shape), _const_spec(n0.shape), _const_spec(m0.shape)]
    state_specs = [pl.BlockSpec(a.shape, lambda i, nd=a.ndim: (0,) * nd) for a in (c0, n0, m0)]
    act = jax.ShapeDtypeStruct((t, D_MODEL), BF16)
    return pl.pallas_call(
        _mlstm_kernel,
        grid=(nc,),
        in_specs=in_specs,
        out_specs=[fwd(D_MODEL), bwd(D_MODEL)] + state_specs,
        out_shape=[act, act] + [jax.ShapeDtypeStruct(a.shape, F32) for a in (c0, n0, m0)],
        compiler_params=pltpu.CompilerParams(dimension_semantics=("arbitrary",), vmem_limit_bytes=_VMEM_LIMIT_V7X),
        name="mlstm",
    )(q, k, v, gc, gr, q, k, v, gc, gr, c0, n0, m0)


def _post_kernel(hf_ref, hb_ref, so_ref, sgm_ref, s_ref, x_ref, hg_ref, wbm_ref, wout_ref, gate_ref,
                 g2_ref, shift_ref, scale_ref, hx_ref, hn2_ref, ym_s):
    hm = hf_ref[...].astype(F32) + hb_ref[...].astype(F32)
    for hd in range(HEADS):
        cols = slice(hd * HEAD_DIM, (hd + 1) * HEAD_DIM)
        ym = _rms(hm[:, cols]) * hg_ref[:, cols]
        ym_s[:, cols] = (so_ref[:, cols].astype(F32) * ym).astype(BF16)
    y = sgm_ref[...].astype(F32) * _dot(ym_s[...], wbm_ref[...]) + s_ref[...].astype(F32)
    hx = x_ref[...] + gate_ref[...] * _dot(y.astype(BF16), wout_ref[...])
    hx_ref[...] = hx
    hn2_ref[...] = ((_rms(hx) * g2_ref[...]) * (1.0 + scale_ref[...]) + shift_ref[...]).astype(BF16)


def _post(hf, hb, so, sgm, s, x, hg, wbm, wout, gate, g2, shift, scale, *, tm):
    t = x.shape[0]
    tile = lambda: pl.BlockSpec((tm, D_MODEL), lambda i: (i, 0))
    return pl.pallas_call(
        _post_kernel,
        grid=(t // tm,),
        in_specs=[tile() for _ in range(6)] + [_const_spec(a.shape) for a in (hg, wbm, wout, gate, g2, shift, scale)],
        out_specs=[tile(), tile()],
        out_shape=[jax.ShapeDtypeStruct((t, D_MODEL), F32), jax.ShapeDtypeStruct((t, D_MODEL), BF16)],
        scratch_shapes=[pltpu.VMEM((tm, D_MODEL), BF16)],
        compiler_params=pltpu.CompilerParams(dimension_semantics=("parallel",), vmem_limit_bytes=_VMEM_LIMIT_V7X),
        name="post_mixer",
    )(hf, hb, so, sgm, s, x, hg, wbm, wout, gate, g2, shift, scale)


def _ffn_kernel(hn_ref, hnp_ref, hnn_ref, hx_ref, wua_ref, wub_ref, cw_ref, wdn_ref, gate_ref, fg_ref, o_ref,
                hne_s, a_s, acc_s, *, tm):
    i = pl.program_id(0)
    last = pl.num_programs(0) - 1
    ext = tm + 2 * GRID_W
    hne_s[0:GRID_W, :] = hnp_ref[...]
    hne_s[GRID_W:GRID_W + tm, :] = hn_ref[...]
    hne_s[GRID_W + tm:, :] = hnn_ref[...]
    a_s[0:_PAD, :] = jnp.zeros((_PAD, _FF_CHUNK), F32)
    a_s[_PAD + ext:, :] = jnp.zeros((_PAD, _FF_CHUNK), F32)
    acc_s[...] = jnp.zeros((tm, D_MODEL), F32)
    gcol = lax.broadcasted_iota(jnp.int32, (ext, 1), 0) % GRID_W

    def chunk(c, carry):
        a_s[_PAD:_PAD + ext, :] = _dot(hne_s[...], wua_ref[c])

        @pl.when(i == 0)
        def _():
            a_s[_PAD:_PAD + GRID_W, :] = jnp.zeros((GRID_W, _FF_CHUNK), F32)

        @pl.when(i == last)
        def _():
            a_s[_PAD + GRID_W + tm:_PAD + ext, :] = jnp.zeros((GRID_W, _FF_CHUNK), F32)

        taps = (jnp.where(gcol >= 1, a_s[_PAD - 1:_PAD - 1 + ext, :], 0.0),
                a_s[_PAD:_PAD + ext, :],
                jnp.where(gcol <= GRID_W - 2, a_s[_PAD + 1:_PAD + 1 + ext, :], 0.0))
        cw = cw_ref[c]
        conv = None
        for dr in range(3):
            for dc in range(3):
                term = cw[3 * dr + dc:3 * dr + dc + 1, :] * taps[dc][dr * GRID_W:dr * GRID_W + tm, :]
                conv = term if conv is None else conv + term
        b = _dot(hn_ref[...], wub_ref[c])
        act = (_silu(conv) * b).astype(BF16)
        acc_s[...] += _dot(act, wdn_ref[c])
        return carry

    lax.fori_loop(0, _N_FF_CHUNKS, chunk, 0)
    h2 = hx_ref[...] + gate_ref[...] * acc_s[...]
    o_ref[...] = _rms(h2) * fg_ref[...]


def _ffn(hn2, hx, wua, wub, cwf, wdn, gate, fg, *, tm):
    t = hx.shape[0]
    rb = tm // GRID_W
    nrb = t // GRID_W
    tile = lambda: pl.BlockSpec((tm, D_MODEL), lambda i: (i, 0))
    ext = tm + 2 * GRID_W
    return pl.pallas_call(
        functools.partial(_ffn_kernel, tm=tm),
        grid=(t // tm,),
        in_specs=[tile(),
                  pl.BlockSpec((GRID_W, D_MODEL), lambda i: (jnp.maximum(i * rb - 1, 0), 0)),
                  pl.BlockSpec((GRID_W, D_MODEL), lambda i: (jnp.minimum((i + 1) * rb, nrb - 1), 0)),
                  tile()] + [_const_spec(a.shape) for a in (wua, wub, cwf, wdn, gate, fg)],
        out_specs=tile(),
        out_shape=jax.ShapeDtypeStruct((t, D_MODEL), F32),
        scratch_shapes=[pltpu.VMEM((ext, D_MODEL), BF16),
                        pltpu.VMEM((ext + 2 * _PAD, _FF_CHUNK), F32),
                        pltpu.VMEM((tm, D_MODEL), F32)],
        compiler_params=pltpu.CompilerParams(dimension_semantics=("parallel",), vmem_limit_bytes=_VMEM_LIMIT_V7X),
        name="conv_ffn",
    )(hn2, hn2, hn2, hx, wua, wub, cwf, wdn, gate, fg)


def _gate_columns(wg):
    idx_i = [2 * d * HEADS + h for d in _GROUP_DIR for h in range(HEADS)]
    idx_f = [(2 * d + 1) * HEADS + h for d in _GROUP_DIR for h in range(HEADS)]
    pad = _GATE_LANES - len(idx_i)
    take = lambda idx: jnp.pad(wg[..., jnp.array(idx)], [(0, 0)] * (wg.ndim - 1) + [(0, pad)])
    return take(idx_i), take(idx_f)


def _tile_size(t, want):
    tm = min(want, t)
    assert t % tm == 0 and tm % _L == 0, (t, tm)
    return tm


def kernel(x, c, ctx, c_ctx, w_mod, b_mod, norm1_g, w_in, b_gate, conv_qk, head_norm_g, sgu_ln_g, sgu_ln_b, w_s, b_s,
           w_branch_mlstm, w_branch_sgu, w_out, norm2_g, w_up, w_ffn_conv, w_down, final_g):
    assert x.shape[0] == 1 and w_mod.shape[0] == 1, "single batch element, single layer"
    d = D_MODEL
    x2, ctx2 = x[0], ctx[0]
    t = x2.shape[0]
    assert t % GRID_W == 0 and ctx2.shape[0] % _L == 0
    row = lambda a: a.reshape(1, -1)

    cc = jnp.zeros((8, d), F32).at[0].set(c[0]).at[1].set(c_ctx)
    mod = _modulation(cc, w_mod[0], row(b_mod[0]))
    mx = [mod[0:1, j * d:(j + 1) * d] for j in range(N_MOD)]
    mc = [mod[1:2, j * d:(j + 1) * d] for j in range(N_MOD)]

    w_in0 = w_in[0]
    wqkv = w_in0[:, :3 * d].astype(BF16)
    wgi, wgf = _gate_columns(w_in0[:, 3 * d:3 * d + 4 * HEADS])
    bgi, bgf = _gate_columns(row(b_gate[0]))
    wrest = w_in0[:, 3 * d + 4 * HEADS:].astype(BF16)
    eye2 = jnp.eye(2, dtype=F32)
    wsd = jnp.stack([jnp.kron(eye2, w_s[0, g]) for g in range(HEADS)]).astype(BF16)
    bs2 = jnp.tile(b_s[0].T, (2, 1))
    proj_w = (row(norm1_g[0]), wqkv, wgi.astype(BF16), wgf.astype(BF16), bgi, bgf, conv_qk[0], wrest,
              row(sgu_ln_g[0]), row(sgu_ln_b[0]), wsd, bs2, w_branch_sgu[0].astype(BF16))

    pc = _projection(ctx2, mc[0], mc[1], *proj_w, tm=_tile_size(ctx2.shape[0], 512))
    c0 = jnp.zeros((2 * HEADS, HEAD_DIM, HEAD_DIM), F32)
    n0 = jnp.zeros((2 * HEADS, HEAD_DIM), F32)
    m0 = jnp.full((2 * HEADS, _GATE_LANES), M_INIT, F32)
    _, _, c1, n1, m1 = _mlstm(*pc[:5], c0, n0, m0)

    q, k, v, gc, gr, so, sgm, s = _projection(x2, mx[0], mx[1], *proj_w, tm=_tile_size(t, 512))
    hf, hb, _, _, _ = _mlstm(q, k, v, gc, gr, c1, n1, m1)
    hx, hn2 = _post(hf, hb, so, sgm, s, x2, row(head_norm_g[0]), w_branch_mlstm[0].astype(BF16),
                    w_out[0].astype(BF16), mx[2], row(norm2_g[0]), mx[3], mx[4], tm=_tile_size(t, 512))

    nch = _N_FF_CHUNKS
    w_up0 = w_up[0]
    wua = w_up0[:, :D_FF].reshape(d, nch, _FF_CHUNK).transpose(1, 0, 2).astype(BF16)
    wub = w_up0[:, D_FF:].reshape(d, nch, _FF_CHUNK).transpose(1, 0, 2).astype(BF16)
    cwf = w_ffn_conv[0].reshape(9, nch, _FF_CHUNK).transpose(1, 0, 2)
    wdn = w_down[0].reshape(nch, _FF_CHUNK, d).astype(BF16)
    out = _ffn(hn2, hx, wua, wub, cwf, wdn, mx[5], row(final_g), tm=_tile_size(t, 1024))
    return out[None]
```

```python
import functools

import jax
import jax.numpy as jnp
from jax import lax
from jax.experimental import pallas as pl
from jax.experimental.pallas import tpu as pltpu

F32 = jnp.float32
BF16 = jnp.bfloat16

D_MODEL = 1024
GRID_W = 64
HEADS = 4
HEAD_DIM = 256
N_MOD = 6
D_FF = 2816
EPS = 1e-6
M_INIT = -1e30

_L = 256
_HALO = 16
_FF_CHUNK = 256
_N_FF_CHUNKS = D_FF // _FF_CHUNK
_GATE_LANES = 128
_PAD = 8
_VMEM_LIMIT_V7X = 56 * 1024 * 1024

_GROUP_DIR = (0, 0, 1, 1, 0, 1, 0, 1, 0, 1)
_ROW_GROUP0 = 4


def _dot(a, b):
    return jnp.dot(a, b, preferred_element_type=F32)


def _dot_nt(a, b):
    return lax.dot_general(a, b, (((1,), (1,)), ((), ())), preferred_element_type=F32)


def _dot_tn(a, b):
    return lax.dot_general(a, b, (((0,), (0,)), ((), ())), preferred_element_type=F32)


def _split3(x):
    h1 = x.astype(BF16)
    r1 = x - h1.astype(F32)
    h2 = r1.astype(BF16)
    h3 = (r1 - h2.astype(F32)).astype(BF16)
    return h1, h2, h3


def _sigmoid(x):
    return 1.0 / (1.0 + jnp.exp(-x))


def _silu(x):
    return x * _sigmoid(x)


def _gelu_tanh(x):
    return 0.5 * x * (1.0 + jnp.tanh(0.7978845608028654 * (x + 0.044715 * (x * x * x))))


def _log_sigmoid(x):
    return jnp.minimum(x, 0.0) - jnp.log(1.0 + jnp.exp(-jnp.abs(x)))


def _rms(x):
    return x * lax.rsqrt(jnp.mean(x * x, axis=-1, keepdims=True) + EPS)


def _const_spec(shape):
    nd = len(shape)
    return pl.BlockSpec(shape, lambda *_: (0,) * nd, pipeline_mode=pl.Buffered(1))


def _mod_kernel(cc_ref, w_ref, b_ref, o_ref):
    s = _silu(cc_ref[...])
    sh, sm, sl = _split3(s)
    wh, wm, wl = _split3(w_ref[...])
    acc = _dot(sl, wh) + _dot(sh, wl) + _dot(sm, wm)
    acc = acc + _dot(sm, wh) + _dot(sh, wm)
    o_ref[...] = acc + _dot(sh, wh) + b_ref[...]


def _modulation(cc, w_mod, b_mod):
    n_out = w_mod.shape[1]
    nb = D_MODEL
    return pl.pallas_call(
        _mod_kernel,
        grid=(n_out // nb,),
        in_specs=[pl.BlockSpec((8, D_MODEL), lambda j: (0, 0)),
                  pl.BlockSpec((D_MODEL, nb), lambda j: (0, j)),
                  pl.BlockSpec((1, nb), lambda j: (0, j))],
        out_specs=pl.BlockSpec((8, nb), lambda j: (0, j)),
        out_shape=jax.ShapeDtypeStruct((8, n_out), F32),
        compiler_params=pltpu.CompilerParams(dimension_semantics=("arbitrary",)),
        name="modulation",
    )(cc, w_mod, b_mod)


def _gate_table(gi, lf):
    row = lax.broadcasted_iota(jnp.int32, (_L, _L), 0)
    col = lax.broadcasted_iota(jnp.int32, (_L, _L), 1)
    ltri = (col <= row).astype(BF16)
    l1, l2, l3 = _split3(lf)
    prefix = _dot(ltri, l3) + _dot(ltri, l2) + _dot(ltri, l1)
    total = prefix[_L - 1:_L, :]
    suffix = total - prefix + lf
    group = lax.broadcasted_iota(jnp.int32, (1, _GATE_LANES), 1) // HEADS
    is_bwd = functools.reduce(jnp.logical_or, [group == k for k, d in enumerate(_GROUP_DIR) if d == 1])
    cum = jnp.where(is_bwd, suffix, prefix)
    a = total - cum + gi
    mloc = jnp.max(a, axis=0, keepdims=True)
    table = jnp.where(group <= 3, jnp.where(group % 2 == 0, cum, a),
                      jnp.where(group <= 5, gi - cum,
                                jnp.where(group <= 7, jnp.broadcast_to(total, a.shape),
                                          jnp.broadcast_to(mloc, a.shape))))
    return table


def _proj_kernel(x_ref, xp_ref, xn_ref, shift_ref, scale_ref, g1_ref, wqkv_ref, wgi_ref, wgf_ref, bgi_ref, bgf_ref,
                 cw_ref, wrest_ref, lng_ref, lnb_ref, wsd_ref, bs_ref, wbs_ref,
                 q_ref, k_ref, v_ref, gc_ref, gr_ref, so_ref, sgm_ref, s_ref,
                 hn_s, z_s, ys_s, *, tm):
    i = pl.program_id(0)
    last = pl.num_programs(0) - 1
    scale1 = 1.0 + scale_ref[...]

    def norm_mod(xb):
        return ((_rms(xb) * g1_ref[...]) * scale1 + shift_ref[...]).astype(BF16)

    hn_s[0:_HALO, :] = norm_mod(xp_ref[...])
    hn_s[_HALO:_HALO + tm, :] = norm_mod(x_ref[...])
    hn_s[_HALO + tm:, :] = norm_mod(xn_ref[...])
    hn = hn_s[_HALO:_HALO + tm, :]

    for c, out_ref in enumerate((q_ref, k_ref)):
        z_s[...] = _dot(hn_s[...], wqkv_ref[:, c * D_MODEL:(c + 1) * D_MODEL])

        @pl.when(i == 0)
        def _():
            z_s[0:_HALO, :] = jnp.zeros((_HALO, D_MODEL), F32)

        @pl.when(i == last)
        def _():
            z_s[_HALO + tm:, :] = jnp.zeros((_HALO, D_MODEL), F32)

        cw = cw_ref[:, c * D_MODEL:(c + 1) * D_MODEL]
        y = (cw[0:1, :] * z_s[_HALO - 1:_HALO - 1 + tm, :] + cw[1:2, :] * z_s[_HALO:_HALO + tm, :]
             + cw[2:3, :] * z_s[_HALO + 1:_HALO + 1 + tm, :])
        y = _silu(y)
        if c == 0:
            y = y * (HEAD_DIM ** -0.5)
        out_ref[...] = y.astype(BF16)

    v_ref[...] = _dot(hn, wqkv_ref[:, 2 * D_MODEL:3 * D_MODEL]).astype(BF16)

    gi = _dot(hn, wgi_ref[...]) + bgi_ref[...]
    lf = _log_sigmoid(_dot(hn, wgf_ref[...]) + bgf_ref[...])
    for j in range(tm // _L):
        table = _gate_table(gi[j * _L:(j + 1) * _L, :], lf[j * _L:(j + 1) * _L, :])
        gc_ref[j * _L:(j + 1) * _L, :] = table
        gr_ref[:, j * _L:(j + 1) * _L] = table.T[_ROW_GROUP0 * HEADS:(_ROW_GROUP0 + 2) * HEADS, :]

    so_ref[...] = _sigmoid(_dot(hn, wrest_ref[:, 0:D_MODEL])).astype(BF16)
    sgm_ref[...] = _sigmoid(_dot(hn, wrest_ref[:, 3 * D_MODEL:4 * D_MODEL])).astype(BF16)

    u = _gelu_tanh(_dot(hn, wrest_ref[:, D_MODEL:2 * D_MODEL]))
    vg = _gelu_tanh(_dot(hn, wrest_ref[:, 2 * D_MODEL:3 * D_MODEL]))
    vg = vg - jnp.mean(vg, axis=-1, keepdims=True)
    vn = (vg * lax.rsqrt(jnp.mean(vg * vg, axis=-1, keepdims=True) + EPS) * lng_ref[...] + lnb_ref[...]).astype(BF16)
    for p in range(tm // _L):
        for g in range(HEADS):
            rows = slice(p * _L, (p + 1) * _L)
            cols = slice(g * HEAD_DIM, (g + 1) * HEAD_DIM)
            mixed = _dot(wsd_ref[g], vn[rows, cols]) + bs_ref[:, g:g + 1]
            ys_s[rows, cols] = (u[rows, cols] * mixed).astype(BF16)
    gg = _sigmoid(_dot(hn, wrest_ref[:, 4 * D_MODEL:5 * D_MODEL]))
    s_ref[...] = (gg * _dot(ys_s[...], wbs_ref[...])).astype(BF16)


def _projection(x, shift, scale, g1, wqkv, wgi, wgf, bgi, bgf, cw, wrest, lng, lnb, wsd, bs2, wbs, *, tm):
    t = x.shape[0]
    nt = t // tm
    hb = tm // _HALO
    nhb = t // _HALO
    tile = lambda w: pl.BlockSpec((tm, w), lambda i: (i, 0))
    in_specs = [
        tile(D_MODEL),
        pl.BlockSpec((_HALO, D_MODEL), lambda i: (jnp.maximum(i * hb - 1, 0), 0)),
        pl.BlockSpec((_HALO, D_MODEL), lambda i: (jnp.minimum((i + 1) * hb, nhb - 1), 0)),
    ] + [_const_spec(a.shape) for a in (shift, scale, g1, wqkv, wgi, wgf, bgi, bgf, cw, wrest, lng, lnb, wsd, bs2, wbs)]
    out_specs = [tile(D_MODEL), tile(D_MODEL), tile(D_MODEL), tile(_GATE_LANES),
                 pl.BlockSpec((2 * HEADS, tm), lambda i: (0, i)),
                 tile(D_MODEL), tile(D_MODEL), tile(D_MODEL)]
    act = jax.ShapeDtypeStruct((t, D_MODEL), BF16)
    out_shape = [act, act, act, jax.ShapeDtypeStruct((t, _GATE_LANES), F32),
                 jax.ShapeDtypeStruct((2 * HEADS, t), F32), act, act, act]
    return pl.pallas_call(
        functools.partial(_proj_kernel, tm=tm),
        grid=(nt,),
        in_specs=in_specs,
        out_specs=out_specs,
        out_shape=out_shape,
        scratch_shapes=[pltpu.VMEM((tm + 2 * _HALO, D_MODEL), BF16),
                        pltpu.VMEM((tm + 2 * _HALO, D_MODEL), F32),
                        pltpu.VMEM((tm, D_MODEL), BF16)],
        compiler_params=pltpu.CompilerParams(dimension_semantics=("parallel",), vmem_limit_bytes=_VMEM_LIMIT_V7X),
        name="projection",
    )(x, x, x, shift, scale, g1, wqkv, wgi, wgf, bgi, bgf, cw, wrest, lng, lnb, wsd, bs2, wbs)


def _mlstm_head(qh, kh, vh, bcol, acol, crow, g, mloc, m_prev, c_prev, n_prev, mask):
    inter = bcol + m_prev
    dmat = jnp.where(mask, bcol + crow, -jnp.inf)
    m_row = jnp.maximum(inter, jnp.max(dmat, axis=1, keepdims=True))
    s = _dot_nt(qh, kh) * jnp.exp(dmat - m_row)
    w_inter = jnp.exp(inter - m_row)
    num = _dot(s.astype(BF16), vh) + w_inter * _dot(qh, c_prev.astype(BF16))
    den = jnp.sum(s, axis=1, keepdims=True) + w_inter * jnp.sum(qh.astype(F32) * n_prev, axis=1, keepdims=True)
    h = num / jnp.maximum(jnp.abs(den), jnp.exp(-m_row))
    m_new = jnp.maximum(g + m_prev, mloc)
    a_old = jnp.exp(g + m_prev - m_new)
    kw = kh.astype(F32) * jnp.exp(acol - m_new)
    c_new = a_old * c_prev + _dot_tn(kw.astype(BF16), vh)
    n_new = a_old * n_prev + jnp.sum(kw, axis=0, keepdims=True)
    return h, c_new, n_new, m_new


def _mlstm_kernel(qf_ref, kf_ref, vf_ref, gcf_ref, grf_ref, qb_ref, kb_ref, vb_ref, gcb_ref, grb_ref,
                  c0_ref, n0_ref, m0_ref, hf_ref, hb_ref, c_ref, n_ref, m_ref):
    @pl.when(pl.program_id(0) == 0)
    def _():
        c_ref[...] = c0_ref[...]
        n_ref[...] = n0_ref[...]
        m_ref[...] = m0_ref[...]

    row = lax.broadcasted_iota(jnp.int32, (_L, _L), 0)
    col = lax.broadcasted_iota(jnp.int32, (_L, _L), 1)
    dirs = ((qf_ref, kf_ref, vf_ref, gcf_ref, grf_ref, hf_ref, col <= row),
            (qb_ref, kb_ref, vb_ref, gcb_ref, grb_ref, hb_ref, col >= row))
    for d, (q_ref, k_ref, v_ref, gc_ref, gr_ref, h_ref, mask) in enumerate(dirs):
        gc = gc_ref[...]
        gr = gr_ref[...]
        for hd in range(HEADS):
            lane = lambda grp: slice((grp + d) * HEADS + hd, (grp + d) * HEADS + hd + 1)
            cols = slice(hd * HEAD_DIM, (hd + 1) * HEAD_DIM)
            st = d * HEADS + hd
            h, c_new, n_new, m_new = _mlstm_head(
                q_ref[:, cols], k_ref[:, cols], v_ref[:, cols],
                gc[:, (2 * d) * HEADS + hd:(2 * d) * HEADS + hd + 1],
                gc[:, (2 * d + 1) * HEADS + hd:(2 * d + 1) * HEADS + hd + 1],
                gr[st:st + 1, :],
                gc[0:1, lane(6)], gc[0:1, lane(8)],
                m_ref[st:st + 1, 0:1], c_ref[st], n_ref[st:st + 1, :], mask)
            h_ref[:, cols] = h.astype(BF16)
            c_ref[st] = c_new
            n_ref[st:st + 1, :] = n_new
            m_ref[st:st + 1, :] = jnp.broadcast_to(m_new, (1, _GATE_LANES))


def _mlstm(q, k, v, gc, gr, c0, n0, m0):
    t = q.shape[0]
    nc = t // _L
    fwd = lambda w: pl.BlockSpec((_L, w), lambda i: (i, 0))
    bwd = lambda w: pl.BlockSpec((_L, w), lambda i: (nc - 1 - i, 0))
    in_specs = [fwd(D_MODEL), fwd(D_MODEL), fwd(D_MODEL), fwd(_GATE_LANES),
                pl.BlockSpec((2 * HEADS, _L), lambda i: (0, i)),
                bwd(D_MODEL), bwd(D_MODEL), bwd(D_MODEL), bwd(_GATE_LANES),
                pl.BlockSpec((2 * HEADS, _L), lambda i: (0, nc - 1 - i)),
                _const_spec(c0.shape), _const_spec(n0.shape), _const_spec(m0.shape)]
    state_specs = [pl.BlockSpec(a.shape, lambda i, nd=a.ndim: (0,) * nd) for a in (c0, n0, m0)]
    act = jax.ShapeDtypeStruct((t, D_MODEL), BF16)
    return pl.pallas_call(
        _mlstm_kernel,
        grid=(nc,),
        in_specs=in_specs,
        out_specs=[fwd(D_MODEL), bwd(D_MODEL)] + state_specs,
        out_shape=[act, act] + [jax.ShapeDtypeStruct(a.shape, F32) for a in (c0, n0, m0)],
        compiler_params=pltpu.CompilerParams(dimension_semantics=("arbitrary",), vmem_limit_bytes=_VMEM_LIMIT_V7X),
        name="mlstm",
    )(q, k, v, gc, gr, q, k, v, gc, gr, c0, n0, m0)


def _post_kernel(hf_ref, hb_ref, so_ref, sgm_ref, s_ref, x_ref, hg_ref, wbm_ref, wout_ref, gate_ref,
                 g2_ref, shift_ref, scale_ref, hx_ref, hn2_ref, ym_s):
    hm = hf_ref[...].astype(F32) + hb_ref[...].astype(F32)
    for hd in range(HEADS):
        cols = slice(hd * HEAD_DIM, (hd + 1) * HEAD_DIM)
        ym = _rms(hm[:, cols]) * hg_ref[:, cols]
        ym_s[:, cols] = (so_ref[:, cols].astype(F32) * ym).astype(BF16)
    y = sgm_ref[...].astype(F32) * _dot(ym_s[...], wbm_ref[...]) + s_ref[...].astype(F32)
    hx = x_ref[...] + gate_ref[...] * _dot(y.astype(BF16), wout_ref[...])
    hx_ref[...] = hx
    hn2_ref[...] = ((_rms(hx) * g2_ref[...]) * (1.0 + scale_ref[...]) + shift_ref[...]).astype(BF16)


def _post(hf, hb, so, sgm, s, x, hg, wbm, wout, gate, g2, shift, scale, *, tm):
    t = x.shape[0]
    tile = lambda: pl.BlockSpec((tm, D_MODEL), lambda i: (i, 0))
    return pl.pallas_call(
        _post_kernel,
        grid=(t // tm,),
        in_specs=[tile() for _ in range(6)] + [_const_spec(a.shape) for a in (hg, wbm, wout, gate, g2, shift, scale)],
        out_specs=[tile(), tile()],
        out_shape=[jax.ShapeDtypeStruct((t, D_MODEL), F32), jax.ShapeDtypeStruct((t, D_MODEL), BF16)],
        scratch_shapes=[pltpu.VMEM((tm, D_MODEL), BF16)],
        compiler_params=pltpu.CompilerParams(dimension_semantics=("parallel",), vmem_limit_bytes=_VMEM_LIMIT_V7X),
        name="post_mixer",
    )(hf, hb, so, sgm, s, x, hg, wbm, wout, gate, g2, shift, scale)


def _ffn_kernel(hn_ref, hnp_ref, hnn_ref, hx_ref, wab_ref, cw_ref, wdn_ref, gate_ref, fg_ref, o_ref,
                hne_s, a_s, b_s, act_s, acc_s, *, tm):
    i = pl.program_id(0)
    has_above = i > 0
    has_below = i < pl.num_programs(0) - 1
    ext = tm + 2 * GRID_W
    hne_s[0:GRID_W, :] = hnp_ref[...]
    hne_s[GRID_W:GRID_W + tm, :] = hn_ref[...]
    hne_s[GRID_W + tm:, :] = hnn_ref[...]
    for slot in range(2):
        a_s[slot, 0:_PAD, :] = jnp.zeros((_PAD, _FF_CHUNK), F32)
        a_s[slot, _PAD + ext:, :] = jnp.zeros((_PAD, _FF_CHUNK), F32)
    gcol = lax.broadcasted_iota(jnp.int32, (ext, 1), 0) % GRID_W

    def up(c):
        slot = c % 2
        ab = _dot(hne_s[...], wab_ref[c])
        a = ab[:, :_FF_CHUNK]
        a_s[slot, _PAD:_PAD + GRID_W, :] = jnp.where(has_above, a[:GRID_W, :], 0.0)
        a_s[slot, _PAD + GRID_W:_PAD + GRID_W + tm, :] = a[GRID_W:GRID_W + tm, :]
        a_s[slot, _PAD + GRID_W + tm:_PAD + ext, :] = jnp.where(has_below, a[GRID_W + tm:, :], 0.0)
        b_s[slot] = ab[GRID_W:GRID_W + tm, _FF_CHUNK:]

    def mix(c):
        slot = c % 2
        taps = (jnp.where(gcol >= 1, a_s[slot, _PAD - 1:_PAD - 1 + ext, :], 0.0),
                a_s[slot, _PAD:_PAD + ext, :],
                jnp.where(gcol <= GRID_W - 2, a_s[slot, _PAD + 1:_PAD + 1 + ext, :], 0.0))
        cw = cw_ref[c]
        conv = None
        for dr in range(3):
            for dc in range(3):
                term = cw[3 * dr + dc:3 * dr + dc + 1, :] * taps[dc][dr * GRID_W:dr * GRID_W + tm, :]
                conv = term if conv is None else conv + term
        act_s[slot] = (_silu(conv) * b_s[slot]).astype(BF16)

    def down(c):
        part = _dot(act_s[c % 2], wdn_ref[c])
        if c == 0:
            acc_s[...] = part
        else:
            acc_s[...] += part

    for step in range(_N_FF_CHUNKS + 2):
        if 0 <= step - 2:
            down(step - 2)
        if step < _N_FF_CHUNKS:
            up(step)
        if 0 <= step - 1 < _N_FF_CHUNKS:
            mix(step - 1)
    h2 = hx_ref[...] + gate_ref[...] * acc_s[...]
    o_ref[...] = _rms(h2) * fg_ref[...]


def _ffn(hn2, hx, wab, cwf, wdn, gate, fg, *, tm):
    t = hx.shape[0]
    rb = tm // GRID_W
    nrb = t // GRID_W
    tile = lambda: pl.BlockSpec((tm, D_MODEL), lambda i: (i, 0))
    ext = tm + 2 * GRID_W
    return pl.pallas_call(
        functools.partial(_ffn_kernel, tm=tm),
        grid=(t // tm,),
        in_specs=[tile(),
                  pl.BlockSpec((GRID_W, D_MODEL), lambda i: (jnp.maximum(i * rb - 1, 0), 0)),
                  pl.BlockSpec((GRID_W, D_MODEL), lambda i: (jnp.minimum((i + 1) * rb, nrb - 1), 0)),
                  tile()] + [_const_spec(a.shape) for a in (wab, cwf, wdn, gate, fg)],
        out_specs=tile(),
        out_shape=jax.ShapeDtypeStruct((t, D_MODEL), F32),
        scratch_shapes=[pltpu.VMEM((ext, D_MODEL), BF16),
                        pltpu.VMEM((2, ext + 2 * _PAD, _FF_CHUNK), F32),
                        pltpu.VMEM((2, tm, _FF_CHUNK), F32),
                        pltpu.VMEM((2, tm, _FF_CHUNK), BF16),
                        pltpu.VMEM((tm, D_MODEL), F32)],
        compiler_params=pltpu.CompilerParams(dimension_semantics=("parallel",), vmem_limit_bytes=_VMEM_LIMIT_V7X),
        name="conv_ffn",
    )(hn2, hn2, hn2, hx, wab, cwf, wdn, gate, fg)


def _gate_columns(wg):
    idx_i = [2 * d * HEADS + h for d in _GROUP_DIR for h in range(HEADS)]
    idx_f = [(2 * d + 1) * HEADS + h for d in _GROUP_DIR for h in range(HEADS)]
    pad = _GATE_LANES - len(idx_i)
    take = lambda idx: jnp.pad(wg[..., jnp.array(idx)], [(0, 0)] * (wg.ndim - 1) + [(0, pad)])
    return take(idx_i), take(idx_f)


def _tile_size(t, want):
    tm = min(want, t)
    assert t % tm == 0 and tm % _L == 0, (t, tm)
    return tm


def kernel(x, c, ctx, c_ctx, w_mod, b_mod, norm1_g, w_in, b_gate, conv_qk, head_norm_g, sgu_ln_g, sgu_ln_b, w_s, b_s,
           w_branch_mlstm, w_branch_sgu, w_out, norm2_g, w_up, w_ffn_conv, w_down, final_g):
    assert x.shape[0] == 1 and w_mod.shape[0] == 1, "single batch element, single layer"
    d = D_MODEL
    x2, ctx2 = x[0], ctx[0]
    t = x2.shape[0]
    assert t % GRID_W == 0 and ctx2.shape[0] % _L == 0
    row = lambda a: a.reshape(1, -1)

    cc = jnp.zeros((8, d), F32).at[0].set(c[0]).at[1].set(c_ctx)
    mod = _modulation(cc, w_mod[0], row(b_mod[0]))
    mx = [mod[0:1, j * d:(j + 1) * d] for j in range(N_MOD)]
    mc = [mod[1:2, j * d:(j + 1) * d] for j in range(N_MOD)]

    w_in0 = w_in[0]
    wqkv = w_in0[:, :3 * d].astype(BF16)
    wgi, wgf = _gate_columns(w_in0[:, 3 * d:3 * d + 4 * HEADS])
    bgi, bgf = _gate_columns(row(b_gate[0]))
    wrest = w_in0[:, 3 * d + 4 * HEADS:].astype(BF16)
    eye2 = jnp.eye(2, dtype=F32)
    wsd = jnp.stack([jnp.kron(eye2, w_s[0, g]) for g in range(HEADS)]).astype(BF16)
    bs2 = jnp.tile(b_s[0].T, (2, 1))
    proj_w = (row(norm1_g[0]), wqkv, wgi.astype(BF16), wgf.astype(BF16), bgi, bgf, conv_qk[0], wrest,
              row(sgu_ln_g[0]), row(sgu_ln_b[0]), wsd, bs2, w_branch_sgu[0].astype(BF16))

    pc = _projection(ctx2, mc[0], mc[1], *proj_w, tm=_tile_size(ctx2.shape[0], 512))
    c0 = jnp.zeros((2 * HEADS, HEAD_DIM, HEAD_DIM), F32)
    n0 = jnp.zeros((2 * HEADS, HEAD_DIM), F32)
    m0 = jnp.full((2 * HEADS, _GATE_LANES), M_INIT, F32)
    _, _, c1, n1, m1 = _mlstm(*pc[:5], c0, n0, m0)

    q, k, v, gc, gr, so, sgm, s = _projection(x2, mx[0], mx[1], *proj_w, tm=_tile_size(t, 512))
    hf, hb, _, _, _ = _mlstm(q, k, v, gc, gr, c1, n1, m1)
    hx, hn2 = _post(hf, hb, so, sgm, s, x2, row(head_norm_g[0]), w_branch_mlstm[0].astype(BF16),
                    w_out[0].astype(BF16), mx[2], row(norm2_g[0]), mx[3], mx[4], tm=_tile_size(t, 512))

    nch = _N_FF_CHUNKS
    w_up0 = w_up[0]
    wab = w_up0.reshape(d, 2, nch, _FF_CHUNK).transpose(2, 0, 1, 3).reshape(nch, d, 2 * _FF_CHUNK).astype(BF16)
    cwf = w_ffn_conv[0].reshape(9, nch, _FF_CHUNK).transpose(1, 0, 2)
    wdn = w_down[0].reshape(nch, _FF_CHUNK, d).astype(BF16)
    out = _ffn(hn2, hx, wab, cwf, wdn, mx[5], row(final_g), tm=_tile_size(t, 1024))
    return out[None]
```

```python
import functools

import jax
import jax.numpy as jnp
from jax import lax
from jax.experimental import pallas as pl
from jax.experimental.pallas import tpu as pltpu

F32 = jnp.float32
BF16 = jnp.bfloat16

D_MODEL = 1024
GRID_W = 64
HEADS = 4
HEAD_DIM = 256
N_MOD = 6
D_FF = 2816
EPS = 1e-6
M_INIT = -1e30

_L = 256
_HALO = 16
_FF_CHUNK = 256
_N_FF_CHUNKS = D_FF // _FF_CHUNK
_GATE_LANES = 128
_PAD = 8
_VMEM_LIMIT_V7X = 56 * 1024 * 1024

_N_STATES = 2 * HEADS
_TAB_W = 3 * _GATE_LANES
_C_EXT = HEAD_DIM + _GATE_LANES


def _dot(a, b):
    return jnp.dot(a, b, preferred_element_type=F32)


def _split3(x):
    h1 = x.astype(BF16)
    r1 = x - h1.astype(F32)
    h2 = r1.astype(BF16)
    h3 = (r1 - h2.astype(F32)).astype(BF16)
    return h1, h2, h3


def _sigmoid(x):
    return 1.0 / (1.0 + jnp.exp(-x))


def _silu(x):
    return x * _sigmoid(x)


def _gelu_tanh(x):
    return 0.5 * x * (1.0 + jnp.tanh(0.7978845608028654 * (x + 0.044715 * (x * x * x))))


def _log_sigmoid(x):
    return jnp.minimum(x, 0.0) - jnp.log(1.0 + jnp.exp(-jnp.abs(x)))


def _rms(x):
    return x * lax.rsqrt(jnp.mean(x * x, axis=-1, keepdims=True) + EPS)


def _const_spec(shape):
    nd = len(shape)
    return pl.BlockSpec(shape, lambda *_: (0,) * nd, pipeline_mode=pl.Buffered(1))


def _mod_kernel(cc_ref, w_ref, b_ref, o_ref):
    s = _silu(cc_ref[...])
    sh, sm, sl = _split3(s)
    wh, wm, wl = _split3(w_ref[...])
    acc = _dot(sl, wh) + _dot(sh, wl) + _dot(sm, wm)
    acc = acc + _dot(sm, wh) + _dot(sh, wm)
    o_ref[...] = acc + _dot(sh, wh) + b_ref[...]


def _modulation(cc, w_mod, b_mod):
    n_out = w_mod.shape[1]
    nb = D_MODEL
    return pl.pallas_call(
        _mod_kernel,
        grid=(n_out // nb,),
        in_specs=[pl.BlockSpec((8, D_MODEL), lambda j: (0, 0)),
                  pl.BlockSpec((D_MODEL, nb), lambda j: (0, j)),
                  pl.BlockSpec((1, nb), lambda j: (0, j))],
        out_specs=pl.BlockSpec((8, nb), lambda j: (0, j)),
        out_shape=jax.ShapeDtypeStruct((8, n_out), F32),
        compiler_params=pltpu.CompilerParams(dimension_semantics=("arbitrary",)),
        name="modulation",
    )(cc, w_mod, b_mod)


def _gate_tables(gi, lf):
    row = lax.broadcasted_iota(jnp.int32, (_L, _L), 0)
    col = lax.broadcasted_iota(jnp.int32, (_L, _L), 1)
    ltri = (col <= row).astype(BF16)
    l1, l2, l3 = _split3(lf)
    prefix = _dot(ltri, l3) + _dot(ltri, l2) + _dot(ltri, l1)
    total = prefix[_L - 1:_L, :]
    suffix = total - prefix + lf
    lane = lax.broadcasted_iota(jnp.int32, (1, _GATE_LANES), 1)
    cum = jnp.where(lane >= HEADS, suffix, prefix)
    a = total - cum + gi
    c = gi - cum
    c_rows = c.T[0:_N_STATES, :]
    a_rows = a.T[0:_N_STATES, :]
    pos = lax.broadcasted_iota(jnp.int32, (_N_STATES, _L), 1)
    fwd_max, bwd_max = c_rows, c_rows
    k = 1
    while k < _L:
        fwd_max = jnp.maximum(fwd_max, jnp.where(pos >= k, pltpu.roll(fwd_max, k, 1), -jnp.inf))
        bwd_max = jnp.maximum(bwd_max, jnp.where(pos < _L - k, pltpu.roll(bwd_max, _L - k, 1), -jnp.inf))
        k *= 2
    state = lax.broadcasted_iota(jnp.int32, (_N_STATES, _L), 0)
    run_max = jnp.where(state < HEADS, fwd_max, bwd_max)
    run_max_cols = jnp.concatenate([run_max, jnp.zeros((_GATE_LANES - _N_STATES, _L), F32)], axis=0).T
    return cum, cum + run_max_cols, a, c_rows, a_rows, total, jnp.max(a, axis=0, keepdims=True)


def _proj_kernel(x_ref, xp_ref, xn_ref, shift_ref, scale_ref, g1_ref, wqkv_ref, wgi_ref, wgf_ref, bgi_ref, bgf_ref,
                 cw_ref, wrest_ref, lng_ref, lnb_ref, wsd_ref, bs_ref, wbs_ref,
                 q_ref, kt_ref, v_ref, tab_ref, gr_ref, gm_ref, so_ref, sgm_ref, s_ref,
                 hn_s, z_s, ys_s, *, tm):
    i = pl.program_id(0)
    has_prev = i > 0
    has_next = i < pl.num_programs(0) - 1
    scale1 = 1.0 + scale_ref[...]

    def norm_mod(xb):
        return ((_rms(xb) * g1_ref[...]) * scale1 + shift_ref[...]).astype(BF16)

    hn_s[0:_HALO, :] = norm_mod(xp_ref[...])
    hn_s[_HALO:_HALO + tm, :] = norm_mod(x_ref[...])
    hn_s[_HALO + tm:, :] = norm_mod(xn_ref[...])
    hn = hn_s[_HALO:_HALO + tm, :]

    for c in range(2):
        z = _dot(hn_s[...], wqkv_ref[:, c * D_MODEL:(c + 1) * D_MODEL])
        z_s[c, 0:_HALO, :] = jnp.where(has_prev, z[0:_HALO, :], 0.0)
        z_s[c, _HALO:_HALO + tm, :] = z[_HALO:_HALO + tm, :]
        z_s[c, _HALO + tm:, :] = jnp.where(has_next, z[_HALO + tm:, :], 0.0)
        cw = cw_ref[:, c * D_MODEL:(c + 1) * D_MODEL]
        y = (cw[0:1, :] * z_s[c, _HALO - 1:_HALO - 1 + tm, :] + cw[1:2, :] * z_s[c, _HALO:_HALO + tm, :]
             + cw[2:3, :] * z_s[c, _HALO + 1:_HALO + 1 + tm, :])
        y = _silu(y)
        if c == 0:
            q_ref[...] = (y * (HEAD_DIM ** -0.5)).astype(BF16)
        else:
            kt_ref[...] = y.T.astype(BF16)

    v_ref[...] = _dot(hn, wqkv_ref[:, 2 * D_MODEL:3 * D_MODEL]).astype(BF16)

    gi = _dot(hn, wgi_ref[...]) + bgi_ref[...]
    lf = _log_sigmoid(_dot(hn, wgf_ref[...]) + bgf_ref[...])
    for j in range(tm // _L):
        rows = slice(j * _L, (j + 1) * _L)
        cum, mi, a, c_rows, a_rows, total, a_max = _gate_tables(gi[rows, :], lf[rows, :])
        tab_ref[rows, 0:_GATE_LANES] = cum
        tab_ref[rows, _GATE_LANES:2 * _GATE_LANES] = mi
        tab_ref[rows, 2 * _GATE_LANES:] = a
        gr_ref[0:_N_STATES, rows] = c_rows
        gr_ref[_N_STATES:, rows] = a_rows
        gm_ref[j * 8:(j + 1) * 8, :] = jnp.concatenate([total, a_max, jnp.zeros((6, _GATE_LANES), F32)], axis=0)

    so_ref[...] = _sigmoid(_dot(hn, wrest_ref[:, 0:D_MODEL])).astype(BF16)
    sgm_ref[...] = _sigmoid(_dot(hn, wrest_ref[:, 3 * D_MODEL:4 * D_MODEL])).astype(BF16)

    u = _gelu_tanh(_dot(hn, wrest_ref[:, D_MODEL:2 * D_MODEL]))
    vg = _gelu_tanh(_dot(hn, wrest_ref[:, 2 * D_MODEL:3 * D_MODEL]))
    vg = vg - jnp.mean(vg, axis=-1, keepdims=True)
    vn = (vg * lax.rsqrt(jnp.mean(vg * vg, axis=-1, keepdims=True) + EPS) * lng_ref[...] + lnb_ref[...]).astype(BF16)
    for p in range(tm // _L):
        for g in range(HEADS):
            rows = slice(p * _L, (p + 1) * _L)
            cols = slice(g * HEAD_DIM, (g + 1) * HEAD_DIM)
            mixed = _dot(wsd_ref[g], vn[rows, cols]) + bs_ref[:, g:g + 1]
            ys_s[rows, cols] = (u[rows, cols] * mixed).astype(BF16)
    gg = _sigmoid(_dot(hn, wrest_ref[:, 4 * D_MODEL:5 * D_MODEL]))
    s_ref[...] = (gg * _dot(ys_s[...], wbs_ref[...])).astype(BF16)


def _projection(x, shift, scale, g1, wqkv, wgi, wgf, bgi, bgf, cw, wrest, lng, lnb, wsd, bs2, wbs, *, tm):
    t = x.shape[0]
    nt = t // tm
    hb = tm // _HALO
    nhb = t // _HALO
    tile = lambda w: pl.BlockSpec((tm, w), lambda i: (i, 0))
    in_specs = [
        tile(D_MODEL),
        pl.BlockSpec((_HALO, D_MODEL), lambda i: (jnp.maximum(i * hb - 1, 0), 0)),
        pl.BlockSpec((_HALO, D_MODEL), lambda i: (jnp.minimum((i + 1) * hb, nhb - 1), 0)),
    ] + [_const_spec(a.shape) for a in (shift, scale, g1, wqkv, wgi, wgf, bgi, bgf, cw, wrest, lng, lnb, wsd, bs2, wbs)]
    gm_rows = 8 * (tm // _L)
    out_specs = [tile(D_MODEL), pl.BlockSpec((D_MODEL, tm), lambda i: (0, i)), tile(D_MODEL), tile(_TAB_W),
                 pl.BlockSpec((2 * _N_STATES, tm), lambda i: (0, i)),
                 pl.BlockSpec((gm_rows, _GATE_LANES), lambda i: (i, 0)),
                 tile(D_MODEL), tile(D_MODEL), tile(D_MODEL)]
    act = jax.ShapeDtypeStruct((t, D_MODEL), BF16)
    out_shape = [act, jax.ShapeDtypeStruct((D_MODEL, t), BF16), act, jax.ShapeDtypeStruct((t, _TAB_W), F32),
                 jax.ShapeDtypeStruct((2 * _N_STATES, t), F32), jax.ShapeDtypeStruct((8 * (t // _L), _GATE_LANES), F32),
                 act, act, act]
    return pl.pallas_call(
        functools.partial(_proj_kernel, tm=tm),
        grid=(nt,),
        in_specs=in_specs,
        out_specs=out_specs,
        out_shape=out_shape,
        scratch_shapes=[pltpu.VMEM((tm + 2 * _HALO, D_MODEL), BF16),
                        pltpu.VMEM((2, tm + 2 * _HALO, D_MODEL), F32),
                        pltpu.VMEM((tm, D_MODEL), BF16)],
        compiler_params=pltpu.CompilerParams(dimension_semantics=("parallel",), vmem_limit_bytes=_VMEM_LIMIT_V7X),
        name="projection",
    )(x, x, x, shift, scale, g1, wqkv, wgi, wgf, bgi, bgf, cw, wrest, lng, lnb, wsd, bs2, wbs)


def _mlstm_kernel(qf_ref, ktf_ref, vf_ref, tabf_ref, grf_ref, gmf_ref, qb_ref, ktb_ref, vb_ref, tabb_ref, grb_ref, gmb_ref,
                  c0_ref, m0_ref, hf_ref, hb_ref, c_ref, m_ref):
    @pl.when(pl.program_id(0) == 0)
    def _():
        c_ref[...] = c0_ref[...]
        m_ref[...] = m0_ref[...]

    row = lax.broadcasted_iota(jnp.int32, (_L, _L), 0)
    col = lax.broadcasted_iota(jnp.int32, (_L, _L), 1)
    ones = jnp.ones((_L, _GATE_LANES), BF16)
    m_prev = m_ref[...]
    m_next = []
    dirs = ((qf_ref, ktf_ref, vf_ref, tabf_ref, grf_ref, gmf_ref, hf_ref, col <= row),
            (qb_ref, ktb_ref, vb_ref, tabb_ref, grb_ref, gmb_ref, hb_ref, col >= row))
    for d, (q_ref, kt_ref, v_ref, tab_ref, gr_ref, gm_ref, h_ref, mask) in enumerate(dirs):
        cum = tab_ref[:, 0:_GATE_LANES]
        m_row = jnp.maximum(cum + m_prev, tab_ref[:, _GATE_LANES:2 * _GATE_LANES])
        e = cum - m_row
        total, a_max = gm_ref[0:1, :], gm_ref[1:2, :]
        m_new = jnp.maximum(total + m_prev, a_max)
        a_old = jnp.exp(total + m_prev - m_new)
        m_next.append(m_new)
        gr = gr_ref[...]
        for hd in range(HEADS):
            j = d * HEADS + hd
            cols = slice(hd * HEAD_DIM, (hd + 1) * HEAD_DIM)
            qh, kth = q_ref[:, cols], kt_ref[cols, :]
            v_ext = jnp.concatenate([v_ref[:, cols], ones], axis=1)
            e_b = jnp.broadcast_to(e[:, j:j + 1], (_L, _GATE_LANES))
            m_row_b = jnp.broadcast_to(m_row[:, j:j + 1], (_L, _GATE_LANES))
            c_row, a_row = gr[j:j + 1, :], gr[_N_STATES + j:_N_STATES + j + 1, :]
            p = jnp.exp(jnp.where(mask, jnp.concatenate([e_b, e_b], axis=1) + c_row, -jnp.inf))
            s = (_dot(qh, kth) * p).astype(BF16)
            c_prev = c_ref[j]
            w_inter = jnp.exp(e_b + m_prev[:, j:j + 1])
            num = _dot(s, v_ext) + jnp.concatenate([w_inter] * 3, axis=1) * _dot(qh, c_prev.astype(BF16))
            inv = 1.0 / jnp.maximum(jnp.abs(num[:, HEAD_DIM:]), jnp.exp(-m_row_b))
            h_ref[:, cols] = (num[:, :HEAD_DIM] * jnp.concatenate([inv, inv], axis=1)).astype(BF16)
            kw_t = (kth.astype(F32) * jnp.exp(a_row - m_new[:, j:j + 1])).astype(BF16)
            c_ref[j] = a_old[:, j:j + 1] * c_prev + _dot(kw_t, v_ext)
    lane = lax.broadcasted_iota(jnp.int32, (1, _GATE_LANES), 1)
    m_ref[...] = jnp.where(lane < HEADS, m_next[0], m_next[1])


def _mlstm(q, kt, v, tab, gr, gm, c0, m0):
    t = q.shape[0]
    nc = t // _L
    fwd_i = lambda i: i
    bwd_i = lambda i: nc - 1 - i

    def specs(ix):
        rows = lambda w: pl.BlockSpec((_L, w), lambda i: (ix(i), 0))
        return [rows(D_MODEL), pl.BlockSpec((D_MODEL, _L), lambda i: (0, ix(i))), rows(D_MODEL), rows(_TAB_W),
                pl.BlockSpec((2 * _N_STATES, _L), lambda i: (0, ix(i))),
                pl.BlockSpec((8, _GATE_LANES), lambda i: (ix(i), 0))]

    out_rows = lambda ix: pl.BlockSpec((_L, D_MODEL), lambda i: (ix(i), 0))
    state_specs = [pl.BlockSpec(a.shape, lambda i, nd=a.ndim: (0,) * nd) for a in (c0, m0)]
    act = jax.ShapeDtypeStruct((t, D_MODEL), BF16)
    return pl.pallas_call(
        _mlstm_kernel,
        grid=(nc,),
        in_specs=specs(fwd_i) + specs(bwd_i) + [_const_spec(c0.shape), _const_spec(m0.shape)],
        out_specs=[out_rows(fwd_i), out_rows(bwd_i)] + state_specs,
        out_shape=[act, act] + [jax.ShapeDtypeStruct(a.shape, F32) for a in (c0, m0)],
        compiler_params=pltpu.CompilerParams(dimension_semantics=("arbitrary",), vmem_limit_bytes=_VMEM_LIMIT_V7X),
        name="mlstm",
    )(q, kt, v, tab, gr, gm, q, kt, v, tab, gr, gm, c0, m0)


def _post_kernel(hf_ref, hb_ref, so_ref, sgm_ref, s_ref, x_ref, hg_ref, wbm_ref, wout_ref, gate_ref,
                 g2_ref, shift_ref, scale_ref, hx_ref, hn2_ref, ym_s):
    hm = hf_ref[...].astype(F32) + hb_ref[...].astype(F32)
    for hd in range(HEADS):
        cols = slice(hd * HEAD_DIM, (hd + 1) * HEAD_DIM)
        ym = _rms(hm[:, cols]) * hg_ref[:, cols]
        ym_s[:, cols] = (so_ref[:, cols].astype(F32) * ym).astype(BF16)
    y = sgm_ref[...].astype(F32) * _dot(ym_s[...], wbm_ref[...]) + s_ref[...].astype(F32)
    hx = x_ref[...] + gate_ref[...] * _dot(y.astype(BF16), wout_ref[...])
    hx_ref[...] = hx
    hn2_ref[...] = ((_rms(hx) * g2_ref[...]) * (1.0 + scale_ref[...]) + shift_ref[...]).astype(BF16)


def _post(hf, hb, so, sgm, s, x, hg, wbm, wout, gate, g2, shift, scale, *, tm):
    t = x.shape[0]
    tile = lambda: pl.BlockSpec((tm, D_MODEL), lambda i: (i, 0))
    return pl.pallas_call(
        _post_kernel,
        grid=(t // tm,),
        in_specs=[tile() for _ in range(6)] + [_const_spec(a.shape) for a in (hg, wbm, wout, gate, g2, shift, scale)],
        out_specs=[tile(), tile()],
        out_shape=[jax.ShapeDtypeStruct((t, D_MODEL), F32), jax.ShapeDtypeStruct((t, D_MODEL), BF16)],
        scratch_shapes=[pltpu.VMEM((tm, D_MODEL), BF16)],
        compiler_params=pltpu.CompilerParams(dimension_semantics=("parallel",), vmem_limit_bytes=_VMEM_LIMIT_V7X),
        name="post_mixer",
    )(hf, hb, so, sgm, s, x, hg, wbm, wout, gate, g2, shift, scale)


def _ffn_kernel(hn_ref, hnp_ref, hnn_ref, hx_ref, wab_ref, cw_ref, wdn_ref, gate_ref, fg_ref, o_ref,
                hne_s, a_s, b_s, act_s, acc_s, *, tm):
    i = pl.program_id(0)
    has_above = i > 0
    has_below = i < pl.num_programs(0) - 1
    ext = tm + 2 * GRID_W
    hne_s[0:GRID_W, :] = hnp_ref[...]
    hne_s[GRID_W:GRID_W + tm, :] = hn_ref[...]
    hne_s[GRID_W + tm:, :] = hnn_ref[...]
    for slot in range(2):
        a_s[slot, 0:_PAD, :] = jnp.zeros((_PAD, _FF_CHUNK), F32)
        a_s[slot, _PAD + ext:, :] = jnp.zeros((_PAD, _FF_CHUNK), F32)
    gcol = lax.broadcasted_iota(jnp.int32, (ext, 1), 0) % GRID_W

    def up(c):
        slot = c % 2
        ab = _dot(hne_s[...], wab_ref[c])
        a = ab[:, :_FF_CHUNK]
        a_s[slot, _PAD:_PAD + GRID_W, :] = jnp.where(has_above, a[:GRID_W, :], 0.0)
        a_s[slot, _PAD + GRID_W:_PAD + GRID_W + tm, :] = a[GRID_W:GRID_W + tm, :]
        a_s[slot, _PAD + GRID_W + tm:_PAD + ext, :] = jnp.where(has_below, a[GRID_W + tm:, :], 0.0)
        b_s[slot] = ab[GRID_W:GRID_W + tm, _FF_CHUNK:]

    def mix(c):
        slot = c % 2
        taps = (jnp.where(gcol >= 1, a_s[slot, _PAD - 1:_PAD - 1 + ext, :], 0.0),
                a_s[slot, _PAD:_PAD + ext, :],
                jnp.where(gcol <= GRID_W - 2, a_s[slot, _PAD + 1:_PAD + 1 + ext, :], 0.0))
        cw = cw_ref[c]
        conv = None
        for dr in range(3):
            for dc in range(3):
                term = cw[3 * dr + dc:3 * dr + dc + 1, :] * taps[dc][dr * GRID_W:dr * GRID_W + tm, :]
                conv = term if conv is None else conv + term
        act_s[slot] = (_silu(conv) * b_s[slot]).astype(BF16)

    def down(c):
        part = _dot(act_s[c % 2], wdn_ref[c])
        if c == 0:
            acc_s[...] = part
        else:
            acc_s[...] += part

    for step in range(_N_FF_CHUNKS + 2):
        if 0 <= step - 2:
            down(step - 2)
        if step < _N_FF_CHUNKS:
            up(step)
        if 0 <= step - 1 < _N_FF_CHUNKS:
            mix(step - 1)
    h2 = hx_ref[...] + gate_ref[...] * acc_s[...]
    o_ref[...] = _rms(h2) * fg_ref[...]


def _ffn(hn2, hx, wab, cwf, wdn, gate, fg, *, tm):
    t = hx.shape[0]
    rb = tm // GRID_W
    nrb = t // GRID_W
    tile = lambda: pl.BlockSpec((tm, D_MODEL), lambda i: (i, 0))
    ext = tm + 2 * GRID_W
    return pl.pallas_call(
        functools.partial(_ffn_kernel, tm=tm),
        grid=(t // tm,),
        in_specs=[tile(),
                  pl.BlockSpec((GRID_W, D_MODEL), lambda i: (jnp.maximum(i * rb - 1, 0), 0)),
                  pl.BlockSpec((GRID_W, D_MODEL), lambda i: (jnp.minimum((i + 1) * rb, nrb - 1), 0)),
                  tile()] + [_const_spec(a.shape) for a in (wab, cwf, wdn, gate, fg)],
        out_specs=tile(),
        out_shape=jax.ShapeDtypeStruct((t, D_MODEL), F32),
        scratch_shapes=[pltpu.VMEM((ext, D_MODEL), BF16),
                        pltpu.VMEM((2, ext + 2 * _PAD, _FF_CHUNK), F32),
                        pltpu.VMEM((2, tm, _FF_CHUNK), F32),
                        pltpu.VMEM((2, tm, _FF_CHUNK), BF16),
                        pltpu.VMEM((tm, D_MODEL), F32)],
        compiler_params=pltpu.CompilerParams(dimension_semantics=("parallel",), vmem_limit_bytes=_VMEM_LIMIT_V7X),
        name="conv_ffn",
    )(hn2, hn2, hn2, hx, wab, cwf, wdn, gate, fg)


def _gate_columns(wg):
    idx_i = [2 * d * HEADS + h for d in range(2) for h in range(HEADS)]
    idx_f = [(2 * d + 1) * HEADS + h for d in range(2) for h in range(HEADS)]
    pad = _GATE_LANES - _N_STATES
    take = lambda idx: jnp.pad(wg[..., jnp.array(idx)], [(0, 0)] * (wg.ndim - 1) + [(0, pad)])
    return take(idx_i), take(idx_f)


def _tile_size(t, want):
    tm = min(want, t)
    assert t % tm == 0 and tm % _L == 0, (t, tm)
    return tm


def kernel(x, c, ctx, c_ctx, w_mod, b_mod, norm1_g, w_in, b_gate, conv_qk, head_norm_g, sgu_ln_g, sgu_ln_b, w_s, b_s,
           w_branch_mlstm, w_branch_sgu, w_out, norm2_g, w_up, w_ffn_conv, w_down, final_g):
    assert x.shape[0] == 1 and w_mod.shape[0] == 1, "single batch element, single layer"
    d = D_MODEL
    x2, ctx2 = x[0], ctx[0]
    t = x2.shape[0]
    assert t % GRID_W == 0 and ctx2.shape[0] % _L == 0
    row = lambda a: a.reshape(1, -1)

    cc = jnp.zeros((8, d), F32).at[0].set(c[0]).at[1].set(c_ctx)
    mod = _modulation(cc, w_mod[0], row(b_mod[0]))
    mx = [mod[0:1, j * d:(j + 1) * d] for j in range(N_MOD)]
    mc = [mod[1:2, j * d:(j + 1) * d] for j in range(N_MOD)]

    w_in0 = w_in[0]
    wqkv = w_in0[:, :3 * d].astype(BF16)
    wgi, wgf = _gate_columns(w_in0[:, 3 * d:3 * d + 4 * HEADS])
    bgi, bgf = _gate_columns(row(b_gate[0]))
    wrest = w_in0[:, 3 * d + 4 * HEADS:].astype(BF16)
    eye2 = jnp.eye(2, dtype=F32)
    wsd = jnp.stack([jnp.kron(eye2, w_s[0, g]) for g in range(HEADS)]).astype(BF16)
    bs2 = jnp.tile(b_s[0].T, (2, 1))
    proj_w = (row(norm1_g[0]), wqkv, wgi.astype(BF16), wgf.astype(BF16), bgi, bgf, conv_qk[0], wrest,
              row(sgu_ln_g[0]), row(sgu_ln_b[0]), wsd, bs2, w_branch_sgu[0].astype(BF16))

    pc = _projection(ctx2, mc[0], mc[1], *proj_w, tm=_tile_size(ctx2.shape[0], 512))
    c0 = jnp.zeros((_N_STATES, HEAD_DIM, _C_EXT), F32)
    m0 = jnp.full((1, _GATE_LANES), M_INIT, F32)
    _, _, c1, m1 = _mlstm(*pc[:6], c0, m0)

    q, kt, v, tab, gr, gm, so, sgm, s = _projection(x2, mx[0], mx[1], *proj_w, tm=_tile_size(t, 512))
    hf, hb, _, _ = _mlstm(q, kt, v, tab, gr, gm, c1, m1)
    hx, hn2 = _post(hf, hb, so, sgm, s, x2, row(head_norm_g[0]), w_branch_mlstm[0].astype(BF16),
                    w_out[0].astype(BF16), mx[2], row(norm2_g[0]), mx[3], mx[4], tm=_tile_size(t, 512))

    nch = _N_FF_CHUNKS
    w_up0 = w_up[0]
    wab = w_up0.reshape(d, 2, nch, _FF_CHUNK).transpose(2, 0, 1, 3).reshape(nch, d, 2 * _FF_CHUNK).astype(BF16)
    cwf = w_ffn_conv[0].reshape(9, nch, _FF_CHUNK).transpose(1, 0, 2)
    wdn = w_down[0].reshape(nch, _FF_CHUNK, d).astype(BF16)
    out = _ffn(hn2, hx, wab, cwf, wdn, mx[5], row(final_g), tm=_tile_size(t, 1024))
    return out[None]
```

```python
import functools

import jax
import jax.numpy as jnp
from jax import lax
from jax.experimental import pallas as pl
from jax.experimental.pallas import tpu as pltpu

F32 = jnp.float32
BF16 = jnp.bfloat16

D_MODEL = 1024
GRID_W = 64
HEADS = 4
HEAD_DIM = 256
N_MOD = 6
D_FF = 2816
EPS = 1e-6
M_INIT = -1e30

_L = 256
_HALO = 16
_FF_CHUNK = 256
_N_FF_CHUNKS = D_FF // _FF_CHUNK
_GATE_LANES = 128
_PAD = 8
_PROJ_BLOCK = 256
_VMEM_LIMIT_V7X = 56 * 1024 * 1024

_N_STATES = 2 * HEADS
_TAB_W = 3 * _GATE_LANES
_C_EXT = HEAD_DIM + _GATE_LANES
_W_GATES = 3 * D_MODEL
_W_REST = _W_GATES + 2 * _GATE_LANES


def _dot(a, b):
    return jnp.dot(a, b, preferred_element_type=F32)


def _split3(x):
    h1 = x.astype(BF16)
    r1 = x - h1.astype(F32)
    h2 = r1.astype(BF16)
    h3 = (r1 - h2.astype(F32)).astype(BF16)
    return h1, h2, h3


def _sigmoid(x):
    return 1.0 / (1.0 + jnp.exp(-x))


def _silu(x):
    return x * _sigmoid(x)


def _gelu_tanh(x):
    return 0.5 * x * (1.0 + jnp.tanh(0.7978845608028654 * (x + 0.044715 * (x * x * x))))


def _log_sigmoid(x):
    return jnp.minimum(x, 0.0) - jnp.log(1.0 + jnp.exp(-jnp.abs(x)))


def _rms(x):
    return x * lax.rsqrt(jnp.mean(x * x, axis=-1, keepdims=True) + EPS)


def _const_spec(shape):
    nd = len(shape)
    return pl.BlockSpec(shape, lambda *_: (0,) * nd, pipeline_mode=pl.Buffered(1))


def _mod_kernel(cc_ref, w_ref, b_ref, o_ref):
    s = _silu(cc_ref[...])
    sh, sm, sl = _split3(s)
    wh, wm, wl = _split3(w_ref[...])
    acc = _dot(sl, wh) + _dot(sh, wl) + _dot(sm, wm)
    acc = acc + _dot(sm, wh) + _dot(sh, wm)
    o_ref[...] = acc + _dot(sh, wh) + b_ref[...]


def _modulation(cc, w_mod, b_mod):
    n_out = w_mod.shape[1]
    nb = D_MODEL
    return pl.pallas_call(
        _mod_kernel,
        grid=(n_out // nb,),
        in_specs=[pl.BlockSpec((8, D_MODEL), lambda j: (0, 0)),
                  pl.BlockSpec((D_MODEL, nb), lambda j: (0, j)),
                  pl.BlockSpec((1, nb), lambda j: (0, j))],
        out_specs=pl.BlockSpec((8, nb), lambda j: (0, j)),
        out_shape=jax.ShapeDtypeStruct((8, n_out), F32),
        compiler_params=pltpu.CompilerParams(dimension_semantics=("arbitrary",)),
        name="modulation",
    )(cc, w_mod, b_mod)


def _gate_tables(gi, lf):
    row = lax.broadcasted_iota(jnp.int32, (_L, _L), 0)
    col = lax.broadcasted_iota(jnp.int32, (_L, _L), 1)
    ltri = (col <= row).astype(BF16)
    l1, l2, l3 = _split3(lf)
    prefix = _dot(ltri, l3) + _dot(ltri, l2) + _dot(ltri, l1)
    total = prefix[_L - 1:_L, :]
    suffix = total - prefix + lf
    lane = lax.broadcasted_iota(jnp.int32, (1, _GATE_LANES), 1)
    cum = jnp.where(lane >= HEADS, suffix, prefix)
    a = total - cum + gi
    c = gi - cum
    c_rows = c.T[0:_N_STATES, :]
    a_rows = a.T[0:_N_STATES, :]
    pos = lax.broadcasted_iota(jnp.int32, (_N_STATES, _L), 1)
    fwd_max, bwd_max = c_rows, c_rows
    k = 1
    while k < _L:
        fwd_max = jnp.maximum(fwd_max, jnp.where(pos >= k, pltpu.roll(fwd_max, k, 1), -jnp.inf))
        bwd_max = jnp.maximum(bwd_max, jnp.where(pos < _L - k, pltpu.roll(bwd_max, _L - k, 1), -jnp.inf))
        k *= 2
    state = lax.broadcasted_iota(jnp.int32, (_N_STATES, _L), 0)
    run_max = jnp.where(state < HEADS, fwd_max, bwd_max)
    run_max_cols = jnp.concatenate([run_max, jnp.zeros((_GATE_LANES - _N_STATES, _L), F32)], axis=0).T
    return cum, cum + run_max_cols, a, c_rows, a_rows, total, jnp.max(a, axis=0, keepdims=True)


def _proj_kernel(x_ref, xp_ref, xn_ref, shift_ref, scale_ref, g1_ref, w_ref, bg_ref,
                 cw_ref, lng_ref, lnb_ref, wsd_ref, bs_ref, wbs_ref,
                 q_ref, kt_ref, v_ref, tab_ref, gr_ref, gm_ref, so_ref, sgm_ref, s_ref,
                 hn_s, z_s, ys_s, *, tm):
    i = pl.program_id(0)
    has_prev = i > 0
    has_next = i < pl.num_programs(0) - 1
    scale1 = 1.0 + scale_ref[...]

    def norm_mod(xb):
        return ((_rms(xb) * g1_ref[...]) * scale1 + shift_ref[...]).astype(BF16)

    hn_s[0:_HALO, :] = norm_mod(xp_ref[...])
    hn_s[_HALO:_HALO + tm, :] = norm_mod(x_ref[...])
    hn_s[_HALO + tm:, :] = norm_mod(xn_ref[...])
    hn = hn_s[_HALO:_HALO + tm, :]

    bw = _PROJ_BLOCK
    u_blocks, vg_blocks = [], []

    def wcols(seg, b):
        return w_ref[:, seg + b * bw:seg + (b + 1) * bw]

    def qk_block(c, b):
        lanes = slice(b * bw, (b + 1) * bw)
        z = _dot(hn_s[...], wcols(c * D_MODEL, b))
        z_s[c, 0:_HALO, lanes] = jnp.where(has_prev, z[0:_HALO, :], 0.0)
        z_s[c, _HALO:_HALO + tm, lanes] = z[_HALO:_HALO + tm, :]
        z_s[c, _HALO + tm:, lanes] = jnp.where(has_next, z[_HALO + tm:, :], 0.0)
        cw = cw_ref[:, c * D_MODEL + b * bw:c * D_MODEL + (b + 1) * bw]
        y = (cw[0:1, :] * z_s[c, _HALO - 1:_HALO - 1 + tm, lanes] + cw[1:2, :] * z_s[c, _HALO:_HALO + tm, lanes]
             + cw[2:3, :] * z_s[c, _HALO + 1:_HALO + 1 + tm, lanes])
        y = _silu(y)
        if c == 0:
            q_ref[:, lanes] = (y * (HEAD_DIM ** -0.5)).astype(BF16)
        else:
            kt_ref[lanes, :] = y.T.astype(BF16)

    for b in range(D_MODEL // bw):
        lanes = slice(b * bw, (b + 1) * bw)
        qk_block(0, b)
        v_ref[:, lanes] = _dot(hn, wcols(2 * D_MODEL, b)).astype(BF16)
        qk_block(1, b)
        so_ref[:, lanes] = _sigmoid(_dot(hn, wcols(_W_REST, b))).astype(BF16)
        u_blocks.append(_gelu_tanh(_dot(hn, wcols(_W_REST + D_MODEL, b))))
        sgm_ref[:, lanes] = _sigmoid(_dot(hn, wcols(_W_REST + 3 * D_MODEL, b))).astype(BF16)
        vg_blocks.append(_gelu_tanh(_dot(hn, wcols(_W_REST + 2 * D_MODEL, b))))

    gates = _dot(hn, w_ref[:, _W_GATES:_W_REST]) + bg_ref[...]
    gi = gates[:, 0:_GATE_LANES]
    lf = _log_sigmoid(gates[:, _GATE_LANES:])
    for j in range(tm // _L):
        rows = slice(j * _L, (j + 1) * _L)
        cum, mi, a, c_rows, a_rows, total, a_max = _gate_tables(gi[rows, :], lf[rows, :])
        tab_ref[rows, 0:_GATE_LANES] = cum
        tab_ref[rows, _GATE_LANES:2 * _GATE_LANES] = mi
        tab_ref[rows, 2 * _GATE_LANES:] = a
        gr_ref[0:_N_STATES, rows] = c_rows
        gr_ref[_N_STATES:, rows] = a_rows
        gm_ref[j * 8:(j + 1) * 8, :] = jnp.concatenate([total, a_max, jnp.zeros((6, _GATE_LANES), F32)], axis=0)

    vg = jnp.concatenate(vg_blocks, axis=1)
    u = jnp.concatenate(u_blocks, axis=1)
    vg = vg - jnp.mean(vg, axis=-1, keepdims=True)
    vn = (vg * lax.rsqrt(jnp.mean(vg * vg, axis=-1, keepdims=True) + EPS) * lng_ref[...] + lnb_ref[...]).astype(BF16)
    for p in range(tm // _L):
        for g in range(HEADS):
            rows = slice(p * _L, (p + 1) * _L)
            cols = slice(g * HEAD_DIM, (g + 1) * HEAD_DIM)
            mixed = _dot(wsd_ref[g], vn[rows, cols]) + bs_ref[:, g:g + 1]
            ys_s[rows, cols] = (u[rows, cols] * mixed).astype(BF16)
    gg = _sigmoid(_dot(hn, w_ref[:, _W_REST + 4 * D_MODEL:_W_REST + 5 * D_MODEL]))
    s_ref[...] = (gg * _dot(ys_s[...], wbs_ref[...])).astype(BF16)


def _projection(x, shift, scale, g1, w, bg, cw, lng, lnb, wsd, bs2, wbs, *, tm):
    t = x.shape[0]
    nt = t // tm
    hb = tm // _HALO
    nhb = t // _HALO
    tile = lambda w: pl.BlockSpec((tm, w), lambda i: (i, 0))
    in_specs = [
        tile(D_MODEL),
        pl.BlockSpec((_HALO, D_MODEL), lambda i: (jnp.maximum(i * hb - 1, 0), 0)),
        pl.BlockSpec((_HALO, D_MODEL), lambda i: (jnp.minimum((i + 1) * hb, nhb - 1), 0)),
    ] + [_const_spec(a.shape) for a in (shift, scale, g1, w, bg, cw, lng, lnb, wsd, bs2, wbs)]
    gm_rows = 8 * (tm // _L)
    out_specs = [tile(D_MODEL), pl.BlockSpec((D_MODEL, tm), lambda i: (0, i)), tile(D_MODEL), tile(_TAB_W),
                 pl.BlockSpec((2 * _N_STATES, tm), lambda i: (0, i)),
                 pl.BlockSpec((gm_rows, _GATE_LANES), lambda i: (i, 0)),
                 tile(D_MODEL), tile(D_MODEL), tile(D_MODEL)]
    act = jax.ShapeDtypeStruct((t, D_MODEL), BF16)
    out_shape = [act, jax.ShapeDtypeStruct((D_MODEL, t), BF16), act, jax.ShapeDtypeStruct((t, _TAB_W), F32),
                 jax.ShapeDtypeStruct((2 * _N_STATES, t), F32), jax.ShapeDtypeStruct((8 * (t // _L), _GATE_LANES), F32),
                 act, act, act]
    return pl.pallas_call(
        functools.partial(_proj_kernel, tm=tm),
        grid=(nt,),
        in_specs=in_specs,
        out_specs=out_specs,
        out_shape=out_shape,
        scratch_shapes=[pltpu.VMEM((tm + 2 * _HALO, D_MODEL), BF16),
                        pltpu.VMEM((2, tm + 2 * _HALO, D_MODEL), F32),
                        pltpu.VMEM((tm, D_MODEL), BF16)],
        compiler_params=pltpu.CompilerParams(dimension_semantics=("parallel",), vmem_limit_bytes=_VMEM_LIMIT_V7X),
        name="projection",
    )(x, x, x, shift, scale, g1, w, bg, cw, lng, lnb, wsd, bs2, wbs)


def _mlstm_kernel(qf_ref, ktf_ref, vf_ref, tabf_ref, grf_ref, gmf_ref, qb_ref, ktb_ref, vb_ref, tabb_ref, grb_ref, gmb_ref,
                  c0_ref, m0_ref, hf_ref, hb_ref, c_ref, m_ref):
    @pl.when(pl.program_id(0) == 0)
    def _():
        c_ref[...] = c0_ref[...]
        m_ref[...] = m0_ref[...]

    row = lax.broadcasted_iota(jnp.int32, (_L, _L), 0)
    col = lax.broadcasted_iota(jnp.int32, (_L, _L), 1)
    ones = jnp.ones((_L, _GATE_LANES), BF16)
    m_prev = m_ref[...]
    m_next = []
    dirs = ((qf_ref, ktf_ref, vf_ref, tabf_ref, grf_ref, gmf_ref, hf_ref, col <= row),
            (qb_ref, ktb_ref, vb_ref, tabb_ref, grb_ref, gmb_ref, hb_ref, col >= row))
    for d, (q_ref, kt_ref, v_ref, tab_ref, gr_ref, gm_ref, h_ref, mask) in enumerate(dirs):
        cum = tab_ref[:, 0:_GATE_LANES]
        m_row = jnp.maximum(cum + m_prev, tab_ref[:, _GATE_LANES:2 * _GATE_LANES])
        e = cum - m_row
        total, a_max = gm_ref[0:1, :], gm_ref[1:2, :]
        m_new = jnp.maximum(total + m_prev, a_max)
        a_old = jnp.exp(total + m_prev - m_new)
        m_next.append(m_new)
        gr = gr_ref[...]
        for hd in range(HEADS):
            j = d * HEADS + hd
            cols = slice(hd * HEAD_DIM, (hd + 1) * HEAD_DIM)
            qh, kth = q_ref[:, cols], kt_ref[cols, :]
            v_ext = jnp.concatenate([v_ref[:, cols], ones], axis=1)
            e_b = jnp.broadcast_to(e[:, j:j + 1], (_L, _GATE_LANES))
            m_row_b = jnp.broadcast_to(m_row[:, j:j + 1], (_L, _GATE_LANES))
            c_row, a_row = gr[j:j + 1, :], gr[_N_STATES + j:_N_STATES + j + 1, :]
            p = jnp.exp(jnp.where(mask, jnp.concatenate([e_b, e_b], axis=1) + c_row, -jnp.inf))
            s = (_dot(qh, kth) * p).astype(BF16)
            c_prev = c_ref[j]
            w_inter = jnp.exp(e_b + m_prev[:, j:j + 1])
            num = _dot(s, v_ext) + jnp.concatenate([w_inter] * 3, axis=1) * _dot(qh, c_prev.astype(BF16))
            inv = 1.0 / jnp.maximum(jnp.abs(num[:, HEAD_DIM:]), jnp.exp(-m_row_b))
            h_ref[:, cols] = (num[:, :HEAD_DIM] * jnp.concatenate([inv, inv], axis=1)).astype(BF16)
            kw_t = (kth.astype(F32) * jnp.exp(a_row - m_new[:, j:j + 1])).astype(BF16)
            c_ref[j] = a_old[:, j:j + 1] * c_prev + _dot(kw_t, v_ext)
    lane = lax.broadcasted_iota(jnp.int32, (1, _GATE_LANES), 1)
    m_ref[...] = jnp.where(lane < HEADS, m_next[0], m_next[1])


def _mlstm(q, kt, v, tab, gr, gm, c0, m0):
    t = q.shape[0]
    nc = t // _L
    fwd_i = lambda i: i
    bwd_i = lambda i: nc - 1 - i

    def specs(ix):
        rows = lambda w: pl.BlockSpec((_L, w), lambda i: (ix(i), 0))
        return [rows(D_MODEL), pl.BlockSpec((D_MODEL, _L), lambda i: (0, ix(i))), rows(D_MODEL), rows(_TAB_W),
                pl.BlockSpec((2 * _N_STATES, _L), lambda i: (0, ix(i))),
                pl.BlockSpec((8, _GATE_LANES), lambda i: (ix(i), 0))]

    out_rows = lambda ix: pl.BlockSpec((_L, D_MODEL), lambda i: (ix(i), 0))
    state_specs = [pl.BlockSpec(a.shape, lambda i, nd=a.ndim: (0,) * nd) for a in (c0, m0)]
    act = jax.ShapeDtypeStruct((t, D_MODEL), BF16)
    return pl.pallas_call(
        _mlstm_kernel,
        grid=(nc,),
        in_specs=specs(fwd_i) + specs(bwd_i) + [_const_spec(c0.shape), _const_spec(m0.shape)],
        out_specs=[out_rows(fwd_i), out_rows(bwd_i)] + state_specs,
        out_shape=[act, act] + [jax.ShapeDtypeStruct(a.shape, F32) for a in (c0, m0)],
        compiler_params=pltpu.CompilerParams(dimension_semantics=("arbitrary",), vmem_limit_bytes=_VMEM_LIMIT_V7X),
        name="mlstm",
    )(q, kt, v, tab, gr, gm, q, kt, v, tab, gr, gm, c0, m0)


def _post_kernel(hf_ref, hb_ref, so_ref, sgm_ref, s_ref, x_ref, hg_ref, wbm_ref, wout_ref, gate_ref,
                 g2_ref, shift_ref, scale_ref, hx_ref, hn2_ref, ym_s):
    hm = hf_ref[...].astype(F32) + hb_ref[...].astype(F32)
    for hd in range(HEADS):
        cols = slice(hd * HEAD_DIM, (hd + 1) * HEAD_DIM)
        ym = _rms(hm[:, cols]) * hg_ref[:, cols]
        ym_s[:, cols] = (so_ref[:, cols].astype(F32) * ym).astype(BF16)
    y = sgm_ref[...].astype(F32) * _dot(ym_s[...], wbm_ref[...]) + s_ref[...].astype(F32)
    hx = x_ref[...] + gate_ref[...] * _dot(y.astype(BF16), wout_ref[...])
    hx_ref[...] = hx
    hn2_ref[...] = ((_rms(hx) * g2_ref[...]) * (1.0 + scale_ref[...]) + shift_ref[...]).astype(BF16)


def _post(hf, hb, so, sgm, s, x, hg, wbm, wout, gate, g2, shift, scale, *, tm):
    t = x.shape[0]
    tile = lambda: pl.BlockSpec((tm, D_MODEL), lambda i: (i, 0))
    return pl.pallas_call(
        _post_kernel,
        grid=(t // tm,),
        in_specs=[tile() for _ in range(6)] + [_const_spec(a.shape) for a in (hg, wbm, wout, gate, g2, shift, scale)],
        out_specs=[tile(), tile()],
        out_shape=[jax.ShapeDtypeStruct((t, D_MODEL), F32), jax.ShapeDtypeStruct((t, D_MODEL), BF16)],
        scratch_shapes=[pltpu.VMEM((tm, D_MODEL), BF16)],
        compiler_params=pltpu.CompilerParams(dimension_semantics=("parallel",), vmem_limit_bytes=_VMEM_LIMIT_V7X),
        name="post_mixer",
    )(hf, hb, so, sgm, s, x, hg, wbm, wout, gate, g2, shift, scale)


def _ffn_kernel(hn_ref, hnp_ref, hnn_ref, hx_ref, wab_ref, cw_ref, wdn_ref, gate_ref, fg_ref, o_ref,
                hne_s, a_s, b_s, act_s, acc_s, *, tm):
    i = pl.program_id(0)
    has_above = i > 0
    has_below = i < pl.num_programs(0) - 1
    ext = tm + 2 * GRID_W
    hne_s[0:GRID_W, :] = hnp_ref[...]
    hne_s[GRID_W:GRID_W + tm, :] = hn_ref[...]
    hne_s[GRID_W + tm:, :] = hnn_ref[...]
    for slot in range(2):
        a_s[slot, 0:_PAD, :] = jnp.zeros((_PAD, _FF_CHUNK), F32)
        a_s[slot, _PAD + ext:, :] = jnp.zeros((_PAD, _FF_CHUNK), F32)
    gcol = lax.broadcasted_iota(jnp.int32, (ext, 1), 0) % GRID_W

    def up(c):
        slot = c % 2
        ab = _dot(hne_s[...], wab_ref[:, 2 * c * _FF_CHUNK:2 * (c + 1) * _FF_CHUNK])
        a = ab[:, :_FF_CHUNK]
        a_s[slot, _PAD:_PAD + GRID_W, :] = jnp.where(has_above, a[:GRID_W, :], 0.0)
        a_s[slot, _PAD + GRID_W:_PAD + GRID_W + tm, :] = a[GRID_W:GRID_W + tm, :]
        a_s[slot, _PAD + GRID_W + tm:_PAD + ext, :] = jnp.where(has_below, a[GRID_W + tm:, :], 0.0)
        b_s[slot] = ab[GRID_W:GRID_W + tm, _FF_CHUNK:]

    def mix(c):
        slot = c % 2
        taps = (jnp.where(gcol >= 1, a_s[slot, _PAD - 1:_PAD - 1 + ext, :], 0.0),
                a_s[slot, _PAD:_PAD + ext, :],
                jnp.where(gcol <= GRID_W - 2, a_s[slot, _PAD + 1:_PAD + 1 + ext, :], 0.0))
        cw = cw_ref[:, c * _FF_CHUNK:(c + 1) * _FF_CHUNK]
        conv = None
        for dr in range(3):
            for dc in range(3):
                term = cw[3 * dr + dc:3 * dr + dc + 1, :] * taps[dc][dr * GRID_W:dr * GRID_W + tm, :]
                conv = term if conv is None else conv + term
        act_s[slot] = (_silu(conv) * b_s[slot]).astype(BF16)

    def down(c):
        part = _dot(act_s[c % 2], wdn_ref[c * _FF_CHUNK:(c + 1) * _FF_CHUNK, :])
        if c == 0:
            acc_s[...] = part
        else:
            acc_s[...] += part

    for step in range(_N_FF_CHUNKS + 2):
        if 0 <= step - 2:
            down(step - 2)
        if step < _N_FF_CHUNKS:
            up(step)
        if 0 <= step - 1 < _N_FF_CHUNKS:
            mix(step - 1)
    h2 = hx_ref[...] + gate_ref[...] * acc_s[...]
    o_ref[...] = _rms(h2) * fg_ref[...]


def _ffn(hn2, hx, wab, cwf, wdn, gate, fg, *, tm):
    t = hx.shape[0]
    rb = tm // GRID_W
    nrb = t // GRID_W
    tile = lambda: pl.BlockSpec((tm, D_MODEL), lambda i: (i, 0))
    ext = tm + 2 * GRID_W
    return pl.pallas_call(
        functools.partial(_ffn_kernel, tm=tm),
        grid=(t // tm,),
        in_specs=[tile(),
                  pl.BlockSpec((GRID_W, D_MODEL), lambda i: (jnp.maximum(i * rb - 1, 0), 0)),
                  pl.BlockSpec((GRID_W, D_MODEL), lambda i: (jnp.minimum((i + 1) * rb, nrb - 1), 0)),
                  tile()] + [_const_spec(a.shape) for a in (wab, cwf, wdn, gate, fg)],
        out_specs=tile(),
        out_shape=jax.ShapeDtypeStruct((t, D_MODEL), F32),
        scratch_shapes=[pltpu.VMEM((ext, D_MODEL), BF16),
                        pltpu.VMEM((2, ext + 2 * _PAD, _FF_CHUNK), F32),
                        pltpu.VMEM((2, tm, _FF_CHUNK), F32),
                        pltpu.VMEM((2, tm, _FF_CHUNK), BF16),
                        pltpu.VMEM((tm, D_MODEL), F32)],
        compiler_params=pltpu.CompilerParams(dimension_semantics=("parallel",), vmem_limit_bytes=_VMEM_LIMIT_V7X),
        name="conv_ffn",
    )(hn2, hn2, hn2, hx, wab, cwf, wdn, gate, fg)


def _gate_columns(wg):
    idx_i = [2 * d * HEADS + h for d in range(2) for h in range(HEADS)]
    idx_f = [(2 * d + 1) * HEADS + h for d in range(2) for h in range(HEADS)]
    pad = [(0, 0)] * (wg.ndim - 1) + [(0, _GATE_LANES - _N_STATES)]
    return jnp.concatenate([jnp.pad(wg[..., jnp.array(idx)], pad) for idx in (idx_i, idx_f)], axis=-1)


def _tile_size(t, want):
    tm = min(want, t)
    assert t % tm == 0 and tm % _L == 0, (t, tm)
    return tm


def kernel(x, c, ctx, c_ctx, w_mod, b_mod, norm1_g, w_in, b_gate, conv_qk, head_norm_g, sgu_ln_g, sgu_ln_b, w_s, b_s,
           w_branch_mlstm, w_branch_sgu, w_out, norm2_g, w_up, w_ffn_conv, w_down, final_g):
    assert x.shape[0] == 1 and w_mod.shape[0] == 1, "single batch element, single layer"
    d = D_MODEL
    x2, ctx2 = x[0], ctx[0]
    t = x2.shape[0]
    assert t % GRID_W == 0 and ctx2.shape[0] % _L == 0
    row = lambda a: a.reshape(1, -1)

    cc = jnp.zeros((8, d), F32).at[0].set(c[0]).at[1].set(c_ctx)
    mod = _modulation(cc, w_mod[0], row(b_mod[0]))
    mx = [mod[0:1, j * d:(j + 1) * d] for j in range(N_MOD)]
    mc = [mod[1:2, j * d:(j + 1) * d] for j in range(N_MOD)]

    w_in0 = w_in[0]
    n_gates = 4 * HEADS
    w_proj = jnp.concatenate([w_in0[:, :3 * d], _gate_columns(w_in0[:, 3 * d:3 * d + n_gates]),
                              w_in0[:, 3 * d + n_gates:]], axis=1).astype(BF16)
    bg = _gate_columns(row(b_gate[0]))
    eye2 = jnp.eye(2, dtype=F32)
    wsd = jnp.stack([jnp.kron(eye2, w_s[0, g]) for g in range(HEADS)]).astype(BF16)
    bs2 = jnp.tile(b_s[0].T, (2, 1))
    proj_w = (row(norm1_g[0]), w_proj, bg, conv_qk[0],
              row(sgu_ln_g[0]), row(sgu_ln_b[0]), wsd, bs2, w_branch_sgu[0].astype(BF16))

    pc = _projection(ctx2, mc[0], mc[1], *proj_w, tm=_tile_size(ctx2.shape[0], 512))
    c0 = jnp.zeros((_N_STATES, HEAD_DIM, _C_EXT), F32)
    m0 = jnp.full((1, _GATE_LANES), M_INIT, F32)
    _, _, c1, m1 = _mlstm(*pc[:6], c0, m0)

    q, kt, v, tab, gr, gm, so, sgm, s = _projection(x2, mx[0], mx[1], *proj_w, tm=_tile_size(t, 512))
    hf, hb, _, _ = _mlstm(q, kt, v, tab, gr, gm, c1, m1)
    hx, hn2 = _post(hf, hb, so, sgm, s, x2, row(head_norm_g[0]), w_branch_mlstm[0].astype(BF16),
                    w_out[0].astype(BF16), mx[2], row(norm2_g[0]), mx[3], mx[4], tm=_tile_size(t, 512))

    wab = w_up[0].reshape(d, 2, _N_FF_CHUNKS, _FF_CHUNK).transpose(0, 2, 1, 3).reshape(d, 2 * D_FF).astype(BF16)
    out = _ffn(hn2, hx, wab, w_ffn_conv[0].reshape(9, D_FF), w_down[0].astype(BF16), mx[5], row(final_g),
               tm=_tile_size(t, 1024))
    return out[None]
```

```python
import functools

import jax
import jax.numpy as jnp
from jax import lax
from jax.experimental import pallas as pl
from jax.experimental.pallas import tpu as pltpu

F32 = jnp.float32
BF16 = jnp.bfloat16

D_MODEL = 1024
GRID_W = 64
HEADS = 4
HEAD_DIM = 256
N_MOD = 6
D_FF = 2816
EPS = 1e-6
M_INIT = -1e30

_L = 256
_HALO = 16
_FF_CHUNK = 256
_N_FF_CHUNKS = D_FF // _FF_CHUNK
_GATE_LANES = 128
_PAD = 8
_PROJ_BLOCK = 256
_VMEM_LIMIT_V7X = 56 * 1024 * 1024

_N_STATES = 2 * HEADS
_TAB_W = 3 * _GATE_LANES
_C_EXT = HEAD_DIM + _GATE_LANES
_N_GATES = 4 * HEADS
_W_GATES = 3 * D_MODEL
_W_REST = _W_GATES + _N_GATES
_LANES = 128
_PREP_BLOCK = 512


def _dot(a, b):
    return jnp.dot(a, b, preferred_element_type=F32)


def _split3(x):
    h1 = x.astype(BF16)
    r1 = x - h1.astype(F32)
    h2 = r1.astype(BF16)
    h3 = (r1 - h2.astype(F32)).astype(BF16)
    return h1, h2, h3


def _sigmoid(x):
    return 1.0 / (1.0 + jnp.exp(-x))


def _silu(x):
    return x * _sigmoid(x)


def _gelu_tanh(x):
    return 0.5 * x * (1.0 + jnp.tanh(0.7978845608028654 * (x + 0.044715 * (x * x * x))))


def _log_sigmoid(x):
    return jnp.minimum(x, 0.0) - jnp.log(1.0 + jnp.exp(-jnp.abs(x)))


def _rms(x):
    return x * lax.rsqrt(jnp.mean(x * x, axis=-1, keepdims=True) + EPS)


def _const_spec(shape):
    nd = len(shape)
    return pl.BlockSpec(shape, lambda *_: (0,) * nd, pipeline_mode=pl.Buffered(1))


def _cast_kernel(w_ref, o_ref):
    o_ref[...] = w_ref[...].astype(BF16)


def _shift_cast_kernel(w_ref, next_ref, o_ref, *, shift):
    x = jnp.concatenate([w_ref[...], next_ref[...]], axis=1)
    o_ref[...] = pltpu.roll(x, x.shape[1] - shift, 1)[:, :w_ref.shape[1]].astype(BF16)


def _cast_columns(w, col0, ncols):
    rows = w.shape[0]
    shift = col0 % _LANES
    base = col0 - shift
    assert base % _PREP_BLOCK == 0 and ncols % _PREP_BLOCK == 0 and col0 + ncols <= w.shape[1]
    b0 = base // _PREP_BLOCK
    in_specs = [pl.BlockSpec((rows, _PREP_BLOCK), lambda k: (0, b0 + k))]
    operands = [w]
    body = _cast_kernel
    if shift:
        tiles = _PREP_BLOCK // _LANES
        in_specs.append(pl.BlockSpec((rows, _LANES), lambda k: (0, (b0 + k + 1) * tiles)))
        operands.append(w)
        body = functools.partial(_shift_cast_kernel, shift=shift)
    return pl.pallas_call(
        body,
        grid=(ncols // _PREP_BLOCK,),
        in_specs=in_specs,
        out_specs=pl.BlockSpec((rows, _PREP_BLOCK), lambda k: (0, k)),
        out_shape=jax.ShapeDtypeStruct((rows, ncols), BF16),
        compiler_params=pltpu.CompilerParams(dimension_semantics=("parallel",)),
        name="cast_columns",
    )(*operands)


def _regroup_up(w_up):
    rows = w_up.shape[0]
    return pl.pallas_call(
        _cast_kernel,
        grid=(2 * _N_FF_CHUNKS,),
        in_specs=[pl.BlockSpec((rows, _FF_CHUNK), lambda j: (0, (j % 2) * _N_FF_CHUNKS + j // 2))],
        out_specs=pl.BlockSpec((rows, _FF_CHUNK), lambda j: (0, j)),
        out_shape=jax.ShapeDtypeStruct((rows, 2 * D_FF), BF16),
        compiler_params=pltpu.CompilerParams(dimension_semantics=("parallel",)),
        name="regroup_up",
    )(w_up)


def _gate_cast_kernel(w_ref, sel_ref, o_ref):
    o_ref[...] = _dot(w_ref[...].astype(BF16), sel_ref[...]).astype(BF16)


def _gate_selector():
    sel = [[0.0] * (2 * _GATE_LANES) for _ in range(_LANES)]
    for d in range(2):
        for h in range(HEADS):
            j = d * HEADS + h
            sel[2 * d * HEADS + h][j] = 1.0
            sel[(2 * d + 1) * HEADS + h][_GATE_LANES + j] = 1.0
    return jnp.array(sel, BF16)


def _gate_weights(w_in):
    rows = w_in.shape[0]
    assert _W_GATES % _LANES == 0 and _N_GATES <= _LANES
    return pl.pallas_call(
        _gate_cast_kernel,
        grid=(1,),
        in_specs=[pl.BlockSpec((rows, _LANES), lambda k: (0, _W_GATES // _LANES)),
                  pl.BlockSpec((_LANES, 2 * _GATE_LANES), lambda k: (0, 0))],
        out_specs=pl.BlockSpec((rows, 2 * _GATE_LANES), lambda k: (0, 0)),
        out_shape=jax.ShapeDtypeStruct((rows, 2 * _GATE_LANES), BF16),
        name="gate_weights",
    )(w_in, _gate_selector())


def _mod_kernel(cc_ref, w_ref, b_ref, o_ref):
    s = _silu(cc_ref[...])
    w = w_ref[...]
    rows = [jnp.sum(w * s[:, r:r + 1], axis=0, keepdims=True) + b_ref[...] for r in range(2)]
    o_ref[...] = jnp.concatenate(rows + [jnp.zeros((6, w.shape[1]), F32)], axis=0)


def _modulation(cc, w_mod, b_mod):
    n_out = w_mod.shape[1]
    nb = D_MODEL
    return pl.pallas_call(
        _mod_kernel,
        grid=(n_out // nb,),
        in_specs=[pl.BlockSpec((D_MODEL, 8), lambda j: (0, 0)),
                  pl.BlockSpec((D_MODEL, nb), lambda j: (0, j)),
                  pl.BlockSpec((1, nb), lambda j: (0, j))],
        out_specs=pl.BlockSpec((8, nb), lambda j: (0, j)),
        out_shape=jax.ShapeDtypeStruct((8, n_out), F32),
        compiler_params=pltpu.CompilerParams(dimension_semantics=("arbitrary",)),
        name="modulation",
    )(cc, w_mod, b_mod)


def _gate_tables(gi, lf):
    row = lax.broadcasted_iota(jnp.int32, (_L, _L), 0)
    col = lax.broadcasted_iota(jnp.int32, (_L, _L), 1)
    ltri = (col <= row).astype(BF16)
    l1, l2, l3 = _split3(lf)
    prefix = _dot(ltri, l3) + _dot(ltri, l2) + _dot(ltri, l1)
    total = prefix[_L - 1:_L, :]
    suffix = total - prefix + lf
    lane = lax.broadcasted_iota(jnp.int32, (1, _GATE_LANES), 1)
    cum = jnp.where(lane >= HEADS, suffix, prefix)
    a = total - cum + gi
    c = gi - cum
    c_rows = c.T[0:_N_STATES, :]
    a_rows = a.T[0:_N_STATES, :]
    pos = lax.broadcasted_iota(jnp.int32, (_N_STATES, _L), 1)
    fwd_max, bwd_max = c_rows, c_rows
    k = 1
    while k < _L:
        fwd_max = jnp.maximum(fwd_max, jnp.where(pos >= k, pltpu.roll(fwd_max, k, 1), -jnp.inf))
        bwd_max = jnp.maximum(bwd_max, jnp.where(pos < _L - k, pltpu.roll(bwd_max, _L - k, 1), -jnp.inf))
        k *= 2
    state = lax.broadcasted_iota(jnp.int32, (_N_STATES, _L), 0)
    run_max = jnp.where(state < HEADS, fwd_max, bwd_max)
    run_max_cols = jnp.concatenate([run_max, jnp.zeros((_GATE_LANES - _N_STATES, _L), F32)], axis=0).T
    return cum, cum + run_max_cols, a, c_rows, a_rows, total, jnp.max(a, axis=0, keepdims=True)


def _proj_kernel(x_ref, xp_ref, xn_ref, shift_ref, scale_ref, g1_ref, wqkv_ref, wg_ref, wrest_ref, bg_ref,
                 cw_ref, lng_ref, lnb_ref, wsd_ref, bs_ref, wbs_ref,
                 q_ref, kt_ref, v_ref, tab_ref, gr_ref, gm_ref, so_ref, sgm_ref, s_ref,
                 hn_s, z_s, ys_s, *, tm):
    i = pl.program_id(0)
    has_prev = i > 0
    has_next = i < pl.num_programs(0) - 1
    scale1 = 1.0 + scale_ref[...]

    def norm_mod(xb):
        return ((_rms(xb) * g1_ref[...]) * scale1 + shift_ref[...]).astype(BF16)

    hn_s[0:_HALO, :] = norm_mod(xp_ref[...])
    hn_s[_HALO:_HALO + tm, :] = norm_mod(x_ref[...])
    hn_s[_HALO + tm:, :] = norm_mod(xn_ref[...])
    hn = hn_s[_HALO:_HALO + tm, :]

    bw = _PROJ_BLOCK
    u_blocks, vg_blocks = [], []

    def qkv_cols(seg, b):
        return wqkv_ref[:, seg * D_MODEL + b * bw:seg * D_MODEL + (b + 1) * bw]

    def rest_cols(seg, b):
        return wrest_ref[:, seg * D_MODEL + b * bw:seg * D_MODEL + (b + 1) * bw]

    def qk_block(c, b):
        lanes = slice(b * bw, (b + 1) * bw)
        z = _dot(hn_s[...], qkv_cols(c, b))
        z_s[c, 0:_HALO, lanes] = jnp.where(has_prev, z[0:_HALO, :], 0.0)
        z_s[c, _HALO:_HALO + tm, lanes] = z[_HALO:_HALO + tm, :]
        z_s[c, _HALO + tm:, lanes] = jnp.where(has_next, z[_HALO + tm:, :], 0.0)
        cw = cw_ref[:, c * D_MODEL + b * bw:c * D_MODEL + (b + 1) * bw]
        y = (cw[0:1, :] * z_s[c, _HALO - 1:_HALO - 1 + tm, lanes] + cw[1:2, :] * z_s[c, _HALO:_HALO + tm, lanes]
             + cw[2:3, :] * z_s[c, _HALO + 1:_HALO + 1 + tm, lanes])
        y = _silu(y)
        if c == 0:
            q_ref[:, lanes] = (y * (HEAD_DIM ** -0.5)).astype(BF16)
        else:
            kt_ref[lanes, :] = y.T.astype(BF16)

    for b in range(D_MODEL // bw):
        lanes = slice(b * bw, (b + 1) * bw)
        qk_block(0, b)
        v_ref[:, lanes] = _dot(hn, qkv_cols(2, b)).astype(BF16)
        qk_block(1, b)
        so_ref[:, lanes] = _sigmoid(_dot(hn, rest_cols(0, b))).astype(BF16)
        u_blocks.append(_gelu_tanh(_dot(hn, rest_cols(1, b))))
        sgm_ref[:, lanes] = _sigmoid(_dot(hn, rest_cols(3, b))).astype(BF16)
        vg_blocks.append(_gelu_tanh(_dot(hn, rest_cols(2, b))))

    gates = _dot(hn, wg_ref[...]) + bg_ref[...]
    gi = gates[:, 0:_GATE_LANES]
    lf = _log_sigmoid(gates[:, _GATE_LANES:])
    for j in range(tm // _L):
        rows = slice(j * _L, (j + 1) * _L)
        cum, mi, a, c_rows, a_rows, total, a_max = _gate_tables(gi[rows, :], lf[rows, :])
        tab_ref[rows, 0:_GATE_LANES] = cum
        tab_ref[rows, _GATE_LANES:2 * _GATE_LANES] = mi
        tab_ref[rows, 2 * _GATE_LANES:] = a
        gr_ref[0:_N_STATES, rows] = c_rows
        gr_ref[_N_STATES:, rows] = a_rows
        gm_ref[j * 8:(j + 1) * 8, :] = jnp.concatenate([total, a_max, jnp.zeros((6, _GATE_LANES), F32)], axis=0)

    vg = jnp.concatenate(vg_blocks, axis=1)
    u = jnp.concatenate(u_blocks, axis=1)
    vg = vg - jnp.mean(vg, axis=-1, keepdims=True)
    vn = (vg * lax.rsqrt(jnp.mean(vg * vg, axis=-1, keepdims=True) + EPS) * lng_ref[...] + lnb_ref[...]).astype(BF16)
    for p in range(tm // _L):
        for g in range(HEADS):
            rows = slice(p * _L, (p + 1) * _L)
            cols = slice(g * HEAD_DIM, (g + 1) * HEAD_DIM)
            mixed = _dot(wsd_ref[g], vn[rows, cols]) + bs_ref[:, g:g + 1]
            ys_s[rows, cols] = (u[rows, cols] * mixed).astype(BF16)
    gg = _sigmoid(_dot(hn, wrest_ref[:, 4 * D_MODEL:5 * D_MODEL]))
    s_ref[...] = (gg * _dot(ys_s[...], wbs_ref[...])).astype(BF16)


def _projection(x, shift, scale, g1, wqkv, wg, wrest, bg, cw, lng, lnb, wsd, bs2, wbs, *, tm):
    t = x.shape[0]
    nt = t // tm
    hb = tm // _HALO
    nhb = t // _HALO
    tile = lambda w: pl.BlockSpec((tm, w), lambda i: (i, 0))
    in_specs = [
        tile(D_MODEL),
        pl.BlockSpec((_HALO, D_MODEL), lambda i: (jnp.maximum(i * hb - 1, 0), 0)),
        pl.BlockSpec((_HALO, D_MODEL), lambda i: (jnp.minimum((i + 1) * hb, nhb - 1), 0)),
    ] + [_const_spec(a.shape) for a in (shift, scale, g1, wqkv, wg, wrest, bg, cw, lng, lnb, wsd, bs2, wbs)]
    gm_rows = 8 * (tm // _L)
    out_specs = [tile(D_MODEL), pl.BlockSpec((D_MODEL, tm), lambda i: (0, i)), tile(D_MODEL), tile(_TAB_W),
                 pl.BlockSpec((2 * _N_STATES, tm), lambda i: (0, i)),
                 pl.BlockSpec((gm_rows, _GATE_LANES), lambda i: (i, 0)),
                 tile(D_MODEL), tile(D_MODEL), tile(D_MODEL)]
    act = jax.ShapeDtypeStruct((t, D_MODEL), BF16)
    out_shape = [act, jax.ShapeDtypeStruct((D_MODEL, t), BF16), act, jax.ShapeDtypeStruct((t, _TAB_W), F32),
                 jax.ShapeDtypeStruct((2 * _N_STATES, t), F32), jax.ShapeDtypeStruct((8 * (t // _L), _GATE_LANES), F32),
                 act, act, act]
    return pl.pallas_call(
        functools.partial(_proj_kernel, tm=tm),
        grid=(nt,),
        in_specs=in_specs,
        out_specs=out_specs,
        out_shape=out_shape,
        scratch_shapes=[pltpu.VMEM((tm + 2 * _HALO, D_MODEL), BF16),
                        pltpu.VMEM((2, tm + 2 * _HALO, D_MODEL), F32),
                        pltpu.VMEM((tm, D_MODEL), BF16)],
        compiler_params=pltpu.CompilerParams(dimension_semantics=("parallel",), vmem_limit_bytes=_VMEM_LIMIT_V7X),
        name="projection",
    )(x, x, x, shift, scale, g1, wqkv, wg, wrest, bg, cw, lng, lnb, wsd, bs2, wbs)


def _mlstm_kernel(qf_ref, ktf_ref, vf_ref, tabf_ref, grf_ref, gmf_ref, qb_ref, ktb_ref, vb_ref, tabb_ref, grb_ref, gmb_ref,
                  c0_ref, m0_ref, hf_ref, hb_ref, c_ref, m_ref):
    @pl.when(pl.program_id(0) == 0)
    def _():
        c_ref[...] = c0_ref[...]
        m_ref[...] = m0_ref[...]

    row = lax.broadcasted_iota(jnp.int32, (_L, _L), 0)
    col = lax.broadcasted_iota(jnp.int32, (_L, _L), 1)
    ones = jnp.ones((_L, _GATE_LANES), BF16)
    m_prev = m_ref[...]
    m_next = []
    dirs = ((qf_ref, ktf_ref, vf_ref, tabf_ref, grf_ref, gmf_ref, hf_ref, col <= row),
            (qb_ref, ktb_ref, vb_ref, tabb_ref, grb_ref, gmb_ref, hb_ref, col >= row))
    for d, (q_ref, kt_ref, v_ref, tab_ref, gr_ref, gm_ref, h_ref, mask) in enumerate(dirs):
        cum = tab_ref[:, 0:_GATE_LANES]
        m_row = jnp.maximum(cum + m_prev, tab_ref[:, _GATE_LANES:2 * _GATE_LANES])
        e = cum - m_row
        total, a_max = gm_ref[0:1, :], gm_ref[1:2, :]
        m_new = jnp.maximum(total + m_prev, a_max)
        a_old = jnp.exp(total + m_prev - m_new)
        m_next.append(m_new)
        gr = gr_ref[...]
        for hd in range(HEADS):
            j = d * HEADS + hd
            cols = slice(hd * HEAD_DIM, (hd + 1) * HEAD_DIM)
            qh, kth = q_ref[:, cols], kt_ref[cols, :]
            v_ext = jnp.concatenate([v_ref[:, cols], ones], axis=1)
            e_b = jnp.broadcast_to(e[:, j:j + 1], (_L, _GATE_LANES))
            m_row_b = jnp.broadcast_to(m_row[:, j:j + 1], (_L, _GATE_LANES))
            c_row, a_row = gr[j:j + 1, :], gr[_N_STATES + j:_N_STATES + j + 1, :]
            p = jnp.exp(jnp.where(mask, jnp.concatenate([e_b, e_b], axis=1) + c_row, -jnp.inf))
            s = (_dot(qh, kth) * p).astype(BF16)
            c_prev = c_ref[j]
            w_inter = jnp.exp(e_b + m_prev[:, j:j + 1])
            num = _dot(s, v_ext) + jnp.concatenate([w_inter] * 3, axis=1) * _dot(qh, c_prev.astype(BF16))
            inv = 1.0 / jnp.maximum(jnp.abs(num[:, HEAD_DIM:]), jnp.exp(-m_row_b))
            h_ref[:, cols] = (num[:, :HEAD_DIM] * jnp.concatenate([inv, inv], axis=1)).astype(BF16)
            kw_t = (kth.astype(F32) * jnp.exp(a_row - m_new[:, j:j + 1])).astype(BF16)
            c_ref[j] = a_old[:, j:j + 1] * c_prev + _dot(kw_t, v_ext)
    lane = lax.broadcasted_iota(jnp.int32, (1, _GATE_LANES), 1)
    m_ref[...] = jnp.where(lane < HEADS, m_next[0], m_next[1])


def _mlstm(q, kt, v, tab, gr, gm, c0, m0):
    t = q.shape[0]
    nc = t // _L
    fwd_i = lambda i: i
    bwd_i = lambda i: nc - 1 - i

    def specs(ix):
        rows = lambda w: pl.BlockSpec((_L, w), lambda i: (ix(i), 0))
        return [rows(D_MODEL), pl.BlockSpec((D_MODEL, _L), lambda i: (0, ix(i))), rows(D_MODEL), rows(_TAB_W),
                pl.BlockSpec((2 * _N_STATES, _L), lambda i: (0, ix(i))),
                pl.BlockSpec((8, _GATE_LANES), lambda i: (ix(i), 0))]

    out_rows = lambda ix: pl.BlockSpec((_L, D_MODEL), lambda i: (ix(i), 0))
    state_specs = [pl.BlockSpec(a.shape, lambda i, nd=a.ndim: (0,) * nd) for a in (c0, m0)]
    act = jax.ShapeDtypeStruct((t, D_MODEL), BF16)
    return pl.pallas_call(
        _mlstm_kernel,
        grid=(nc,),
        in_specs=specs(fwd_i) + specs(bwd_i) + [_const_spec(c0.shape), _const_spec(m0.shape)],
        out_specs=[out_rows(fwd_i), out_rows(bwd_i)] + state_specs,
        out_shape=[act, act] + [jax.ShapeDtypeStruct(a.shape, F32) for a in (c0, m0)],
        compiler_params=pltpu.CompilerParams(dimension_semantics=("arbitrary",), vmem_limit_bytes=_VMEM_LIMIT_V7X),
        name="mlstm",
    )(q, kt, v, tab, gr, gm, q, kt, v, tab, gr, gm, c0, m0)


def _post_kernel(hf_ref, hb_ref, so_ref, sgm_ref, s_ref, x_ref, hg_ref, wbm_ref, wout_ref, gate_ref,
                 g2_ref, shift_ref, scale_ref, hx_ref, hn2_ref, ym_s):
    hm = hf_ref[...].astype(F32) + hb_ref[...].astype(F32)
    for hd in range(HEADS):
        cols = slice(hd * HEAD_DIM, (hd + 1) * HEAD_DIM)
        ym = _rms(hm[:, cols]) * hg_ref[:, cols]
        ym_s[:, cols] = (so_ref[:, cols].astype(F32) * ym).astype(BF16)
    y = sgm_ref[...].astype(F32) * _dot(ym_s[...], wbm_ref[...]) + s_ref[...].astype(F32)
    hx = x_ref[...] + gate_ref[...] * _dot(y.astype(BF16), wout_ref[...])
    hx_ref[...] = hx
    hn2_ref[...] = ((_rms(hx) * g2_ref[...]) * (1.0 + scale_ref[...]) + shift_ref[...]).astype(BF16)


def _post(hf, hb, so, sgm, s, x, hg, wbm, wout, gate, g2, shift, scale, *, tm):
    t = x.shape[0]
    tile = lambda: pl.BlockSpec((tm, D_MODEL), lambda i: (i, 0))
    return pl.pallas_call(
        _post_kernel,
        grid=(t // tm,),
        in_specs=[tile() for _ in range(6)] + [_const_spec(a.shape) for a in (hg, wbm, wout, gate, g2, shift, scale)],
        out_specs=[tile(), tile()],
        out_shape=[jax.ShapeDtypeStruct((t, D_MODEL), F32), jax.ShapeDtypeStruct((t, D_MODEL), BF16)],
        scratch_shapes=[pltpu.VMEM((tm, D_MODEL), BF16)],
        compiler_params=pltpu.CompilerParams(dimension_semantics=("parallel",), vmem_limit_bytes=_VMEM_LIMIT_V7X),
        name="post_mixer",
    )(hf, hb, so, sgm, s, x, hg, wbm, wout, gate, g2, shift, scale)


def _ffn_kernel(hn_ref, hnp_ref, hnn_ref, hx_ref, wab_ref, cw_ref, wdn_ref, gate_ref, fg_ref, o_ref,
                hne_s, a_s, b_s, act_s, acc_s, *, tm):
    i = pl.program_id(0)
    has_above = i > 0
    has_below = i < pl.num_programs(0) - 1
    ext = tm + 2 * GRID_W
    hne_s[0:GRID_W, :] = hnp_ref[...]
    hne_s[GRID_W:GRID_W + tm, :] = hn_ref[...]
    hne_s[GRID_W + tm:, :] = hnn_ref[...]
    for slot in range(2):
        a_s[slot, 0:_PAD, :] = jnp.zeros((_PAD, _FF_CHUNK), F32)
        a_s[slot, _PAD + ext:, :] = jnp.zeros((_PAD, _FF_CHUNK), F32)
    gcol = lax.broadcasted_iota(jnp.int32, (ext, 1), 0) % GRID_W

    def up(c):
        slot = c % 2
        ab = _dot(hne_s[...], wab_ref[:, 2 * c * _FF_CHUNK:2 * (c + 1) * _FF_CHUNK])
        a = ab[:, :_FF_CHUNK]
        a_s[slot, _PAD:_PAD + GRID_W, :] = jnp.where(has_above, a[:GRID_W, :], 0.0)
        a_s[slot, _PAD + GRID_W:_PAD + GRID_W + tm, :] = a[GRID_W:GRID_W + tm, :]
        a_s[slot, _PAD + GRID_W + tm:_PAD + ext, :] = jnp.where(has_below, a[GRID_W + tm:, :], 0.0)
        b_s[slot] = ab[GRID_W:GRID_W + tm, _FF_CHUNK:]

    def mix(c):
        slot = c % 2
        taps = (jnp.where(gcol >= 1, a_s[slot, _PAD - 1:_PAD - 1 + ext, :], 0.0),
                a_s[slot, _PAD:_PAD + ext, :],
                jnp.where(gcol <= GRID_W - 2, a_s[slot, _PAD + 1:_PAD + 1 + ext, :], 0.0))
        cw = cw_ref[:, c * _FF_CHUNK:(c + 1) * _FF_CHUNK]
        conv = None
        for dr in range(3):
            for dc in range(3):
                term = cw[3 * dr + dc:3 * dr + dc + 1, :] * taps[dc][dr * GRID_W:dr * GRID_W + tm, :]
                conv = term if conv is None else conv + term
        act_s[slot] = (_silu(conv) * b_s[slot]).astype(BF16)

    def down(c):
        part = _dot(act_s[c % 2], wdn_ref[c * _FF_CHUNK:(c + 1) * _FF_CHUNK, :])
        if c == 0:
            acc_s[...] = part
        else:
            acc_s[...] += part

    for step in range(_N_FF_CHUNKS + 2):
        if 0 <= step - 2:
            down(step - 2)
        if step < _N_FF_CHUNKS:
            up(step)
        if 0 <= step - 1 < _N_FF_CHUNKS:
            mix(step - 1)
    h2 = hx_ref[...] + gate_ref[...] * acc_s[...]
    o_ref[...] = _rms(h2) * fg_ref[...]


def _ffn(hn2, hx, wab, cwf, wdn, gate, fg, *, tm):
    t = hx.shape[0]
    rb = tm // GRID_W
    nrb = t // GRID_W
    tile = lambda: pl.BlockSpec((tm, D_MODEL), lambda i: (i, 0))
    ext = tm + 2 * GRID_W
    return pl.pallas_call(
        functools.partial(_ffn_kernel, tm=tm),
        grid=(t // tm,),
        in_specs=[tile(),
                  pl.BlockSpec((GRID_W, D_MODEL), lambda i: (jnp.maximum(i * rb - 1, 0), 0)),
                  pl.BlockSpec((GRID_W, D_MODEL), lambda i: (jnp.minimum((i + 1) * rb, nrb - 1), 0)),
                  tile()] + [_const_spec(a.shape) for a in (wab, cwf, wdn, gate, fg)],
        out_specs=tile(),
        out_shape=jax.ShapeDtypeStruct((t, D_MODEL), F32),
        scratch_shapes=[pltpu.VMEM((ext, D_MODEL), BF16),
                        pltpu.VMEM((2, ext + 2 * _PAD, _FF_CHUNK), F32),
                        pltpu.VMEM((2, tm, _FF_CHUNK), F32),
                        pltpu.VMEM((2, tm, _FF_CHUNK), BF16),
                        pltpu.VMEM((tm, D_MODEL), F32)],
        compiler_params=pltpu.CompilerParams(dimension_semantics=("parallel",), vmem_limit_bytes=_VMEM_LIMIT_V7X),
        name="conv_ffn",
    )(hn2, hn2, hn2, hx, wab, cwf, wdn, gate, fg)


def _gate_columns(wg):
    idx_i = [2 * d * HEADS + h for d in range(2) for h in range(HEADS)]
    idx_f = [(2 * d + 1) * HEADS + h for d in range(2) for h in range(HEADS)]
    pad = [(0, 0)] * (wg.ndim - 1) + [(0, _GATE_LANES - _N_STATES)]
    return jnp.concatenate([jnp.pad(wg[..., jnp.array(idx)], pad) for idx in (idx_i, idx_f)], axis=-1)


def _tile_size(t, want):
    tm = min(want, t)
    assert t % tm == 0 and tm % _L == 0, (t, tm)
    return tm


def kernel(x, c, ctx, c_ctx, w_mod, b_mod, norm1_g, w_in, b_gate, conv_qk, head_norm_g, sgu_ln_g, sgu_ln_b, w_s, b_s,
           w_branch_mlstm, w_branch_sgu, w_out, norm2_g, w_up, w_ffn_conv, w_down, final_g):
    assert x.shape[0] == 1 and w_mod.shape[0] == 1, "single batch element, single layer"
    d = D_MODEL
    x2, ctx2 = x[0], ctx[0]
    t = x2.shape[0]
    assert t % GRID_W == 0 and ctx2.shape[0] % _L == 0
    row = lambda a: a.reshape(1, -1)

    cc = jnp.zeros((d, 8), F32).at[:, 0].set(c[0]).at[:, 1].set(c_ctx)
    mod = _modulation(cc, w_mod[0], row(b_mod[0]))
    mx = [mod[0:1, j * d:(j + 1) * d] for j in range(N_MOD)]
    mc = [mod[1:2, j * d:(j + 1) * d] for j in range(N_MOD)]

    w_in0 = w_in[0]
    wqkv = _cast_columns(w_in0, 0, _W_GATES)
    wg = _gate_weights(w_in0)
    wrest = _cast_columns(w_in0, _W_REST, w_in0.shape[1] - _W_REST)
    bg = _gate_columns(row(b_gate[0]))
    eye2 = jnp.eye(2, dtype=F32)
    wsd = jnp.stack([jnp.kron(eye2, w_s[0, g]) for g in range(HEADS)]).astype(BF16)
    bs2 = jnp.tile(b_s[0].T, (2, 1))
    proj_w = (row(norm1_g[0]), wqkv, wg, wrest, bg, conv_qk[0],
              row(sgu_ln_g[0]), row(sgu_ln_b[0]), wsd, bs2, w_branch_sgu[0].astype(BF16))

    pc = _projection(ctx2, mc[0], mc[1], *proj_w, tm=_tile_size(ctx2.shape[0], 512))
    c0 = jnp.zeros((_N_STATES, HEAD_DIM, _C_EXT), F32)
    m0 = jnp.full((1, _GATE_LANES), M_INIT, F32)
    _, _, c1, m1 = _mlstm(*pc[:6], c0, m0)

    q, kt, v, tab, gr, gm, so, sgm, s = _projection(x2, mx[0], mx[1], *proj_w, tm=_tile_size(t, 512))
    hf, hb, _, _ = _mlstm(q, kt, v, tab, gr, gm, c1, m1)
    hx, hn2 = _post(hf, hb, so, sgm, s, x2, row(head_norm_g[0]), w_branch_mlstm[0].astype(BF16),
                    w_out[0].astype(BF16), mx[2], row(norm2_g[0]), mx[3], mx[4], tm=_tile_size(t, 512))

    out = _ffn(hn2, hx, _regroup_up(w_up[0]), w_ffn_conv[0].reshape(9, D_FF), w_down[0].astype(BF16), mx[5],
               row(final_g), tm=_tile_size(t, 1024))
    return out[None]
```

```python
import functools

import jax
import jax.numpy as jnp
from jax import lax
from jax.experimental import pallas as pl
from jax.experimental.pallas import tpu as pltpu

F32 = jnp.float32
BF16 = jnp.bfloat16

D_MODEL = 1024
GRID_W = 64
HEADS = 4
HEAD_DIM = 256
N_MOD = 6
D_FF = 2816
EPS = 1e-6
M_INIT = -1e30

_L = 256
_HALO = 16
_FF_CHUNK = 256
_N_FF_CHUNKS = D_FF // _FF_CHUNK
_GATE_LANES = 128
_PAD = 8
_PROJ_BLOCK = 256
_VMEM_LIMIT_V7X = 56 * 1024 * 1024

_N_STATES = 2 * HEADS
_TAB_W = 3 * _GATE_LANES
_C_EXT = HEAD_DIM + _GATE_LANES
_N_GATES = 4 * HEADS
_W_GATES = 3 * D_MODEL
_W_REST = _W_GATES + _N_GATES
_SUBLANES = 8
_PREP_BLOCK = 512


def _dot(a, b):
    return jnp.dot(a, b, preferred_element_type=F32)


def _split3(x):
    h1 = x.astype(BF16)
    r1 = x - h1.astype(F32)
    h2 = r1.astype(BF16)
    h3 = (r1 - h2.astype(F32)).astype(BF16)
    return h1, h2, h3


def _sigmoid(x):
    return 1.0 / (1.0 + jnp.exp(-x))


def _silu(x):
    return x * _sigmoid(x)


def _gelu_tanh(x):
    return 0.5 * x * (1.0 + jnp.tanh(0.7978845608028654 * (x + 0.044715 * (x * x * x))))


def _log_sigmoid(x):
    return jnp.minimum(x, 0.0) - jnp.log(1.0 + jnp.exp(-jnp.abs(x)))


def _rms(x):
    return x * lax.rsqrt(jnp.mean(x * x, axis=-1, keepdims=True) + EPS)


def _const_spec(shape):
    nd = len(shape)
    return pl.BlockSpec(shape, lambda *_: (0,) * nd, pipeline_mode=pl.Buffered(1))


def _cast_kernel(w_ref, o_ref):
    o_ref[...] = w_ref[...].astype(BF16)


def _transpose_cast_kernel(w_ref, o_ref):
    o_ref[...] = w_ref[...].T.astype(BF16)


def _shift_transpose_cast_kernel(w_ref, next_ref, o_ref, *, shift):
    x = jnp.concatenate([w_ref[...], next_ref[...]], axis=0)
    o_ref[...] = x[shift:shift + w_ref.shape[0], :].T.astype(BF16)


def _cast_columns(w_t, col0, ncols):
    rows = w_t.shape[1]
    shift = col0 % _PREP_BLOCK
    base = col0 - shift
    assert shift % _SUBLANES == 0 and ncols % _PREP_BLOCK == 0 and col0 + ncols <= w_t.shape[0]
    b0 = base // _PREP_BLOCK
    in_specs = [pl.BlockSpec((_PREP_BLOCK, rows), lambda k: (b0 + k, 0))]
    operands = [w_t]
    body = _transpose_cast_kernel
    if shift:
        assert _PREP_BLOCK % shift == 0
        per_block = _PREP_BLOCK // shift
        in_specs.append(pl.BlockSpec((shift, rows), lambda k: ((b0 + k + 1) * per_block, 0)))
        operands.append(w_t)
        body = functools.partial(_shift_transpose_cast_kernel, shift=shift)
    return pl.pallas_call(
        body,
        grid=(ncols // _PREP_BLOCK,),
        in_specs=in_specs,
        out_specs=pl.BlockSpec((rows, _PREP_BLOCK), lambda k: (0, k)),
        out_shape=jax.ShapeDtypeStruct((rows, ncols), BF16),
        compiler_params=pltpu.CompilerParams(dimension_semantics=("parallel",)),
        name="cast_columns",
    )(*operands)


def _regroup_up(w_up):
    rows = w_up.shape[0]
    return pl.pallas_call(
        _cast_kernel,
        grid=(2 * _N_FF_CHUNKS,),
        in_specs=[pl.BlockSpec((rows, _FF_CHUNK), lambda j: (0, (j % 2) * _N_FF_CHUNKS + j // 2))],
        out_specs=pl.BlockSpec((rows, _FF_CHUNK), lambda j: (0, j)),
        out_shape=jax.ShapeDtypeStruct((rows, 2 * D_FF), BF16),
        compiler_params=pltpu.CompilerParams(dimension_semantics=("parallel",)),
        name="regroup_up",
    )(w_up)


def _gate_cast_kernel(w_ref, sel_ref, o_ref):
    o_ref[...] = lax.dot_general(w_ref[...].astype(BF16), sel_ref[...], (((0,), (0,)), ((), ())),
                                 preferred_element_type=F32).astype(BF16)


def _gate_selector():
    sel = [[0.0] * (2 * _GATE_LANES) for _ in range(_N_GATES)]
    for d in range(2):
        for h in range(HEADS):
            j = d * HEADS + h
            sel[2 * d * HEADS + h][j] = 1.0
            sel[(2 * d + 1) * HEADS + h][_GATE_LANES + j] = 1.0
    return jnp.array(sel, BF16)


def _gate_weights(w_t):
    rows = w_t.shape[1]
    assert _W_GATES % _N_GATES == 0
    return pl.pallas_call(
        _gate_cast_kernel,
        grid=(1,),
        in_specs=[pl.BlockSpec((_N_GATES, rows), lambda k: (_W_GATES // _N_GATES, 0)),
                  pl.BlockSpec((_N_GATES, 2 * _GATE_LANES), lambda k: (0, 0))],
        out_specs=pl.BlockSpec((rows, 2 * _GATE_LANES), lambda k: (0, 0)),
        out_shape=jax.ShapeDtypeStruct((rows, 2 * _GATE_LANES), BF16),
        name="gate_weights",
    )(w_t, _gate_selector())


def _mod_kernel(cc_ref, w_ref, b_ref, o_ref):
    s = _silu(cc_ref[...])
    w = w_ref[...]
    rows = [jnp.sum(w * s[:, r:r + 1], axis=0, keepdims=True) + b_ref[...] for r in range(2)]
    o_ref[...] = jnp.concatenate(rows + [jnp.zeros((6, w.shape[1]), F32)], axis=0)


def _modulation(cc, w_mod, b_mod):
    n_out = w_mod.shape[1]
    nb = D_MODEL
    return pl.pallas_call(
        _mod_kernel,
        grid=(n_out // nb,),
        in_specs=[pl.BlockSpec((D_MODEL, 8), lambda j: (0, 0)),
                  pl.BlockSpec((D_MODEL, nb), lambda j: (0, j)),
                  pl.BlockSpec((1, nb), lambda j: (0, j))],
        out_specs=pl.BlockSpec((8, nb), lambda j: (0, j)),
        out_shape=jax.ShapeDtypeStruct((8, n_out), F32),
        compiler_params=pltpu.CompilerParams(dimension_semantics=("arbitrary",)),
        name="modulation",
    )(cc, w_mod, b_mod)


def _gate_tables(gi, lf):
    row = lax.broadcasted_iota(jnp.int32, (_L, _L), 0)
    col = lax.broadcasted_iota(jnp.int32, (_L, _L), 1)
    ltri = (col <= row).astype(BF16)
    l1, l2, l3 = _split3(lf)
    prefix = _dot(ltri, l3) + _dot(ltri, l2) + _dot(ltri, l1)
    total = prefix[_L - 1:_L, :]
    suffix = total - prefix + lf
    lane = lax.broadcasted_iota(jnp.int32, (1, _GATE_LANES), 1)
    cum = jnp.where(lane >= HEADS, suffix, prefix)
    a = total - cum + gi
    c = gi - cum
    c_rows = c.T[0:_N_STATES, :]
    a_rows = a.T[0:_N_STATES, :]
    pos = lax.broadcasted_iota(jnp.int32, (_N_STATES, _L), 1)
    fwd_max, bwd_max = c_rows, c_rows
    k = 1
    while k < _L:
        fwd_max = jnp.maximum(fwd_max, jnp.where(pos >= k, pltpu.roll(fwd_max, k, 1), -jnp.inf))
        bwd_max = jnp.maximum(bwd_max, jnp.where(pos < _L - k, pltpu.roll(bwd_max, _L - k, 1), -jnp.inf))
        k *= 2
    state = lax.broadcasted_iota(jnp.int32, (_N_STATES, _L), 0)
    run_max = jnp.where(state < HEADS, fwd_max, bwd_max)
    run_max_cols = jnp.concatenate([run_max, jnp.zeros((_GATE_LANES - _N_STATES, _L), F32)], axis=0).T
    return cum, cum + run_max_cols, a, c_rows, a_rows, total, jnp.max(a, axis=0, keepdims=True)


def _proj_kernel(x_ref, xp_ref, xn_ref, shift_ref, scale_ref, g1_ref, wqkv_ref, wg_ref, wrest_ref, bg_ref,
                 cw_ref, lng_ref, lnb_ref, wsd_ref, bs_ref, wbs_ref,
                 q_ref, kt_ref, v_ref, tab_ref, gr_ref, gm_ref, so_ref, sgm_ref, s_ref,
                 hn_s, z_s, ys_s, *, tm):
    i = pl.program_id(0)
    has_prev = i > 0
    has_next = i < pl.num_programs(0) - 1
    scale1 = 1.0 + scale_ref[...]

    def norm_mod(xb):
        return ((_rms(xb) * g1_ref[...]) * scale1 + shift_ref[...]).astype(BF16)

    hn_s[0:_HALO, :] = norm_mod(xp_ref[...])
    hn_s[_HALO:_HALO + tm, :] = norm_mod(x_ref[...])
    hn_s[_HALO + tm:, :] = norm_mod(xn_ref[...])
    hn = hn_s[_HALO:_HALO + tm, :]

    bw = _PROJ_BLOCK
    u_blocks, vg_blocks = [], []

    def qkv_cols(seg, b):
        return wqkv_ref[:, seg * D_MODEL + b * bw:seg * D_MODEL + (b + 1) * bw]

    def rest_cols(seg, b):
        return wrest_ref[:, seg * D_MODEL + b * bw:seg * D_MODEL + (b + 1) * bw]

    def qk_block(c, b):
        lanes = slice(b * bw, (b + 1) * bw)
        z = _dot(hn_s[...], qkv_cols(c, b))
        z_s[c, 0:_HALO, lanes] = jnp.where(has_prev, z[0:_HALO, :], 0.0)
        z_s[c, _HALO:_HALO + tm, lanes] = z[_HALO:_HALO + tm, :]
        z_s[c, _HALO + tm:, lanes] = jnp.where(has_next, z[_HALO + tm:, :], 0.0)
        cw = cw_ref[:, c * D_MODEL + b * bw:c * D_MODEL + (b + 1) * bw]
        y = (cw[0:1, :] * z_s[c, _HALO - 1:_HALO - 1 + tm, lanes] + cw[1:2, :] * z_s[c, _HALO:_HALO + tm, lanes]
             + cw[2:3, :] * z_s[c, _HALO + 1:_HALO + 1 + tm, lanes])
        y = _silu(y)
        if c == 0:
            q_ref[:, lanes] = (y * (HEAD_DIM ** -0.5)).astype(BF16)
        else:
            kt_ref[lanes, :] = y.T.astype(BF16)

    for b in range(D_MODEL // bw):
        lanes = slice(b * bw, (b + 1) * bw)
        qk_block(0, b)
        v_ref[:, lanes] = _dot(hn, qkv_cols(2, b)).astype(BF16)
        qk_block(1, b)
        so_ref[:, lanes] = _sigmoid(_dot(hn, rest_cols(0, b))).astype(BF16)
        u_blocks.append(_gelu_tanh(_dot(hn, rest_cols(1, b))))
        sgm_ref[:, lanes] = _sigmoid(_dot(hn, rest_cols(3, b))).astype(BF16)
        vg_blocks.append(_gelu_tanh(_dot(hn, rest_cols(2, b))))

    gates = _dot(hn, wg_ref[...]) + bg_ref[...]
    gi = gates[:, 0:_GATE_LANES]
    lf = _log_sigmoid(gates[:, _GATE_LANES:])
    for j in range(tm // _L):
        rows = slice(j * _L, (j + 1) * _L)
        cum, mi, a, c_rows, a_rows, total, a_max = _gate_tables(gi[rows, :], lf[rows, :])
        tab_ref[rows, 0:_GATE_LANES] = cum
        tab_ref[rows, _GATE_LANES:2 * _GATE_LANES] = mi
        tab_ref[rows, 2 * _GATE_LANES:] = a
        gr_ref[0:_N_STATES, rows] = c_rows
        gr_ref[_N_STATES:, rows] = a_rows
        gm_ref[j * 8:(j + 1) * 8, :] = jnp.concatenate([total, a_max, jnp.zeros((6, _GATE_LANES), F32)], axis=0)

    vg = jnp.concatenate(vg_blocks, axis=1)
    u = jnp.concatenate(u_blocks, axis=1)
    vg = vg - jnp.mean(vg, axis=-1, keepdims=True)
    vn = (vg * lax.rsqrt(jnp.mean(vg * vg, axis=-1, keepdims=True) + EPS) * lng_ref[...] + lnb_ref[...]).astype(BF16)
    for p in range(tm // _L):
        for g in range(HEADS):
            rows = slice(p * _L, (p + 1) * _L)
            cols = slice(g * HEAD_DIM, (g + 1) * HEAD_DIM)
            mixed = _dot(wsd_ref[g], vn[rows, cols]) + bs_ref[:, g:g + 1]
            ys_s[rows, cols] = (u[rows, cols] * mixed).astype(BF16)
    gg = _sigmoid(_dot(hn, wrest_ref[:, 4 * D_MODEL:5 * D_MODEL]))
    s_ref[...] = (gg * _dot(ys_s[...], wbs_ref[...])).astype(BF16)


def _projection(x, shift, scale, g1, wqkv, wg, wrest, bg, cw, lng, lnb, wsd, bs2, wbs, *, tm):
    t = x.shape[0]
    nt = t // tm
    hb = tm // _HALO
    nhb = t // _HALO
    tile = lambda w: pl.BlockSpec((tm, w), lambda i: (i, 0))
    in_specs = [
        tile(D_MODEL),
        pl.BlockSpec((_HALO, D_MODEL), lambda i: (jnp.maximum(i * hb - 1, 0), 0)),
        pl.BlockSpec((_HALO, D_MODEL), lambda i: (jnp.minimum((i + 1) * hb, nhb - 1), 0)),
    ] + [_const_spec(a.shape) for a in (shift, scale, g1, wqkv, wg, wrest, bg, cw, lng, lnb, wsd, bs2, wbs)]
    gm_rows = 8 * (tm // _L)
    out_specs = [tile(D_MODEL), pl.BlockSpec((D_MODEL, tm), lambda i: (0, i)), tile(D_MODEL), tile(_TAB_W),
                 pl.BlockSpec((2 * _N_STATES, tm), lambda i: (0, i)),
                 pl.BlockSpec((gm_rows, _GATE_LANES), lambda i: (i, 0)),
                 tile(D_MODEL), tile(D_MODEL), tile(D_MODEL)]
    act = jax.ShapeDtypeStruct((t, D_MODEL), BF16)
    out_shape = [act, jax.ShapeDtypeStruct((D_MODEL, t), BF16), act, jax.ShapeDtypeStruct((t, _TAB_W), F32),
                 jax.ShapeDtypeStruct((2 * _N_STATES, t), F32), jax.ShapeDtypeStruct((8 * (t // _L), _GATE_LANES), F32),
                 act, act, act]
    return pl.pallas_call(
        functools.partial(_proj_kernel, tm=tm),
        grid=(nt,),
        in_specs=in_specs,
        out_specs=out_specs,
        out_shape=out_shape,
        scratch_shapes=[pltpu.VMEM((tm + 2 * _HALO, D_MODEL), BF16),
                        pltpu.VMEM((2, tm + 2 * _HALO, D_MODEL), F32),
                        pltpu.VMEM((tm, D_MODEL), BF16)],
        compiler_params=pltpu.CompilerParams(dimension_semantics=("parallel",), vmem_limit_bytes=_VMEM_LIMIT_V7X),
        name="projection",
    )(x, x, x, shift, scale, g1, wqkv, wg, wrest, bg, cw, lng, lnb, wsd, bs2, wbs)


def _mlstm_kernel(qf_ref, ktf_ref, vf_ref, tabf_ref, grf_ref, gmf_ref, qb_ref, ktb_ref, vb_ref, tabb_ref, grb_ref, gmb_ref,
                  c0_ref, m0_ref, hf_ref, hb_ref, c_ref, m_ref):
    @pl.when(pl.program_id(0) == 0)
    def _():
        c_ref[...] = c0_ref[...]
        m_ref[...] = m0_ref[...]

    row = lax.broadcasted_iota(jnp.int32, (_L, _L), 0)
    col = lax.broadcasted_iota(jnp.int32, (_L, _L), 1)
    ones = jnp.ones((_L, _GATE_LANES), BF16)
    m_prev = m_ref[...]
    m_next = []
    dirs = ((qf_ref, ktf_ref, vf_ref, tabf_ref, grf_ref, gmf_ref, hf_ref, col <= row),
            (qb_ref, ktb_ref, vb_ref, tabb_ref, grb_ref, gmb_ref, hb_ref, col >= row))
    for d, (q_ref, kt_ref, v_ref, tab_ref, gr_ref, gm_ref, h_ref, mask) in enumerate(dirs):
        cum = tab_ref[:, 0:_GATE_LANES]
        m_row = jnp.maximum(cum + m_prev, tab_ref[:, _GATE_LANES:2 * _GATE_LANES])
        e = cum - m_row
        total, a_max = gm_ref[0:1, :], gm_ref[1:2, :]
        m_new = jnp.maximum(total + m_prev, a_max)
        a_old = jnp.exp(total + m_prev - m_new)
        m_next.append(m_new)
        gr = gr_ref[...]
        for hd in range(HEADS):
            j = d * HEADS + hd
            cols = slice(hd * HEAD_DIM, (hd + 1) * HEAD_DIM)
            qh, kth = q_ref[:, cols], kt_ref[cols, :]
            v_ext = jnp.concatenate([v_ref[:, cols], ones], axis=1)
            e_b = jnp.broadcast_to(e[:, j:j + 1], (_L, _GATE_LANES))
            m_row_b = jnp.broadcast_to(m_row[:, j:j + 1], (_L, _GATE_LANES))
            c_row, a_row = gr[j:j + 1, :], gr[_N_STATES + j:_N_STATES + j + 1, :]
            p = jnp.exp(jnp.where(mask, jnp.concatenate([e_b, e_b], axis=1) + c_row, -jnp.inf))
            s = (_dot(qh, kth) * p).astype(BF16)
            c_prev = c_ref[j]
            w_inter = jnp.exp(e_b + m_prev[:, j:j + 1])
            num = _dot(s, v_ext) + jnp.concatenate([w_inter] * 3, axis=1) * _dot(qh, c_prev.astype(BF16))
            inv = 1.0 / jnp.maximum(jnp.abs(num[:, HEAD_DIM:]), jnp.exp(-m_row_b))
            h_ref[:, cols] = (num[:, :HEAD_DIM] * jnp.concatenate([inv, inv], axis=1)).astype(BF16)
            kw_t = (kth.astype(F32) * jnp.exp(a_row - m_new[:, j:j + 1])).astype(BF16)
            c_ref[j] = a_old[:, j:j + 1] * c_prev + _dot(kw_t, v_ext)
    lane = lax.broadcasted_iota(jnp.int32, (1, _GATE_LANES), 1)
    m_ref[...] = jnp.where(lane < HEADS, m_next[0], m_next[1])


def _mlstm(q, kt, v, tab, gr, gm, c0, m0):
    t = q.shape[0]
    nc = t // _L
    fwd_i = lambda i: i
    bwd_i = lambda i: nc - 1 - i

    def specs(ix):
        rows = lambda w: pl.BlockSpec((_L, w), lambda i: (ix(i), 0))
        return [rows(D_MODEL), pl.BlockSpec((D_MODEL, _L), lambda i: (0, ix(i))), rows(D_MODEL), rows(_TAB_W),
                pl.BlockSpec((2 * _N_STATES, _L), lambda i: (0, ix(i))),
                pl.BlockSpec((8, _GATE_LANES), lambda i: (ix(i), 0))]

    out_rows = lambda ix: pl.BlockSpec((_L, D_MODEL), lambda i: (ix(i), 0))
    state_specs = [pl.BlockSpec(a.shape, lambda i, nd=a.ndim: (0,) * nd) for a in (c0, m0)]
    act = jax.ShapeDtypeStruct((t, D_MODEL), BF16)
    return pl.pallas_call(
        _mlstm_kernel,
        grid=(nc,),
        in_specs=specs(fwd_i) + specs(bwd_i) + [_const_spec(c0.shape), _const_spec(m0.shape)],
        out_specs=[out_rows(fwd_i), out_rows(bwd_i)] + state_specs,
        out_shape=[act, act] + [jax.ShapeDtypeStruct(a.shape, F32) for a in (c0, m0)],
        compiler_params=pltpu.CompilerParams(dimension_semantics=("arbitrary",), vmem_limit_bytes=_VMEM_LIMIT_V7X),
        name="mlstm",
    )(q, kt, v, tab, gr, gm, q, kt, v, tab, gr, gm, c0, m0)


def _post_kernel(hf_ref, hb_ref, so_ref, sgm_ref, s_ref, x_ref, hg_ref, wbm_ref, wout_ref, gate_ref,
                 g2_ref, shift_ref, scale_ref, hx_ref, hn2_ref, ym_s):
    hm = hf_ref[...].astype(F32) + hb_ref[...].astype(F32)
    for hd in range(HEADS):
        cols = slice(hd * HEAD_DIM, (hd + 1) * HEAD_DIM)
        ym = _rms(hm[:, cols]) * hg_ref[:, cols]
        ym_s[:, cols] = (so_ref[:, cols].astype(F32) * ym).astype(BF16)
    y = sgm_ref[...].astype(F32) * _dot(ym_s[...], wbm_ref[...]) + s_ref[...].astype(F32)
    hx = x_ref[...] + gate_ref[...] * _dot(y.astype(BF16), wout_ref[...])
    hx_ref[...] = hx
    hn2_ref[...] = ((_rms(hx) * g2_ref[...]) * (1.0 + scale_ref[...]) + shift_ref[...]).astype(BF16)


def _post(hf, hb, so, sgm, s, x, hg, wbm, wout, gate, g2, shift, scale, *, tm):
    t = x.shape[0]
    tile = lambda: pl.BlockSpec((tm, D_MODEL), lambda i: (i, 0))
    return pl.pallas_call(
        _post_kernel,
        grid=(t // tm,),
        in_specs=[tile() for _ in range(6)] + [_const_spec(a.shape) for a in (hg, wbm, wout, gate, g2, shift, scale)],
        out_specs=[tile(), tile()],
        out_shape=[jax.ShapeDtypeStruct((t, D_MODEL), F32), jax.ShapeDtypeStruct((t, D_MODEL), BF16)],
        scratch_shapes=[pltpu.VMEM((tm, D_MODEL), BF16)],
        compiler_params=pltpu.CompilerParams(dimension_semantics=("parallel",), vmem_limit_bytes=_VMEM_LIMIT_V7X),
        name="post_mixer",
    )(hf, hb, so, sgm, s, x, hg, wbm, wout, gate, g2, shift, scale)


def _ffn_kernel(hn_ref, hnp_ref, hnn_ref, hx_ref, wab_ref, cw_ref, wdn_ref, gate_ref, fg_ref, o_ref,
                hne_s, a_s, b_s, act_s, acc_s, *, tm):
    i = pl.program_id(0)
    has_above = i > 0
    has_below = i < pl.num_programs(0) - 1
    ext = tm + 2 * GRID_W
    hne_s[0:GRID_W, :] = hnp_ref[...]
    hne_s[GRID_W:GRID_W + tm, :] = hn_ref[...]
    hne_s[GRID_W + tm:, :] = hnn_ref[...]
    for slot in range(2):
        a_s[slot, 0:_PAD, :] = jnp.zeros((_PAD, _FF_CHUNK), F32)
        a_s[slot, _PAD + ext:, :] = jnp.zeros((_PAD, _FF_CHUNK), F32)
    gcol = lax.broadcasted_iota(jnp.int32, (ext, 1), 0) % GRID_W

    def up(c):
        slot = c % 2
        ab = _dot(hne_s[...], wab_ref[:, 2 * c * _FF_CHUNK:2 * (c + 1) * _FF_CHUNK])
        a = ab[:, :_FF_CHUNK]
        a_s[slot, _PAD:_PAD + GRID_W, :] = jnp.where(has_above, a[:GRID_W, :], 0.0)
        a_s[slot, _PAD + GRID_W:_PAD + GRID_W + tm, :] = a[GRID_W:GRID_W + tm, :]
        a_s[slot, _PAD + GRID_W + tm:_PAD + ext, :] = jnp.where(has_below, a[GRID_W + tm:, :], 0.0)
        b_s[slot] = ab[GRID_W:GRID_W + tm, _FF_CHUNK:]

    def mix(c):
        slot = c % 2
        taps = (jnp.where(gcol >= 1, a_s[slot, _PAD - 1:_PAD - 1 + ext, :], 0.0),
                a_s[slot, _PAD:_PAD + ext, :],
                jnp.where(gcol <= GRID_W - 2, a_s[slot, _PAD + 1:_PAD + 1 + ext, :], 0.0))
        cw = cw_ref[:, c * _FF_CHUNK:(c + 1) * _FF_CHUNK]
        conv = None
        for dr in range(3):
            for dc in range(3):
                term = cw[3 * dr + dc:3 * dr + dc + 1, :] * taps[dc][dr * GRID_W:dr * GRID_W + tm, :]
                conv = term if conv is None else conv + term
        act_s[slot] = (_silu(conv) * b_s[slot]).astype(BF16)

    def down(c):
        part = _dot(act_s[c % 2], wdn_ref[c * _FF_CHUNK:(c + 1) * _FF_CHUNK, :])
        if c == 0:
            acc_s[...] = part
        else:
            acc_s[...] += part

    for step in range(_N_FF_CHUNKS + 2):
        if 0 <= step - 2:
            down(step - 2)
        if step < _N_FF_CHUNKS:
            up(step)
        if 0 <= step - 1 < _N_FF_CHUNKS:
            mix(step - 1)
    h2 = hx_ref[...] + gate_ref[...] * acc_s[...]
    o_ref[...] = _rms(h2) * fg_ref[...]


def _ffn(hn2, hx, wab, cwf, wdn, gate, fg, *, tm):
    t = hx.shape[0]
    rb = tm // GRID_W
    nrb = t // GRID_W
    tile = lambda: pl.BlockSpec((tm, D_MODEL), lambda i: (i, 0))
    ext = tm + 2 * GRID_W
    return pl.pallas_call(
        functools.partial(_ffn_kernel, tm=tm),
        grid=(t // tm,),
        in_specs=[tile(),
                  pl.BlockSpec((GRID_W, D_MODEL), lambda i: (jnp.maximum(i * rb - 1, 0), 0)),
                  pl.BlockSpec((GRID_W, D_MODEL), lambda i: (jnp.minimum((i + 1) * rb, nrb - 1), 0)),
                  tile()] + [_const_spec(a.shape) for a in (wab, cwf, wdn, gate, fg)],
        out_specs=tile(),
        out_shape=jax.ShapeDtypeStruct((t, D_MODEL), F32),
        scratch_shapes=[pltpu.VMEM((ext, D_MODEL), BF16),
                        pltpu.VMEM((2, ext + 2 * _PAD, _FF_CHUNK), F32),
                        pltpu.VMEM((2, tm, _FF_CHUNK), F32),
                        pltpu.VMEM((2, tm, _FF_CHUNK), BF16),
                        pltpu.VMEM((tm, D_MODEL), F32)],
        compiler_params=pltpu.CompilerParams(dimension_semantics=("parallel",), vmem_limit_bytes=_VMEM_LIMIT_V7X),
        name="conv_ffn",
    )(hn2, hn2, hn2, hx, wab, cwf, wdn, gate, fg)


def _gate_columns(wg):
    idx_i = [2 * d * HEADS + h for d in range(2) for h in range(HEADS)]
    idx_f = [(2 * d + 1) * HEADS + h for d in range(2) for h in range(HEADS)]
    pad = [(0, 0)] * (wg.ndim - 1) + [(0, _GATE_LANES - _N_STATES)]
    return jnp.concatenate([jnp.pad(wg[..., jnp.array(idx)], pad) for idx in (idx_i, idx_f)], axis=-1)


def _tile_size(t, want):
    tm = min(want, t)
    assert t % tm == 0 and tm % _L == 0, (t, tm)
    return tm


def kernel(x, c, ctx, c_ctx, w_mod, b_mod, norm1_g, w_in, b_gate, conv_qk, head_norm_g, sgu_ln_g, sgu_ln_b, w_s, b_s,
           w_branch_mlstm, w_branch_sgu, w_out, norm2_g, w_up, w_ffn_conv, w_down, final_g):
    assert x.shape[0] == 1 and w_mod.shape[0] == 1, "single batch element, single layer"
    d = D_MODEL
    x2, ctx2 = x[0], ctx[0]
    t = x2.shape[0]
    assert t % GRID_W == 0 and ctx2.shape[0] % _L == 0
    row = lambda a: a.reshape(1, -1)

    cc = jnp.zeros((d, 8), F32).at[:, 0].set(c[0]).at[:, 1].set(c_ctx)
    mod = _modulation(cc, w_mod[0], row(b_mod[0]))
    mx = [mod[0:1, j * d:(j + 1) * d] for j in range(N_MOD)]
    mc = [mod[1:2, j * d:(j + 1) * d] for j in range(N_MOD)]

    w_in_t = jnp.swapaxes(w_in[0], 0, 1)
    wqkv = _cast_columns(w_in_t, 0, _W_GATES)
    wg = _gate_weights(w_in_t)
    wrest = _cast_columns(w_in_t, _W_REST, w_in_t.shape[0] - _W_REST)
    bg = _gate_columns(row(b_gate[0]))
    eye2 = jnp.eye(2, dtype=F32)
    wsd = jnp.stack([jnp.kron(eye2, w_s[0, g]) for g in range(HEADS)]).astype(BF16)
    bs2 = jnp.tile(b_s[0].T, (2, 1))
    proj_w = (row(norm1_g[0]), wqkv, wg, wrest, bg, conv_qk[0],
              row(sgu_ln_g[0]), row(sgu_ln_b[0]), wsd, bs2, w_branch_sgu[0].astype(BF16))

    pc = _projection(ctx2, mc[0], mc[1], *proj_w, tm=_tile_size(ctx2.shape[0], 512))
    c0 = jnp.zeros((_N_STATES, HEAD_DIM, _C_EXT), F32)
    m0 = jnp.full((1, _GATE_LANES), M_INIT, F32)
    _, _, c1, m1 = _mlstm(*pc[:6], c0, m0)

    q, kt, v, tab, gr, gm, so, sgm, s = _projection(x2, mx[0], mx[1], *proj_w, tm=_tile_size(t, 512))
    hf, hb, _, _ = _mlstm(q, kt, v, tab, gr, gm, c1, m1)
    hx, hn2 = _post(hf, hb, so, sgm, s, x2, row(head_norm_g[0]), w_branch_mlstm[0].astype(BF16),
                    w_out[0].astype(BF16), mx[2], row(norm2_g[0]), mx[3], mx[4], tm=_tile_size(t, 512))

    out = _ffn(hn2, hx, _regroup_up(w_up[0]), w_ffn_conv[0].reshape(9, D_FF), w_down[0].astype(BF16), mx[5],
               row(final_g), tm=_tile_size(t, 1024))
    return out[None]
```

```python
import functools

import jax
import jax.numpy as jnp
from jax import lax
from jax.experimental import pallas as pl
from jax.experimental.pallas import tpu as pltpu

F32 = jnp.float32
BF16 = jnp.bfloat16

D_MODEL = 1024
GRID_W = 64
HEADS = 4
HEAD_DIM = 256
N_MOD = 6
D_FF = 2816
EPS = 1e-6
M_INIT = -1e30

_L = 256
_HALO = 16
_FF_CHUNK = 256
_N_FF_CHUNKS = D_FF // _FF_CHUNK
_GATE_LANES = 128
_PAD = 8
_PROJ_BLOCK = 256
_VMEM_LIMIT_V7X = 56 * 1024 * 1024

_N_STATES = 2 * HEADS
_TAB_W = 3 * _GATE_LANES
_C_EXT = HEAD_DIM + _GATE_LANES
_N_GATES = 4 * HEADS
_W_GATES = 3 * D_MODEL
_W_REST = _W_GATES + _N_GATES
_SUBLANES = 8
_PREP_BLOCK = 512


def _dot(a, b):
    return jnp.dot(a, b, preferred_element_type=F32)


def _split3(x):
    h1 = x.astype(BF16)
    r1 = x - h1.astype(F32)
    h2 = r1.astype(BF16)
    h3 = (r1 - h2.astype(F32)).astype(BF16)
    return h1, h2, h3


def _sigmoid(x):
    return 0.5 * jnp.tanh(0.5 * x) + 0.5


def _silu(x):
    h = 0.5 * x
    return h * jnp.tanh(h) + h


def _gelu_tanh(x):
    return 0.5 * x * (1.0 + jnp.tanh(0.7978845608028654 * (x + 0.044715 * (x * x * x))))


def _log_sigmoid(x):
    return jnp.minimum(x, 0.0) - jnp.log(1.0 + jnp.exp(-jnp.abs(x)))


def _rms(x):
    return x * lax.rsqrt(jnp.mean(x * x, axis=-1, keepdims=True) + EPS)


def _const_spec(shape):
    nd = len(shape)
    return pl.BlockSpec(shape, lambda *_: (0,) * nd, pipeline_mode=pl.Buffered(1))


def _cast_kernel(w_ref, o_ref):
    o_ref[...] = w_ref[...].astype(BF16)


def _transpose_cast_kernel(w_ref, o_ref):
    o_ref[...] = w_ref[...].T.astype(BF16)


def _shift_transpose_cast_kernel(w_ref, next_ref, o_ref, *, shift):
    x = jnp.concatenate([w_ref[...], next_ref[...]], axis=0)
    o_ref[...] = x[shift:shift + w_ref.shape[0], :].T.astype(BF16)


def _cast_columns(w_t, col0, ncols):
    rows = w_t.shape[1]
    shift = col0 % _PREP_BLOCK
    base = col0 - shift
    assert shift % _SUBLANES == 0 and ncols % _PREP_BLOCK == 0 and col0 + ncols <= w_t.shape[0]
    b0 = base // _PREP_BLOCK
    in_specs = [pl.BlockSpec((_PREP_BLOCK, rows), lambda k: (b0 + k, 0))]
    operands = [w_t]
    body = _transpose_cast_kernel
    if shift:
        assert _PREP_BLOCK % shift == 0
        per_block = _PREP_BLOCK // shift
        in_specs.append(pl.BlockSpec((shift, rows), lambda k: ((b0 + k + 1) * per_block, 0)))
        operands.append(w_t)
        body = functools.partial(_shift_transpose_cast_kernel, shift=shift)
    return pl.pallas_call(
        body,
        grid=(ncols // _PREP_BLOCK,),
        in_specs=in_specs,
        out_specs=pl.BlockSpec((rows, _PREP_BLOCK), lambda k: (0, k)),
        out_shape=jax.ShapeDtypeStruct((rows, ncols), BF16),
        compiler_params=pltpu.CompilerParams(dimension_semantics=("parallel",)),
        name="cast_columns",
    )(*operands)


def _regroup_up(w_up):
    rows = w_up.shape[0]
    return pl.pallas_call(
        _cast_kernel,
        grid=(2 * _N_FF_CHUNKS,),
        in_specs=[pl.BlockSpec((rows, _FF_CHUNK), lambda j: (0, (j % 2) * _N_FF_CHUNKS + j // 2))],
        out_specs=pl.BlockSpec((rows, _FF_CHUNK), lambda j: (0, j)),
        out_shape=jax.ShapeDtypeStruct((rows, 2 * D_FF), BF16),
        compiler_params=pltpu.CompilerParams(dimension_semantics=("parallel",)),
        name="regroup_up",
    )(w_up)


def _gate_cast_kernel(w_ref, sel_ref, o_ref):
    o_ref[...] = lax.dot_general(w_ref[...].astype(BF16), sel_ref[...], (((0,), (0,)), ((), ())),
                                 preferred_element_type=F32).astype(BF16)


def _gate_selector():
    sel = [[0.0] * (2 * _GATE_LANES) for _ in range(_N_GATES)]
    for d in range(2):
        for h in range(HEADS):
            j = d * HEADS + h
            sel[2 * d * HEADS + h][j] = 1.0
            sel[(2 * d + 1) * HEADS + h][_GATE_LANES + j] = 1.0
    return jnp.array(sel, BF16)


def _gate_weights(w_t):
    rows = w_t.shape[1]
    assert _W_GATES % _N_GATES == 0
    return pl.pallas_call(
        _gate_cast_kernel,
        grid=(1,),
        in_specs=[pl.BlockSpec((_N_GATES, rows), lambda k: (_W_GATES // _N_GATES, 0)),
                  pl.BlockSpec((_N_GATES, 2 * _GATE_LANES), lambda k: (0, 0))],
        out_specs=pl.BlockSpec((rows, 2 * _GATE_LANES), lambda k: (0, 0)),
        out_shape=jax.ShapeDtypeStruct((rows, 2 * _GATE_LANES), BF16),
        name="gate_weights",
    )(w_t, _gate_selector())


def _mod_kernel(cc_ref, w_ref, b_ref, o_ref):
    s = _silu(cc_ref[...])
    w = w_ref[...]
    rows = [jnp.sum(w * s[:, r:r + 1], axis=0, keepdims=True) + b_ref[...] for r in range(2)]
    o_ref[...] = jnp.concatenate(rows + [jnp.zeros((6, w.shape[1]), F32)], axis=0)


def _modulation(cc, w_mod, b_mod):
    n_out = w_mod.shape[1]
    nb = D_MODEL
    return pl.pallas_call(
        _mod_kernel,
        grid=(n_out // nb,),
        in_specs=[pl.BlockSpec((D_MODEL, 8), lambda j: (0, 0)),
                  pl.BlockSpec((D_MODEL, nb), lambda j: (0, j)),
                  pl.BlockSpec((1, nb), lambda j: (0, j))],
        out_specs=pl.BlockSpec((8, nb), lambda j: (0, j)),
        out_shape=jax.ShapeDtypeStruct((8, n_out), F32),
        compiler_params=pltpu.CompilerParams(dimension_semantics=("arbitrary",)),
        name="modulation",
    )(cc, w_mod, b_mod)


def _gate_tables(gi, lf):
    row = lax.broadcasted_iota(jnp.int32, (_L, _L), 0)
    col = lax.broadcasted_iota(jnp.int32, (_L, _L), 1)
    ltri = (col <= row).astype(BF16)
    l1, l2, l3 = _split3(lf)
    prefix = _dot(ltri, l3) + _dot(ltri, l2) + _dot(ltri, l1)
    total = prefix[_L - 1:_L, :]
    suffix = total - prefix + lf
    lane = lax.broadcasted_iota(jnp.int32, (1, _GATE_LANES), 1)
    cum = jnp.where(lane >= HEADS, suffix, prefix)
    a = total - cum + gi
    c = gi - cum
    c_rows = c.T[0:_N_STATES, :]
    a_rows = a.T[0:_N_STATES, :]
    pos = lax.broadcasted_iota(jnp.int32, (_N_STATES, _L), 1)
    fwd_max, bwd_max = c_rows, c_rows
    k = 1
    while k < _L:
        fwd_max = jnp.maximum(fwd_max, jnp.where(pos >= k, pltpu.roll(fwd_max, k, 1), -jnp.inf))
        bwd_max = jnp.maximum(bwd_max, jnp.where(pos < _L - k, pltpu.roll(bwd_max, _L - k, 1), -jnp.inf))
        k *= 2
    state = lax.broadcasted_iota(jnp.int32, (_N_STATES, _L), 0)
    run_max = jnp.where(state < HEADS, fwd_max, bwd_max)
    run_max_cols = jnp.concatenate([run_max, jnp.zeros((_GATE_LANES - _N_STATES, _L), F32)], axis=0).T
    return cum, cum + run_max_cols, a, c_rows, a_rows, total, jnp.max(a, axis=0, keepdims=True)


def _proj_kernel(x_ref, xp_ref, xn_ref, shift_ref, scale_ref, g1_ref, wqkv_ref, wg_ref, wrest_ref, bg_ref,
                 cw_ref, lng_ref, lnb_ref, wsd_ref, bs_ref, wbs_ref,
                 q_ref, kt_ref, v_ref, tab_ref, gr_ref, gm_ref, so_ref, sgm_ref, s_ref,
                 hn_s, z_s, ys_s, *, tm):
    i = pl.program_id(0)
    has_prev = i > 0
    has_next = i < pl.num_programs(0) - 1
    scale1 = 1.0 + scale_ref[...]

    def norm_mod(xb):
        return ((_rms(xb) * g1_ref[...]) * scale1 + shift_ref[...]).astype(BF16)

    hn_s[0:_HALO, :] = norm_mod(xp_ref[...])
    hn_s[_HALO:_HALO + tm, :] = norm_mod(x_ref[...])
    hn_s[_HALO + tm:, :] = norm_mod(xn_ref[...])
    hn = hn_s[_HALO:_HALO + tm, :]

    bw = _PROJ_BLOCK
    u_blocks, vg_blocks = [], []

    def qkv_cols(seg, b):
        return wqkv_ref[:, seg * D_MODEL + b * bw:seg * D_MODEL + (b + 1) * bw]

    def rest_cols(seg, b):
        return wrest_ref[:, seg * D_MODEL + b * bw:seg * D_MODEL + (b + 1) * bw]

    def qk_block(c, b):
        lanes = slice(b * bw, (b + 1) * bw)
        z = _dot(hn_s[...], qkv_cols(c, b))
        z_s[c, 0:_HALO, lanes] = jnp.where(has_prev, z[0:_HALO, :], 0.0)
        z_s[c, _HALO:_HALO + tm, lanes] = z[_HALO:_HALO + tm, :]
        z_s[c, _HALO + tm:, lanes] = jnp.where(has_next, z[_HALO + tm:, :], 0.0)
        cw = cw_ref[:, c * D_MODEL + b * bw:c * D_MODEL + (b + 1) * bw]
        y = (cw[0:1, :] * z_s[c, _HALO - 1:_HALO - 1 + tm, lanes] + cw[1:2, :] * z_s[c, _HALO:_HALO + tm, lanes]
             + cw[2:3, :] * z_s[c, _HALO + 1:_HALO + 1 + tm, lanes])
        y = _silu(y)
        if c == 0:
            q_ref[:, lanes] = (y * (HEAD_DIM ** -0.5)).astype(BF16)
        else:
            kt_ref[lanes, :] = y.T.astype(BF16)

    for b in range(D_MODEL // bw):
        lanes = slice(b * bw, (b + 1) * bw)
        qk_block(0, b)
        v_ref[:, lanes] = _dot(hn, qkv_cols(2, b)).astype(BF16)
        qk_block(1, b)
        so_ref[:, lanes] = _sigmoid(_dot(hn, rest_cols(0, b))).astype(BF16)
        u_blocks.append(_gelu_tanh(_dot(hn, rest_cols(1, b))))
        sgm_ref[:, lanes] = _sigmoid(_dot(hn, rest_cols(3, b))).astype(BF16)
        vg_blocks.append(_gelu_tanh(_dot(hn, rest_cols(2, b))))

    gates = _dot(hn, wg_ref[...]) + bg_ref[...]
    gi = gates[:, 0:_GATE_LANES]
    lf = _log_sigmoid(gates[:, _GATE_LANES:])
    for j in range(tm // _L):
        rows = slice(j * _L, (j + 1) * _L)
        cum, mi, a, c_rows, a_rows, total, a_max = _gate_tables(gi[rows, :], lf[rows, :])
        tab_ref[rows, 0:_GATE_LANES] = cum
        tab_ref[rows, _GATE_LANES:2 * _GATE_LANES] = mi
        tab_ref[rows, 2 * _GATE_LANES:] = a
        gr_ref[0:_N_STATES, rows] = c_rows
        gr_ref[_N_STATES:, rows] = a_rows
        gm_ref[j * 8:(j + 1) * 8, :] = jnp.concatenate([total, a_max, jnp.zeros((6, _GATE_LANES), F32)], axis=0)

    vg = jnp.concatenate(vg_blocks, axis=1)
    u = jnp.concatenate(u_blocks, axis=1)
    vg = vg - jnp.mean(vg, axis=-1, keepdims=True)
    vn = (vg * lax.rsqrt(jnp.mean(vg * vg, axis=-1, keepdims=True) + EPS) * lng_ref[...] + lnb_ref[...]).astype(BF16)
    for p in range(tm // _L):
        for g in range(HEADS):
            rows = slice(p * _L, (p + 1) * _L)
            cols = slice(g * HEAD_DIM, (g + 1) * HEAD_DIM)
            mixed = _dot(wsd_ref[g], vn[rows, cols]) + bs_ref[:, g:g + 1]
            ys_s[rows, cols] = (u[rows, cols] * mixed).astype(BF16)
    gg = _sigmoid(_dot(hn, wrest_ref[:, 4 * D_MODEL:5 * D_MODEL]))
    s_ref[...] = (gg * _dot(ys_s[...], wbs_ref[...])).astype(BF16)


def _projection(x, shift, scale, g1, wqkv, wg, wrest, bg, cw, lng, lnb, wsd, bs2, wbs, *, tm):
    t = x.shape[0]
    nt = t // tm
    hb = tm // _HALO
    nhb = t // _HALO
    tile = lambda w: pl.BlockSpec((tm, w), lambda i: (i, 0))
    in_specs = [
        tile(D_MODEL),
        pl.BlockSpec((_HALO, D_MODEL), lambda i: (jnp.maximum(i * hb - 1, 0), 0)),
        pl.BlockSpec((_HALO, D_MODEL), lambda i: (jnp.minimum((i + 1) * hb, nhb - 1), 0)),
    ] + [_const_spec(a.shape) for a in (shift, scale, g1, wqkv, wg, wrest, bg, cw, lng, lnb, wsd, bs2, wbs)]
    gm_rows = 8 * (tm // _L)
    out_specs = [tile(D_MODEL), pl.BlockSpec((D_MODEL, tm), lambda i: (0, i)), tile(D_MODEL), tile(_TAB_W),
                 pl.BlockSpec((2 * _N_STATES, tm), lambda i: (0, i)),
                 pl.BlockSpec((gm_rows, _GATE_LANES), lambda i: (i, 0)),
                 tile(D_MODEL), tile(D_MODEL), tile(D_MODEL)]
    act = jax.ShapeDtypeStruct((t, D_MODEL), BF16)
    out_shape = [act, jax.ShapeDtypeStruct((D_MODEL, t), BF16), act, jax.ShapeDtypeStruct((t, _TAB_W), F32),
                 jax.ShapeDtypeStruct((2 * _N_STATES, t), F32), jax.ShapeDtypeStruct((8 * (t // _L), _GATE_LANES), F32),
                 act, act, act]
    return pl.pallas_call(
        functools.partial(_proj_kernel, tm=tm),
        grid=(nt,),
        in_specs=in_specs,
        out_specs=out_specs,
        out_shape=out_shape,
        scratch_shapes=[pltpu.VMEM((tm + 2 * _HALO, D_MODEL), BF16),
                        pltpu.VMEM((2, tm + 2 * _HALO, D_MODEL), F32),
                        pltpu.VMEM((tm, D_MODEL), BF16)],
        compiler_params=pltpu.CompilerParams(dimension_semantics=("parallel",), vmem_limit_bytes=_VMEM_LIMIT_V7X),
        name="projection",
    )(x, x, x, shift, scale, g1, wqkv, wg, wrest, bg, cw, lng, lnb, wsd, bs2, wbs)


def _mlstm_kernel(qf_ref, ktf_ref, vf_ref, tabf_ref, grf_ref, gmf_ref, qb_ref, ktb_ref, vb_ref, tabb_ref, grb_ref, gmb_ref,
                  c0_ref, m0_ref, hf_ref, hb_ref, c_ref, m_ref):
    @pl.when(pl.program_id(0) == 0)
    def _():
        c_ref[...] = c0_ref[...]
        m_ref[...] = m0_ref[...]

    row = lax.broadcasted_iota(jnp.int32, (_L, _L), 0)
    col = lax.broadcasted_iota(jnp.int32, (_L, _L), 1)
    ones = jnp.ones((_L, _GATE_LANES), BF16)
    m_prev = m_ref[...]
    m_next = []
    dirs = ((qf_ref, ktf_ref, vf_ref, tabf_ref, grf_ref, gmf_ref, hf_ref, col <= row),
            (qb_ref, ktb_ref, vb_ref, tabb_ref, grb_ref, gmb_ref, hb_ref, col >= row))
    for d, (q_ref, kt_ref, v_ref, tab_ref, gr_ref, gm_ref, h_ref, mask) in enumerate(dirs):
        cum = tab_ref[:, 0:_GATE_LANES]
        m_row = jnp.maximum(cum + m_prev, tab_ref[:, _GATE_LANES:2 * _GATE_LANES])
        e = cum - m_row
        total, a_max = gm_ref[0:1, :], gm_ref[1:2, :]
        m_new = jnp.maximum(total + m_prev, a_max)
        a_old = jnp.exp(total + m_prev - m_new)
        m_next.append(m_new)
        gr = gr_ref[...]
        for hd in range(HEADS):
            j = d * HEADS + hd
            cols = slice(hd * HEAD_DIM, (hd + 1) * HEAD_DIM)
            qh, kth = q_ref[:, cols], kt_ref[cols, :]
            v_ext = jnp.concatenate([v_ref[:, cols], ones], axis=1)
            e_b = jnp.broadcast_to(e[:, j:j + 1], (_L, _GATE_LANES))
            m_row_b = jnp.broadcast_to(m_row[:, j:j + 1], (_L, _GATE_LANES))
            c_row, a_row = gr[j:j + 1, :], gr[_N_STATES + j:_N_STATES + j + 1, :]
            p = jnp.exp(jnp.where(mask, jnp.concatenate([e_b, e_b], axis=1) + c_row, -jnp.inf))
            s = (_dot(qh, kth) * p).astype(BF16)
            c_prev = c_ref[j]
            w_inter = jnp.exp(e_b + m_prev[:, j:j + 1])
            num = _dot(s, v_ext) + jnp.concatenate([w_inter] * 3, axis=1) * _dot(qh, c_prev.astype(BF16))
            inv = 1.0 / jnp.maximum(jnp.abs(num[:, HEAD_DIM:]), jnp.exp(-m_row_b))
            h_ref[:, cols] = (num[:, :HEAD_DIM] * jnp.concatenate([inv, inv], axis=1)).astype(BF16)
            kw_t = (kth.astype(F32) * jnp.exp(a_row - m_new[:, j:j + 1])).astype(BF16)
            c_ref[j] = a_old[:, j:j + 1] * c_prev + _dot(kw_t, v_ext)
    lane = lax.broadcasted_iota(jnp.int32, (1, _GATE_LANES), 1)
    m_ref[...] = jnp.where(lane < HEADS, m_next[0], m_next[1])


def _mlstm(q, kt, v, tab, gr, gm, c0, m0):
    t = q.shape[0]
    nc = t // _L
    fwd_i = lambda i: i
    bwd_i = lambda i: nc - 1 - i

    def specs(ix):
        rows = lambda w: pl.BlockSpec((_L, w), lambda i: (ix(i), 0))
        return [rows(D_MODEL), pl.BlockSpec((D_MODEL, _L), lambda i: (0, ix(i))), rows(D_MODEL), rows(_TAB_W),
                pl.BlockSpec((2 * _N_STATES, _L), lambda i: (0, ix(i))),
                pl.BlockSpec((8, _GATE_LANES), lambda i: (ix(i), 0))]

    out_rows = lambda ix: pl.BlockSpec((_L, D_MODEL), lambda i: (ix(i), 0))
    state_specs = [pl.BlockSpec(a.shape, lambda i, nd=a.ndim: (0,) * nd) for a in (c0, m0)]
    act = jax.ShapeDtypeStruct((t, D_MODEL), BF16)
    return pl.pallas_call(
        _mlstm_kernel,
        grid=(nc,),
        in_specs=specs(fwd_i) + specs(bwd_i) + [_const_spec(c0.shape), _const_spec(m0.shape)],
        out_specs=[out_rows(fwd_i), out_rows(bwd_i)] + state_specs,
        out_shape=[act, act] + [jax.ShapeDtypeStruct(a.shape, F32) for a in (c0, m0)],
        compiler_params=pltpu.CompilerParams(dimension_semantics=("arbitrary",), vmem_limit_bytes=_VMEM_LIMIT_V7X),
        name="mlstm",
    )(q, kt, v, tab, gr, gm, q, kt, v, tab, gr, gm, c0, m0)


def _post_kernel(hf_ref, hb_ref, so_ref, sgm_ref, s_ref, x_ref, hg_ref, wbm_ref, wout_ref, gate_ref,
                 g2_ref, shift_ref, scale_ref, hx_ref, hn2_ref, ym_s):
    hm = hf_ref[...].astype(F32) + hb_ref[...].astype(F32)
    for hd in range(HEADS):
        cols = slice(hd * HEAD_DIM, (hd + 1) * HEAD_DIM)
        ym = _rms(hm[:, cols]) * hg_ref[:, cols]
        ym_s[:, cols] = (so_ref[:, cols].astype(F32) * ym).astype(BF16)
    y = sgm_ref[...].astype(F32) * _dot(ym_s[...], wbm_ref[...]) + s_ref[...].astype(F32)
    hx = x_ref[...] + gate_ref[...] * _dot(y.astype(BF16), wout_ref[...])
    hx_ref[...] = hx
    hn2_ref[...] = ((_rms(hx) * g2_ref[...]) * (1.0 + scale_ref[...]) + shift_ref[...]).astype(BF16)


def _post(hf, hb, so, sgm, s, x, hg, wbm, wout, gate, g2, shift, scale, *, tm):
    t = x.shape[0]
    tile = lambda: pl.BlockSpec((tm, D_MODEL), lambda i: (i, 0))
    return pl.pallas_call(
        _post_kernel,
        grid=(t // tm,),
        in_specs=[tile() for _ in range(6)] + [_const_spec(a.shape) for a in (hg, wbm, wout, gate, g2, shift, scale)],
        out_specs=[tile(), tile()],
        out_shape=[jax.ShapeDtypeStruct((t, D_MODEL), F32), jax.ShapeDtypeStruct((t, D_MODEL), BF16)],
        scratch_shapes=[pltpu.VMEM((tm, D_MODEL), BF16)],
        compiler_params=pltpu.CompilerParams(dimension_semantics=("parallel",), vmem_limit_bytes=_VMEM_LIMIT_V7X),
        name="post_mixer",
    )(hf, hb, so, sgm, s, x, hg, wbm, wout, gate, g2, shift, scale)


def _ffn_kernel(hn_ref, hnp_ref, hnn_ref, hx_ref, wab_ref, cw_ref, wdn_ref, gate_ref, fg_ref, o_ref,
                hne_s, a_s, b_s, act_s, acc_s, *, tm):
    i = pl.program_id(0)
    has_above = i > 0
    has_below = i < pl.num_programs(0) - 1
    ext = tm + 2 * GRID_W
    hne_s[0:GRID_W, :] = hnp_ref[...]
    hne_s[GRID_W:GRID_W + tm, :] = hn_ref[...]
    hne_s[GRID_W + tm:, :] = hnn_ref[...]
    for slot in range(2):
        a_s[slot, 0:_PAD, :] = jnp.zeros((_PAD, _FF_CHUNK), F32)
        a_s[slot, _PAD + ext:, :] = jnp.zeros((_PAD, _FF_CHUNK), F32)
    gcol = lax.broadcasted_iota(jnp.int32, (ext, 1), 0) % GRID_W

    def up(c):
        slot = c % 2
        ab = _dot(hne_s[...], wab_ref[:, 2 * c * _FF_CHUNK:2 * (c + 1) * _FF_CHUNK])
        a = ab[:, :_FF_CHUNK]
        a_s[slot, _PAD:_PAD + GRID_W, :] = jnp.where(has_above, a[:GRID_W, :], 0.0)
        a_s[slot, _PAD + GRID_W:_PAD + GRID_W + tm, :] = a[GRID_W:GRID_W + tm, :]
        a_s[slot, _PAD + GRID_W + tm:_PAD + ext, :] = jnp.where(has_below, a[GRID_W + tm:, :], 0.0)
        b_s[slot] = ab[GRID_W:GRID_W + tm, _FF_CHUNK:]

    def mix(c):
        slot = c % 2
        taps = (jnp.where(gcol >= 1, a_s[slot, _PAD - 1:_PAD - 1 + ext, :], 0.0),
                a_s[slot, _PAD:_PAD + ext, :],
                jnp.where(gcol <= GRID_W - 2, a_s[slot, _PAD + 1:_PAD + 1 + ext, :], 0.0))
        cw = cw_ref[:, c * _FF_CHUNK:(c + 1) * _FF_CHUNK]
        conv = None
        for dr in range(3):
            for dc in range(3):
                term = cw[3 * dr + dc:3 * dr + dc + 1, :] * taps[dc][dr * GRID_W:dr * GRID_W + tm, :]
                conv = term if conv is None else conv + term
        act_s[slot] = (_silu(conv) * b_s[slot]).astype(BF16)

    def down(c):
        part = _dot(act_s[c % 2], wdn_ref[c * _FF_CHUNK:(c + 1) * _FF_CHUNK, :])
        if c == 0:
            acc_s[...] = part
        else:
            acc_s[...] += part

    for step in range(_N_FF_CHUNKS + 2):
        if 0 <= step - 2:
            down(step - 2)
        if step < _N_FF_CHUNKS:
            up(step)
        if 0 <= step - 1 < _N_FF_CHUNKS:
            mix(step - 1)
    h2 = hx_ref[...] + gate_ref[...] * acc_s[...]
    o_ref[...] = _rms(h2) * fg_ref[...]


def _ffn(hn2, hx, wab, cwf, wdn, gate, fg, *, tm):
    t = hx.shape[0]
    rb = tm // GRID_W
    nrb = t // GRID_W
    tile = lambda: pl.BlockSpec((tm, D_MODEL), lambda i: (i, 0))
    ext = tm + 2 * GRID_W
    return pl.pallas_call(
        functools.partial(_ffn_kernel, tm=tm),
        grid=(t // tm,),
        in_specs=[tile(),
                  pl.BlockSpec((GRID_W, D_MODEL), lambda i: (jnp.maximum(i * rb - 1, 0), 0)),
                  pl.BlockSpec((GRID_W, D_MODEL), lambda i: (jnp.minimum((i + 1) * rb, nrb - 1), 0)),
                  tile()] + [_const_spec(a.shape) for a in (wab, cwf, wdn, gate, fg)],
        out_specs=tile(),
        out_shape=jax.ShapeDtypeStruct((t, D_MODEL), F32),
        scratch_shapes=[pltpu.VMEM((ext, D_MODEL), BF16),
                        pltpu.VMEM((2, ext + 2 * _PAD, _FF_CHUNK), F32),
                        pltpu.VMEM((2, tm, _FF_CHUNK), F32),
                        pltpu.VMEM((2, tm, _FF_CHUNK), BF16),
                        pltpu.VMEM((tm, D_MODEL), F32)],
        compiler_params=pltpu.CompilerParams(dimension_semantics=("parallel",), vmem_limit_bytes=_VMEM_LIMIT_V7X),
        name="conv_ffn",
    )(hn2, hn2, hn2, hx, wab, cwf, wdn, gate, fg)


def _gate_columns(wg):
    idx_i = [2 * d * HEADS + h for d in range(2) for h in range(HEADS)]
    idx_f = [(2 * d + 1) * HEADS + h for d in range(2) for h in range(HEADS)]
    pad = [(0, 0)] * (wg.ndim - 1) + [(0, _GATE_LANES - _N_STATES)]
    return jnp.concatenate([jnp.pad(wg[..., jnp.array(idx)], pad) for idx in (idx_i, idx_f)], axis=-1)


def _tile_size(t, want):
    tm = min(want, t)
    assert t % tm == 0 and tm % _L == 0, (t, tm)
    return tm


def kernel(x, c, ctx, c_ctx, w_mod, b_mod, norm1_g, w_in, b_gate, conv_qk, head_norm_g, sgu_ln_g, sgu_ln_b, w_s, b_s,
           w_branch_mlstm, w_branch_sgu, w_out, norm2_g, w_up, w_ffn_conv, w_down, final_g):
    assert x.shape[0] == 1 and w_mod.shape[0] == 1, "single batch element, single layer"
    d = D_MODEL
    x2, ctx2 = x[0], ctx[0]
    t = x2.shape[0]
    assert t % GRID_W == 0 and ctx2.shape[0] % _L == 0
    row = lambda a: a.reshape(1, -1)

    cc = jnp.zeros((d, 8), F32).at[:, 0].set(c[0]).at[:, 1].set(c_ctx)
    mod = _modulation(cc, w_mod[0], row(b_mod[0]))
    mx = [mod[0:1, j * d:(j + 1) * d] for j in range(N_MOD)]
    mc = [mod[1:2, j * d:(j + 1) * d] for j in range(N_MOD)]

    w_in_t = jnp.swapaxes(w_in[0], 0, 1)
    wqkv = _cast_columns(w_in_t, 0, _W_GATES)
    wg = _gate_weights(w_in_t)
    wrest = _cast_columns(w_in_t, _W_REST, w_in_t.shape[0] - _W_REST)
    bg = _gate_columns(row(b_gate[0]))
    eye2 = jnp.eye(2, dtype=F32)
    wsd = jnp.stack([jnp.kron(eye2, w_s[0, g]) for g in range(HEADS)]).astype(BF16)
    bs2 = jnp.tile(b_s[0].T, (2, 1))
    proj_w = (row(norm1_g[0]), wqkv, wg, wrest, bg, conv_qk[0],
              row(sgu_ln_g[0]), row(sgu_ln_b[0]), wsd, bs2, w_branch_sgu[0].astype(BF16))

    pc = _projection(ctx2, mc[0], mc[1], *proj_w, tm=_tile_size(ctx2.shape[0], 512))
    c0 = jnp.zeros((_N_STATES, HEAD_DIM, _C_EXT), F32)
    m0 = jnp.full((1, _GATE_LANES), M_INIT, F32)
    _, _, c1, m1 = _mlstm(*pc[:6], c0, m0)

    q, kt, v, tab, gr, gm, so, sgm, s = _projection(x2, mx[0], mx[1], *proj_w, tm=_tile_size(t, 512))
    hf, hb, _, _ = _mlstm(q, kt, v, tab, gr, gm, c1, m1)
    hx, hn2 = _post(hf, hb, so, sgm, s, x2, row(head_norm_g[0]), w_branch_mlstm[0].astype(BF16),
                    w_out[0].astype(BF16), mx[2], row(norm2_g[0]), mx[3], mx[4], tm=_tile_size(t, 512))

    out = _ffn(hn2, hx, _regroup_up(w_up[0]), w_ffn_conv[0].reshape(9, D_FF), w_down[0].astype(BF16), mx[5],
               row(final_g), tm=_tile_size(t, 1024))
    return out[None]
```

```python
import functools

import jax
import jax.numpy as jnp
from jax import lax
from jax.experimental import pallas as pl
from jax.experimental.pallas import tpu as pltpu

F32 = jnp.float32
BF16 = jnp.bfloat16

D_MODEL = 1024
GRID_W = 64
HEADS = 4
HEAD_DIM = 256
N_MOD = 6
D_FF = 2816
EPS = 1e-6
M_INIT = -1e30

_L = 256
_HALO = 16
_FF_CHUNK = 256
_N_FF_CHUNKS = D_FF // _FF_CHUNK
_GATE_LANES = 128
_PAD = 8
_PROJ_BLOCK = 256
_FFN_PIECES = 3
_FFN_DOWN_PIECES = 2
_VMEM_LIMIT_V7X = 56 * 1024 * 1024

_N_STATES = 2 * HEADS
_TAB_W = 3 * _GATE_LANES
_C_EXT = HEAD_DIM + _GATE_LANES
_N_GATES = 4 * HEADS
_W_GATES = 3 * D_MODEL
_W_REST = _W_GATES + _N_GATES
_SUBLANES = 8
_PREP_BLOCK = 512


def _dot(a, b):
    return jnp.dot(a, b, preferred_element_type=F32)


def _split3(x):
    h1 = x.astype(BF16)
    r1 = x - h1.astype(F32)
    h2 = r1.astype(BF16)
    h3 = (r1 - h2.astype(F32)).astype(BF16)
    return h1, h2, h3


def _sigmoid(x):
    return 0.5 * jnp.tanh(0.5 * x) + 0.5


def _silu(x):
    h = 0.5 * x
    return h * jnp.tanh(h) + h


def _gelu_tanh(x):
    return 0.5 * x * (1.0 + jnp.tanh(0.7978845608028654 * (x + 0.044715 * (x * x * x))))


def _log_sigmoid(x):
    return jnp.minimum(x, 0.0) - jnp.log(1.0 + jnp.exp(-jnp.abs(x)))


def _rms(x):
    return x * lax.rsqrt(jnp.mean(x * x, axis=-1, keepdims=True) + EPS)


def _const_spec(shape):
    nd = len(shape)
    return pl.BlockSpec(shape, lambda *_: (0,) * nd, pipeline_mode=pl.Buffered(1))


def _cast_kernel(w_ref, o_ref):
    o_ref[...] = w_ref[...].astype(BF16)


def _transpose_cast_kernel(w_ref, o_ref):
    o_ref[...] = w_ref[...].T.astype(BF16)


def _shift_transpose_cast_kernel(w_ref, next_ref, o_ref, *, shift):
    x = jnp.concatenate([w_ref[...], next_ref[...]], axis=0)
    o_ref[...] = x[shift:shift + w_ref.shape[0], :].T.astype(BF16)


def _cast_columns(w_t, col0, ncols):
    rows = w_t.shape[1]
    shift = col0 % _PREP_BLOCK
    base = col0 - shift
    assert shift % _SUBLANES == 0 and ncols % _PREP_BLOCK == 0 and col0 + ncols <= w_t.shape[0]
    b0 = base // _PREP_BLOCK
    in_specs = [pl.BlockSpec((_PREP_BLOCK, rows), lambda k: (b0 + k, 0))]
    operands = [w_t]
    body = _transpose_cast_kernel
    if shift:
        assert _PREP_BLOCK % shift == 0
        per_block = _PREP_BLOCK // shift
        in_specs.append(pl.BlockSpec((shift, rows), lambda k: ((b0 + k + 1) * per_block, 0)))
        operands.append(w_t)
        body = functools.partial(_shift_transpose_cast_kernel, shift=shift)
    return pl.pallas_call(
        body,
        grid=(ncols // _PREP_BLOCK,),
        in_specs=in_specs,
        out_specs=pl.BlockSpec((rows, _PREP_BLOCK), lambda k: (0, k)),
        out_shape=jax.ShapeDtypeStruct((rows, ncols), BF16),
        compiler_params=pltpu.CompilerParams(dimension_semantics=("parallel",)),
        name="cast_columns",
    )(*operands)


def _regroup_up(w_up):
    rows = w_up.shape[0]
    return pl.pallas_call(
        _cast_kernel,
        grid=(2 * _N_FF_CHUNKS,),
        in_specs=[pl.BlockSpec((rows, _FF_CHUNK), lambda j: (0, (j % 2) * _N_FF_CHUNKS + j // 2))],
        out_specs=pl.BlockSpec((rows, _FF_CHUNK), lambda j: (0, j)),
        out_shape=jax.ShapeDtypeStruct((rows, 2 * D_FF), BF16),
        compiler_params=pltpu.CompilerParams(dimension_semantics=("parallel",)),
        name="regroup_up",
    )(w_up)


def _gate_cast_kernel(w_ref, sel_ref, o_ref):
    o_ref[...] = lax.dot_general(w_ref[...].astype(BF16), sel_ref[...], (((0,), (0,)), ((), ())),
                                 preferred_element_type=F32).astype(BF16)


def _gate_selector():
    sel = [[0.0] * (2 * _GATE_LANES) for _ in range(_N_GATES)]
    for d in range(2):
        for h in range(HEADS):
            j = d * HEADS + h
            sel[2 * d * HEADS + h][j] = 1.0
            sel[(2 * d + 1) * HEADS + h][_GATE_LANES + j] = 1.0
    return jnp.array(sel, BF16)


def _gate_weights(w_t):
    rows = w_t.shape[1]
    assert _W_GATES % _N_GATES == 0
    return pl.pallas_call(
        _gate_cast_kernel,
        grid=(1,),
        in_specs=[pl.BlockSpec((_N_GATES, rows), lambda k: (_W_GATES // _N_GATES, 0)),
                  pl.BlockSpec((_N_GATES, 2 * _GATE_LANES), lambda k: (0, 0))],
        out_specs=pl.BlockSpec((rows, 2 * _GATE_LANES), lambda k: (0, 0)),
        out_shape=jax.ShapeDtypeStruct((rows, 2 * _GATE_LANES), BF16),
        name="gate_weights",
    )(w_t, _gate_selector())


def _mod_kernel(cc_ref, w_ref, b_ref, o_ref):
    s = _silu(cc_ref[...])
    w = w_ref[...]
    rows = [jnp.sum(w * s[:, r:r + 1], axis=0, keepdims=True) + b_ref[...] for r in range(2)]
    o_ref[...] = jnp.concatenate(rows + [jnp.zeros((6, w.shape[1]), F32)], axis=0)


def _modulation(cc, w_mod, b_mod):
    n_out = w_mod.shape[1]
    nb = D_MODEL
    return pl.pallas_call(
        _mod_kernel,
        grid=(n_out // nb,),
        in_specs=[pl.BlockSpec((D_MODEL, 8), lambda j: (0, 0)),
                  pl.BlockSpec((D_MODEL, nb), lambda j: (0, j)),
                  pl.BlockSpec((1, nb), lambda j: (0, j))],
        out_specs=pl.BlockSpec((8, nb), lambda j: (0, j)),
        out_shape=jax.ShapeDtypeStruct((8, n_out), F32),
        compiler_params=pltpu.CompilerParams(dimension_semantics=("arbitrary",)),
        name="modulation",
    )(cc, w_mod, b_mod)


def _gate_tables(gi, lf):
    row = lax.broadcasted_iota(jnp.int32, (_L, _L), 0)
    col = lax.broadcasted_iota(jnp.int32, (_L, _L), 1)
    ltri = (col <= row).astype(BF16)
    l1, l2, l3 = _split3(lf)
    prefix = _dot(ltri, l3) + _dot(ltri, l2) + _dot(ltri, l1)
    total = prefix[_L - 1:_L, :]
    suffix = total - prefix + lf
    lane = lax.broadcasted_iota(jnp.int32, (1, _GATE_LANES), 1)
    cum = jnp.where(lane >= HEADS, suffix, prefix)
    a = total - cum + gi
    c = gi - cum
    c_rows = c.T[0:_N_STATES, :]
    a_rows = a.T[0:_N_STATES, :]
    pos = lax.broadcasted_iota(jnp.int32, (_N_STATES, _L), 1)
    fwd_max, bwd_max = c_rows, c_rows
    k = 1
    while k < _L:
        fwd_max = jnp.maximum(fwd_max, jnp.where(pos >= k, pltpu.roll(fwd_max, k, 1), -jnp.inf))
        bwd_max = jnp.maximum(bwd_max, jnp.where(pos < _L - k, pltpu.roll(bwd_max, _L - k, 1), -jnp.inf))
        k *= 2
    state = lax.broadcasted_iota(jnp.int32, (_N_STATES, _L), 0)
    run_max = jnp.where(state < HEADS, fwd_max, bwd_max)
    run_max_cols = jnp.concatenate([run_max, jnp.zeros((_GATE_LANES - _N_STATES, _L), F32)], axis=0).T
    return cum, cum + run_max_cols, a, c_rows, a_rows, total, jnp.max(a, axis=0, keepdims=True)


def _proj_kernel(x_ref, xp_ref, xn_ref, shift_ref, scale_ref, g1_ref, wqkv_ref, wg_ref, wrest_ref, bg_ref,
                 cw_ref, lng_ref, lnb_ref, wsd_ref, bs_ref, wbs_ref,
                 q_ref, kt_ref, v_ref, tab_ref, gr_ref, gm_ref, so_ref, sgm_ref, s_ref,
                 hn_s, z_s, ys_s, *, tm):
    i = pl.program_id(0)
    has_prev = i > 0
    has_next = i < pl.num_programs(0) - 1
    scale1 = 1.0 + scale_ref[...]

    def norm_mod(xb):
        return ((_rms(xb) * g1_ref[...]) * scale1 + shift_ref[...]).astype(BF16)

    hn_s[0:_HALO, :] = norm_mod(xp_ref[...])
    hn_s[_HALO:_HALO + tm, :] = norm_mod(x_ref[...])
    hn_s[_HALO + tm:, :] = norm_mod(xn_ref[...])
    hn = hn_s[_HALO:_HALO + tm, :]

    bw = _PROJ_BLOCK
    u_blocks, vg_blocks = [], []

    def qkv_cols(seg, b):
        return wqkv_ref[:, seg * D_MODEL + b * bw:seg * D_MODEL + (b + 1) * bw]

    def rest_cols(seg, b):
        return wrest_ref[:, seg * D_MODEL + b * bw:seg * D_MODEL + (b + 1) * bw]

    def qk_block(c, b):
        lanes = slice(b * bw, (b + 1) * bw)
        z = _dot(hn_s[...], qkv_cols(c, b))
        z_s[c, 0:_HALO, lanes] = jnp.where(has_prev, z[0:_HALO, :], 0.0)
        z_s[c, _HALO:_HALO + tm, lanes] = z[_HALO:_HALO + tm, :]
        z_s[c, _HALO + tm:, lanes] = jnp.where(has_next, z[_HALO + tm:, :], 0.0)
        cw = cw_ref[:, c * D_MODEL + b * bw:c * D_MODEL + (b + 1) * bw]
        y = (cw[0:1, :] * z_s[c, _HALO - 1:_HALO - 1 + tm, lanes] + cw[1:2, :] * z_s[c, _HALO:_HALO + tm, lanes]
             + cw[2:3, :] * z_s[c, _HALO + 1:_HALO + 1 + tm, lanes])
        y = _silu(y)
        if c == 0:
            q_ref[:, lanes] = (y * (HEAD_DIM ** -0.5)).astype(BF16)
        else:
            kt_ref[lanes, :] = y.T.astype(BF16)

    for b in range(D_MODEL // bw):
        lanes = slice(b * bw, (b + 1) * bw)
        qk_block(0, b)
        v_ref[:, lanes] = _dot(hn, qkv_cols(2, b)).astype(BF16)
        qk_block(1, b)
        so_ref[:, lanes] = _sigmoid(_dot(hn, rest_cols(0, b))).astype(BF16)
        u_blocks.append(_gelu_tanh(_dot(hn, rest_cols(1, b))))
        sgm_ref[:, lanes] = _sigmoid(_dot(hn, rest_cols(3, b))).astype(BF16)
        vg_blocks.append(_gelu_tanh(_dot(hn, rest_cols(2, b))))

    gates = _dot(hn, wg_ref[...]) + bg_ref[...]
    gi = gates[:, 0:_GATE_LANES]
    lf = _log_sigmoid(gates[:, _GATE_LANES:])
    for j in range(tm // _L):
        rows = slice(j * _L, (j + 1) * _L)
        cum, mi, a, c_rows, a_rows, total, a_max = _gate_tables(gi[rows, :], lf[rows, :])
        tab_ref[rows, 0:_GATE_LANES] = cum
        tab_ref[rows, _GATE_LANES:2 * _GATE_LANES] = mi
        tab_ref[rows, 2 * _GATE_LANES:] = a
        gr_ref[0:_N_STATES, rows] = c_rows
        gr_ref[_N_STATES:, rows] = a_rows
        gm_ref[j * 8:(j + 1) * 8, :] = jnp.concatenate([total, a_max, jnp.zeros((6, _GATE_LANES), F32)], axis=0)

    vg = jnp.concatenate(vg_blocks, axis=1)
    u = jnp.concatenate(u_blocks, axis=1)
    vg = vg - jnp.mean(vg, axis=-1, keepdims=True)
    vn = (vg * lax.rsqrt(jnp.mean(vg * vg, axis=-1, keepdims=True) + EPS) * lng_ref[...] + lnb_ref[...]).astype(BF16)
    for p in range(tm // _L):
        for g in range(HEADS):
            rows = slice(p * _L, (p + 1) * _L)
            cols = slice(g * HEAD_DIM, (g + 1) * HEAD_DIM)
            mixed = _dot(wsd_ref[g], vn[rows, cols]) + bs_ref[:, g:g + 1]
            ys_s[rows, cols] = (u[rows, cols] * mixed).astype(BF16)
    gg = _sigmoid(_dot(hn, wrest_ref[:, 4 * D_MODEL:5 * D_MODEL]))
    s_ref[...] = (gg * _dot(ys_s[...], wbs_ref[...])).astype(BF16)


def _projection(x, shift, scale, g1, wqkv, wg, wrest, bg, cw, lng, lnb, wsd, bs2, wbs, *, tm):
    t = x.shape[0]
    nt = t // tm
    hb = tm // _HALO
    nhb = t // _HALO
    tile = lambda w: pl.BlockSpec((tm, w), lambda i: (i, 0))
    in_specs = [
        tile(D_MODEL),
        pl.BlockSpec((_HALO, D_MODEL), lambda i: (jnp.maximum(i * hb - 1, 0), 0)),
        pl.BlockSpec((_HALO, D_MODEL), lambda i: (jnp.minimum((i + 1) * hb, nhb - 1), 0)),
    ] + [_const_spec(a.shape) for a in (shift, scale, g1, wqkv, wg, wrest, bg, cw, lng, lnb, wsd, bs2, wbs)]
    gm_rows = 8 * (tm // _L)
    out_specs = [tile(D_MODEL), pl.BlockSpec((D_MODEL, tm), lambda i: (0, i)), tile(D_MODEL), tile(_TAB_W),
                 pl.BlockSpec((2 * _N_STATES, tm), lambda i: (0, i)),
                 pl.BlockSpec((gm_rows, _GATE_LANES), lambda i: (i, 0)),
                 tile(D_MODEL), tile(D_MODEL), tile(D_MODEL)]
    act = jax.ShapeDtypeStruct((t, D_MODEL), BF16)
    out_shape = [act, jax.ShapeDtypeStruct((D_MODEL, t), BF16), act, jax.ShapeDtypeStruct((t, _TAB_W), F32),
                 jax.ShapeDtypeStruct((2 * _N_STATES, t), F32), jax.ShapeDtypeStruct((8 * (t // _L), _GATE_LANES), F32),
                 act, act, act]
    return pl.pallas_call(
        functools.partial(_proj_kernel, tm=tm),
        grid=(nt,),
        in_specs=in_specs,
        out_specs=out_specs,
        out_shape=out_shape,
        scratch_shapes=[pltpu.VMEM((tm + 2 * _HALO, D_MODEL), BF16),
                        pltpu.VMEM((2, tm + 2 * _HALO, D_MODEL), F32),
                        pltpu.VMEM((tm, D_MODEL), BF16)],
        compiler_params=pltpu.CompilerParams(dimension_semantics=("parallel",), vmem_limit_bytes=_VMEM_LIMIT_V7X),
        name="projection",
    )(x, x, x, shift, scale, g1, wqkv, wg, wrest, bg, cw, lng, lnb, wsd, bs2, wbs)


def _mlstm_kernel(qf_ref, ktf_ref, vf_ref, tabf_ref, grf_ref, gmf_ref, qb_ref, ktb_ref, vb_ref, tabb_ref, grb_ref, gmb_ref,
                  c0_ref, m0_ref, hf_ref, hb_ref, c_ref, m_ref):
    @pl.when(pl.program_id(0) == 0)
    def _():
        c_ref[...] = c0_ref[...]
        m_ref[...] = m0_ref[...]

    row = lax.broadcasted_iota(jnp.int32, (_L, _L), 0)
    col = lax.broadcasted_iota(jnp.int32, (_L, _L), 1)
    ones = jnp.ones((_L, _GATE_LANES), BF16)
    m_prev = m_ref[...]
    m_next = []
    dirs = ((qf_ref, ktf_ref, vf_ref, tabf_ref, grf_ref, gmf_ref, hf_ref, col <= row),
            (qb_ref, ktb_ref, vb_ref, tabb_ref, grb_ref, gmb_ref, hb_ref, col >= row))
    for d, (q_ref, kt_ref, v_ref, tab_ref, gr_ref, gm_ref, h_ref, mask) in enumerate(dirs):
        cum = tab_ref[:, 0:_GATE_LANES]
        m_row = jnp.maximum(cum + m_prev, tab_ref[:, _GATE_LANES:2 * _GATE_LANES])
        e = cum - m_row
        total, a_max = gm_ref[0:1, :], gm_ref[1:2, :]
        m_new = jnp.maximum(total + m_prev, a_max)
        a_old = jnp.exp(total + m_prev - m_new)
        m_next.append(m_new)
        gr = gr_ref[...]
        for hd in range(HEADS):
            j = d * HEADS + hd
            cols = slice(hd * HEAD_DIM, (hd + 1) * HEAD_DIM)
            qh, kth = q_ref[:, cols], kt_ref[cols, :]
            v_ext = jnp.concatenate([v_ref[:, cols], ones], axis=1)
            e_b = jnp.broadcast_to(e[:, j:j + 1], (_L, _GATE_LANES))
            m_row_b = jnp.broadcast_to(m_row[:, j:j + 1], (_L, _GATE_LANES))
            c_row, a_row = gr[j:j + 1, :], gr[_N_STATES + j:_N_STATES + j + 1, :]
            p = jnp.exp(jnp.where(mask, jnp.concatenate([e_b, e_b], axis=1) + c_row, -jnp.inf))
            s = (_dot(qh, kth) * p).astype(BF16)
            c_prev = c_ref[j]
            w_inter = jnp.exp(e_b + m_prev[:, j:j + 1])
            num = _dot(s, v_ext) + jnp.concatenate([w_inter] * 3, axis=1) * _dot(qh, c_prev.astype(BF16))
            inv = 1.0 / jnp.maximum(jnp.abs(num[:, HEAD_DIM:]), jnp.exp(-m_row_b))
            h_ref[:, cols] = (num[:, :HEAD_DIM] * jnp.concatenate([inv, inv], axis=1)).astype(BF16)
            kw_t = (kth.astype(F32) * jnp.exp(a_row - m_new[:, j:j + 1])).astype(BF16)
            c_ref[j] = a_old[:, j:j + 1] * c_prev + _dot(kw_t, v_ext)
    lane = lax.broadcasted_iota(jnp.int32, (1, _GATE_LANES), 1)
    m_ref[...] = jnp.where(lane < HEADS, m_next[0], m_next[1])


def _mlstm(q, kt, v, tab, gr, gm, c0, m0):
    t = q.shape[0]
    nc = t // _L
    fwd_i = lambda i: i
    bwd_i = lambda i: nc - 1 - i

    def specs(ix):
        rows = lambda w: pl.BlockSpec((_L, w), lambda i: (ix(i), 0))
        return [rows(D_MODEL), pl.BlockSpec((D_MODEL, _L), lambda i: (0, ix(i))), rows(D_MODEL), rows(_TAB_W),
                pl.BlockSpec((2 * _N_STATES, _L), lambda i: (0, ix(i))),
                pl.BlockSpec((8, _GATE_LANES), lambda i: (ix(i), 0))]

    out_rows = lambda ix: pl.BlockSpec((_L, D_MODEL), lambda i: (ix(i), 0))
    state_specs = [pl.BlockSpec(a.shape, lambda i, nd=a.ndim: (0,) * nd) for a in (c0, m0)]
    act = jax.ShapeDtypeStruct((t, D_MODEL), BF16)
    return pl.pallas_call(
        _mlstm_kernel,
        grid=(nc,),
        in_specs=specs(fwd_i) + specs(bwd_i) + [_const_spec(c0.shape), _const_spec(m0.shape)],
        out_specs=[out_rows(fwd_i), out_rows(bwd_i)] + state_specs,
        out_shape=[act, act] + [jax.ShapeDtypeStruct(a.shape, F32) for a in (c0, m0)],
        compiler_params=pltpu.CompilerParams(dimension_semantics=("arbitrary",), vmem_limit_bytes=_VMEM_LIMIT_V7X),
        name="mlstm",
    )(q, kt, v, tab, gr, gm, q, kt, v, tab, gr, gm, c0, m0)


def _post_kernel(hf_ref, hb_ref, so_ref, sgm_ref, s_ref, x_ref, hg_ref, wbm_ref, wout_ref, gate_ref,
                 g2_ref, shift_ref, scale_ref, hx_ref, hn2_ref, ym_s):
    hm = hf_ref[...].astype(F32) + hb_ref[...].astype(F32)
    for hd in range(HEADS):
        cols = slice(hd * HEAD_DIM, (hd + 1) * HEAD_DIM)
        ym = _rms(hm[:, cols]) * hg_ref[:, cols]
        ym_s[:, cols] = (so_ref[:, cols].astype(F32) * ym).astype(BF16)
    y = sgm_ref[...].astype(F32) * _dot(ym_s[...], wbm_ref[...]) + s_ref[...].astype(F32)
    hx = x_ref[...] + gate_ref[...] * _dot(y.astype(BF16), wout_ref[...])
    hx_ref[...] = hx
    hn2_ref[...] = ((_rms(hx) * g2_ref[...]) * (1.0 + scale_ref[...]) + shift_ref[...]).astype(BF16)


def _post(hf, hb, so, sgm, s, x, hg, wbm, wout, gate, g2, shift, scale, *, tm):
    t = x.shape[0]
    tile = lambda: pl.BlockSpec((tm, D_MODEL), lambda i: (i, 0))
    return pl.pallas_call(
        _post_kernel,
        grid=(t // tm,),
        in_specs=[tile() for _ in range(6)] + [_const_spec(a.shape) for a in (hg, wbm, wout, gate, g2, shift, scale)],
        out_specs=[tile(), tile()],
        out_shape=[jax.ShapeDtypeStruct((t, D_MODEL), F32), jax.ShapeDtypeStruct((t, D_MODEL), BF16)],
        scratch_shapes=[pltpu.VMEM((tm, D_MODEL), BF16)],
        compiler_params=pltpu.CompilerParams(dimension_semantics=("parallel",), vmem_limit_bytes=_VMEM_LIMIT_V7X),
        name="post_mixer",
    )(hf, hb, so, sgm, s, x, hg, wbm, wout, gate, g2, shift, scale)


def _ffn_kernel(hn_ref, hnp_ref, hnn_ref, hx_ref, wab_ref, cw_ref, wdn_ref, gate_ref, fg_ref, o_ref,
                hne_s, a_s, b_s, act_s, acc_s, *, tm):
    i = pl.program_id(0)
    has_above = i > 0
    has_below = i < pl.num_programs(0) - 1
    ext = tm + 2 * GRID_W
    hne_s[0:GRID_W, :] = hnp_ref[...]
    hne_s[GRID_W:GRID_W + tm, :] = hn_ref[...]
    hne_s[GRID_W + tm:, :] = hnn_ref[...]
    for slot in range(2):
        a_s[slot, 0:_PAD, :] = jnp.zeros((_PAD, _FF_CHUNK), F32)
        a_s[slot, _PAD + ext:, :] = jnp.zeros((_PAD, _FF_CHUNK), F32)
    up_rows = ext // _FFN_PIECES
    mix_edges = tuple(min(j * up_rows, tm) for j in range(_FFN_PIECES + 1))
    down_edges = tuple(range(0, tm + 1, tm // _FFN_DOWN_PIECES))

    def up_piece(c, j):
        slot = c % 2
        m0, m1 = j * up_rows, (j + 1) * up_rows
        ab = _dot(hne_s[m0:m1, :], wab_ref[:, 2 * c * _FF_CHUNK:2 * (c + 1) * _FF_CHUNK])
        a = ab[:, :_FF_CHUNK]
        lo, hi = max(m0, GRID_W), min(m1, GRID_W + tm)
        if m0 < GRID_W:
            a_s[slot, _PAD + m0:_PAD + GRID_W, :] = jnp.where(has_above, a[:GRID_W - m0, :], 0.0)
        a_s[slot, _PAD + lo:_PAD + hi, :] = a[lo - m0:hi - m0, :]
        if m1 > GRID_W + tm:
            a_s[slot, _PAD + GRID_W + tm:_PAD + m1, :] = jnp.where(has_below, a[GRID_W + tm - m0:, :], 0.0)
        b_s[slot, lo - GRID_W:hi - GRID_W, :] = ab[lo - m0:hi - m0, _FF_CHUNK:]

    def mix_piece(c, p):
        slot = c % 2
        r0, r1 = mix_edges[p], mix_edges[p + 1]
        n = r1 - r0 + 2 * GRID_W
        gcol = lax.broadcasted_iota(jnp.int32, (n, 1), 0) % GRID_W
        taps = (jnp.where(gcol >= 1, a_s[slot, _PAD - 1 + r0:_PAD - 1 + r0 + n, :], 0.0),
                a_s[slot, _PAD + r0:_PAD + r0 + n, :],
                jnp.where(gcol <= GRID_W - 2, a_s[slot, _PAD + 1 + r0:_PAD + 1 + r0 + n, :], 0.0))
        cw = cw_ref[:, c * _FF_CHUNK:(c + 1) * _FF_CHUNK]
        conv = None
        for dr in range(3):
            for dc in range(3):
                term = cw[3 * dr + dc:3 * dr + dc + 1, :] * taps[dc][dr * GRID_W:dr * GRID_W + r1 - r0, :]
                conv = term if conv is None else conv + term
        act_s[slot, r0:r1, :] = (_silu(conv) * b_s[slot, r0:r1, :]).astype(BF16)

    def down_piece(c, p):
        r0, r1 = down_edges[p], down_edges[p + 1]
        part = _dot(act_s[c % 2, r0:r1, :], wdn_ref[c * _FF_CHUNK:(c + 1) * _FF_CHUNK, :])
        if c == 0:
            acc_s[r0:r1, :] = part
        else:
            acc_s[r0:r1, :] += part

    for step in range(_N_FF_CHUNKS + 2):
        for j in range(_FFN_PIECES):
            if step < _N_FF_CHUNKS:
                up_piece(step, j)
            if 0 <= step - 1 < _N_FF_CHUNKS:
                mix_piece(step - 1, j)
        if 0 <= step - 2:
            for p in range(_FFN_DOWN_PIECES):
                down_piece(step - 2, p)
    h2 = hx_ref[...] + gate_ref[...] * acc_s[...]
    o_ref[...] = _rms(h2) * fg_ref[...]


def _ffn(hn2, hx, wab, cwf, wdn, gate, fg, *, tm):
    t = hx.shape[0]
    rb = tm // GRID_W
    nrb = t // GRID_W
    tile = lambda: pl.BlockSpec((tm, D_MODEL), lambda i: (i, 0))
    ext = tm + 2 * GRID_W
    return pl.pallas_call(
        functools.partial(_ffn_kernel, tm=tm),
        grid=(t // tm,),
        in_specs=[tile(),
                  pl.BlockSpec((GRID_W, D_MODEL), lambda i: (jnp.maximum(i * rb - 1, 0), 0)),
                  pl.BlockSpec((GRID_W, D_MODEL), lambda i: (jnp.minimum((i + 1) * rb, nrb - 1), 0)),
                  tile()] + [_const_spec(a.shape) for a in (wab, cwf, wdn, gate, fg)],
        out_specs=tile(),
        out_shape=jax.ShapeDtypeStruct((t, D_MODEL), F32),
        scratch_shapes=[pltpu.VMEM((ext, D_MODEL), BF16),
                        pltpu.VMEM((2, ext + 2 * _PAD, _FF_CHUNK), F32),
                        pltpu.VMEM((2, tm, _FF_CHUNK), F32),
                        pltpu.VMEM((2, tm, _FF_CHUNK), BF16),
                        pltpu.VMEM((tm, D_MODEL), F32)],
        compiler_params=pltpu.CompilerParams(dimension_semantics=("parallel",), vmem_limit_bytes=_VMEM_LIMIT_V7X),
        name="conv_ffn",
    )(hn2, hn2, hn2, hx, wab, cwf, wdn, gate, fg)


def _gate_columns(wg):
    idx_i = [2 * d * HEADS + h for d in range(2) for h in range(HEADS)]
    idx_f = [(2 * d + 1) * HEADS + h for d in range(2) for h in range(HEADS)]
    pad = [(0, 0)] * (wg.ndim - 1) + [(0, _GATE_LANES - _N_STATES)]
    return jnp.concatenate([jnp.pad(wg[..., jnp.array(idx)], pad) for idx in (idx_i, idx_f)], axis=-1)


def _tile_size(t, want):
    tm = min(want, t)
    assert t % tm == 0 and tm % _L == 0, (t, tm)
    return tm


def kernel(x, c, ctx, c_ctx, w_mod, b_mod, norm1_g, w_in, b_gate, conv_qk, head_norm_g, sgu_ln_g, sgu_ln_b, w_s, b_s,
           w_branch_mlstm, w_branch_sgu, w_out, norm2_g, w_up, w_ffn_conv, w_down, final_g):
    assert x.shape[0] == 1 and w_mod.shape[0] == 1, "single batch element, single layer"
    d = D_MODEL
    x2, ctx2 = x[0], ctx[0]
    t = x2.shape[0]
    assert t % GRID_W == 0 and ctx2.shape[0] % _L == 0
    row = lambda a: a.reshape(1, -1)

    cc = jnp.zeros((d, 8), F32).at[:, 0].set(c[0]).at[:, 1].set(c_ctx)
    mod = _modulation(cc, w_mod[0], row(b_mod[0]))
    mx = [mod[0:1, j * d:(j + 1) * d] for j in range(N_MOD)]
    mc = [mod[1:2, j * d:(j + 1) * d] for j in range(N_MOD)]

    w_in_t = jnp.swapaxes(w_in[0], 0, 1)
    wqkv = _cast_columns(w_in_t, 0, _W_GATES)
    wg = _gate_weights(w_in_t)
    wrest = _cast_columns(w_in_t, _W_REST, w_in_t.shape[0] - _W_REST)
    bg = _gate_columns(row(b_gate[0]))
    eye2 = jnp.eye(2, dtype=F32)
    wsd = jnp.stack([jnp.kron(eye2, w_s[0, g]) for g in range(HEADS)]).astype(BF16)
    bs2 = jnp.tile(b_s[0].T, (2, 1))
    proj_w = (row(norm1_g[0]), wqkv, wg, wrest, bg, conv_qk[0],
              row(sgu_ln_g[0]), row(sgu_ln_b[0]), wsd, bs2, w_branch_sgu[0].astype(BF16))

    pc = _projection(ctx2, mc[0], mc[1], *proj_w, tm=_tile_size(ctx2.shape[0], 512))
    c0 = jnp.zeros((_N_STATES, HEAD_DIM, _C_EXT), F32)
    m0 = jnp.full((1, _GATE_LANES), M_INIT, F32)
    _, _, c1, m1 = _mlstm(*pc[:6], c0, m0)

    q, kt, v, tab, gr, gm, so, sgm, s = _projection(x2, mx[0], mx[1], *proj_w, tm=_tile_size(t, 512))
    hf, hb, _, _ = _mlstm(q, kt, v, tab, gr, gm, c1, m1)
    hx, hn2 = _post(hf, hb, so, sgm, s, x2, row(head_norm_g[0]), w_branch_mlstm[0].astype(BF16),
                    w_out[0].astype(BF16), mx[2], row(norm2_g[0]), mx[3], mx[4], tm=_tile_size(t, 512))

    out = _ffn(hn2, hx, _regroup_up(w_up[0]), w_ffn_conv[0].reshape(9, D_FF), w_down[0].astype(BF16), mx[5],
               row(final_g), tm=_tile_size(t, 1024))
    return out[None]
```

```python
import functools

import jax
import jax.numpy as jnp
from jax import lax
from jax.experimental import pallas as pl
from jax.experimental.pallas import tpu as pltpu

F32 = jnp.float32
BF16 = jnp.bfloat16

D_MODEL = 1024
GRID_W = 64
HEADS = 4
HEAD_DIM = 256
N_MOD = 6
D_FF = 2816
EPS = 1e-6
M_INIT = -1e30

_L = 256
_HALO = 16
_FF_CHUNK = 256
_N_FF_CHUNKS = D_FF // _FF_CHUNK
_GATE_LANES = 128
_PAD = 8
_PROJ_BLOCK = 256
_FFN_PIECES = 3
_FFN_DOWN_PIECES = 2
_POST_PIECES = 2
_VMEM_LIMIT_V7X = 56 * 1024 * 1024

_N_STATES = 2 * HEADS
_TAB_W = 3 * _GATE_LANES
_C_EXT = HEAD_DIM + _GATE_LANES
_N_GATES = 4 * HEADS
_W_GATES = 3 * D_MODEL
_W_REST = _W_GATES + _N_GATES
_SUBLANES = 8
_PREP_BLOCK = 512


def _dot(a, b):
    return jnp.dot(a, b, preferred_element_type=F32)


def _split3(x):
    h1 = x.astype(BF16)
    r1 = x - h1.astype(F32)
    h2 = r1.astype(BF16)
    h3 = (r1 - h2.astype(F32)).astype(BF16)
    return h1, h2, h3


def _sigmoid(x):
    return 0.5 * jnp.tanh(0.5 * x) + 0.5


def _silu(x):
    h = 0.5 * x
    return h * jnp.tanh(h) + h


def _gelu_tanh(x):
    return 0.5 * x * (1.0 + jnp.tanh(0.7978845608028654 * (x + 0.044715 * (x * x * x))))


def _log_sigmoid(x):
    return jnp.minimum(x, 0.0) - jnp.log(1.0 + jnp.exp(-jnp.abs(x)))


def _rms(x):
    return x * lax.rsqrt(jnp.mean(x * x, axis=-1, keepdims=True) + EPS)


def _const_spec(shape):
    nd = len(shape)
    return pl.BlockSpec(shape, lambda *_: (0,) * nd, pipeline_mode=pl.Buffered(1))


def _cast_kernel(w_ref, o_ref):
    o_ref[...] = w_ref[...].astype(BF16)


def _transpose_cast_kernel(w_ref, o_ref):
    o_ref[...] = w_ref[...].T.astype(BF16)


def _shift_transpose_cast_kernel(w_ref, next_ref, o_ref, *, shift):
    x = jnp.concatenate([w_ref[...], next_ref[...]], axis=0)
    o_ref[...] = x[shift:shift + w_ref.shape[0], :].T.astype(BF16)


def _cast_columns(w_t, col0, ncols):
    rows = w_t.shape[1]
    shift = col0 % _PREP_BLOCK
    base = col0 - shift
    assert shift % _SUBLANES == 0 and ncols % _PREP_BLOCK == 0 and col0 + ncols <= w_t.shape[0]
    b0 = base // _PREP_BLOCK
    in_specs = [pl.BlockSpec((_PREP_BLOCK, rows), lambda k: (b0 + k, 0))]
    operands = [w_t]
    body = _transpose_cast_kernel
    if shift:
        assert _PREP_BLOCK % shift == 0
        per_block = _PREP_BLOCK // shift
        in_specs.append(pl.BlockSpec((shift, rows), lambda k: ((b0 + k + 1) * per_block, 0)))
        operands.append(w_t)
        body = functools.partial(_shift_transpose_cast_kernel, shift=shift)
    return pl.pallas_call(
        body,
        grid=(ncols // _PREP_BLOCK,),
        in_specs=in_specs,
        out_specs=pl.BlockSpec((rows, _PREP_BLOCK), lambda k: (0, k)),
        out_shape=jax.ShapeDtypeStruct((rows, ncols), BF16),
        compiler_params=pltpu.CompilerParams(dimension_semantics=("parallel",)),
        name="cast_columns",
    )(*operands)


def _regroup_up(w_up):
    rows = w_up.shape[0]
    return pl.pallas_call(
        _cast_kernel,
        grid=(2 * _N_FF_CHUNKS,),
        in_specs=[pl.BlockSpec((rows, _FF_CHUNK), lambda j: (0, (j % 2) * _N_FF_CHUNKS + j // 2))],
        out_specs=pl.BlockSpec((rows, _FF_CHUNK), lambda j: (0, j)),
        out_shape=jax.ShapeDtypeStruct((rows, 2 * D_FF), BF16),
        compiler_params=pltpu.CompilerParams(dimension_semantics=("parallel",)),
        name="regroup_up",
    )(w_up)


def _gate_cast_kernel(w_ref, sel_ref, o_ref):
    o_ref[...] = lax.dot_general(w_ref[...].astype(BF16), sel_ref[...], (((0,), (0,)), ((), ())),
                                 preferred_element_type=F32).astype(BF16)


def _gate_selector():
    sel = [[0.0] * (2 * _GATE_LANES) for _ in range(_N_GATES)]
    for d in range(2):
        for h in range(HEADS):
            j = d * HEADS + h
            sel[2 * d * HEADS + h][j] = 1.0
            sel[(2 * d + 1) * HEADS + h][_GATE_LANES + j] = 1.0
    return jnp.array(sel, BF16)


def _gate_weights(w_t):
    rows = w_t.shape[1]
    assert _W_GATES % _N_GATES == 0
    return pl.pallas_call(
        _gate_cast_kernel,
        grid=(1,),
        in_specs=[pl.BlockSpec((_N_GATES, rows), lambda k: (_W_GATES // _N_GATES, 0)),
                  pl.BlockSpec((_N_GATES, 2 * _GATE_LANES), lambda k: (0, 0))],
        out_specs=pl.BlockSpec((rows, 2 * _GATE_LANES), lambda k: (0, 0)),
        out_shape=jax.ShapeDtypeStruct((rows, 2 * _GATE_LANES), BF16),
        name="gate_weights",
    )(w_t, _gate_selector())


def _mod_kernel(cc_ref, w_ref, b_ref, o_ref):
    s = _silu(cc_ref[...])
    w = w_ref[...]
    rows = [jnp.sum(w * s[:, r:r + 1], axis=0, keepdims=True) + b_ref[...] for r in range(2)]
    o_ref[...] = jnp.concatenate(rows + [jnp.zeros((6, w.shape[1]), F32)], axis=0)


def _modulation(cc, w_mod, b_mod):
    n_out = w_mod.shape[1]
    nb = D_MODEL
    return pl.pallas_call(
        _mod_kernel,
        grid=(n_out // nb,),
        in_specs=[pl.BlockSpec((D_MODEL, 8), lambda j: (0, 0)),
                  pl.BlockSpec((D_MODEL, nb), lambda j: (0, j)),
                  pl.BlockSpec((1, nb), lambda j: (0, j))],
        out_specs=pl.BlockSpec((8, nb), lambda j: (0, j)),
        out_shape=jax.ShapeDtypeStruct((8, n_out), F32),
        compiler_params=pltpu.CompilerParams(dimension_semantics=("arbitrary",)),
        name="modulation",
    )(cc, w_mod, b_mod)


def _gate_tables(gi, lf):
    row = lax.broadcasted_iota(jnp.int32, (_L, _L), 0)
    col = lax.broadcasted_iota(jnp.int32, (_L, _L), 1)
    ltri = (col <= row).astype(BF16)
    l1, l2, l3 = _split3(lf)
    prefix = _dot(ltri, l3) + _dot(ltri, l2) + _dot(ltri, l1)
    total = prefix[_L - 1:_L, :]
    suffix = total - prefix + lf
    lane = lax.broadcasted_iota(jnp.int32, (1, _GATE_LANES), 1)
    cum = jnp.where(lane >= HEADS, suffix, prefix)
    a = total - cum + gi
    c = gi - cum
    c_rows = c.T[0:_N_STATES, :]
    a_rows = a.T[0:_N_STATES, :]
    pos = lax.broadcasted_iota(jnp.int32, (_N_STATES, _L), 1)
    fwd_max, bwd_max = c_rows, c_rows
    k = 1
    while k < _L:
        fwd_max = jnp.maximum(fwd_max, jnp.where(pos >= k, pltpu.roll(fwd_max, k, 1), -jnp.inf))
        bwd_max = jnp.maximum(bwd_max, jnp.where(pos < _L - k, pltpu.roll(bwd_max, _L - k, 1), -jnp.inf))
        k *= 2
    state = lax.broadcasted_iota(jnp.int32, (_N_STATES, _L), 0)
    run_max = jnp.where(state < HEADS, fwd_max, bwd_max)
    run_max_cols = jnp.concatenate([run_max, jnp.zeros((_GATE_LANES - _N_STATES, _L), F32)], axis=0).T
    return cum, cum + run_max_cols, a, c_rows, a_rows, total, jnp.max(a, axis=0, keepdims=True)


def _proj_kernel(x_ref, xp_ref, xn_ref, shift_ref, scale_ref, g1_ref, wqkv_ref, wg_ref, wrest_ref, bg_ref,
                 cw_ref, lng_ref, lnb_ref, wsd_ref, bs_ref, wbs_ref,
                 q_ref, kt_ref, v_ref, tab_ref, gr_ref, gm_ref, so_ref, sgm_ref, s_ref,
                 hn_s, z_s, ys_s, *, tm):
    i = pl.program_id(0)
    has_prev = i > 0
    has_next = i < pl.num_programs(0) - 1
    scale1 = 1.0 + scale_ref[...]

    def norm_mod(xb):
        return ((_rms(xb) * g1_ref[...]) * scale1 + shift_ref[...]).astype(BF16)

    hn_s[0:_HALO, :] = norm_mod(xp_ref[...])
    hn_s[_HALO:_HALO + tm, :] = norm_mod(x_ref[...])
    hn_s[_HALO + tm:, :] = norm_mod(xn_ref[...])
    hn = hn_s[_HALO:_HALO + tm, :]

    bw = _PROJ_BLOCK
    u_blocks, vg_blocks = [], []

    def qkv_cols(seg, b):
        return wqkv_ref[:, seg * D_MODEL + b * bw:seg * D_MODEL + (b + 1) * bw]

    def rest_cols(seg, b):
        return wrest_ref[:, seg * D_MODEL + b * bw:seg * D_MODEL + (b + 1) * bw]

    def qk_block(c, b):
        lanes = slice(b * bw, (b + 1) * bw)
        z = _dot(hn_s[...], qkv_cols(c, b))
        z_s[c, 0:_HALO, lanes] = jnp.where(has_prev, z[0:_HALO, :], 0.0)
        z_s[c, _HALO:_HALO + tm, lanes] = z[_HALO:_HALO + tm, :]
        z_s[c, _HALO + tm:, lanes] = jnp.where(has_next, z[_HALO + tm:, :], 0.0)
        cw = cw_ref[:, c * D_MODEL + b * bw:c * D_MODEL + (b + 1) * bw]
        y = (cw[0:1, :] * z_s[c, _HALO - 1:_HALO - 1 + tm, lanes] + cw[1:2, :] * z_s[c, _HALO:_HALO + tm, lanes]
             + cw[2:3, :] * z_s[c, _HALO + 1:_HALO + 1 + tm, lanes])
        y = _silu(y)
        if c == 0:
            q_ref[:, lanes] = (y * (HEAD_DIM ** -0.5)).astype(BF16)
        else:
            kt_ref[lanes, :] = y.T.astype(BF16)

    for b in range(D_MODEL // bw):
        lanes = slice(b * bw, (b + 1) * bw)
        qk_block(0, b)
        v_ref[:, lanes] = _dot(hn, qkv_cols(2, b)).astype(BF16)
        qk_block(1, b)
        so_ref[:, lanes] = _sigmoid(_dot(hn, rest_cols(0, b))).astype(BF16)
        u_blocks.append(_gelu_tanh(_dot(hn, rest_cols(1, b))))
        sgm_ref[:, lanes] = _sigmoid(_dot(hn, rest_cols(3, b))).astype(BF16)
        vg_blocks.append(_gelu_tanh(_dot(hn, rest_cols(2, b))))

    gates = _dot(hn, wg_ref[...]) + bg_ref[...]
    gi = gates[:, 0:_GATE_LANES]
    lf = _log_sigmoid(gates[:, _GATE_LANES:])
    for j in range(tm // _L):
        rows = slice(j * _L, (j + 1) * _L)
        cum, mi, a, c_rows, a_rows, total, a_max = _gate_tables(gi[rows, :], lf[rows, :])
        tab_ref[rows, 0:_GATE_LANES] = cum
        tab_ref[rows, _GATE_LANES:2 * _GATE_LANES] = mi
        tab_ref[rows, 2 * _GATE_LANES:] = a
        gr_ref[0:_N_STATES, rows] = c_rows
        gr_ref[_N_STATES:, rows] = a_rows
        gm_ref[j * 8:(j + 1) * 8, :] = jnp.concatenate([total, a_max, jnp.zeros((6, _GATE_LANES), F32)], axis=0)

    vg = jnp.concatenate(vg_blocks, axis=1)
    u = jnp.concatenate(u_blocks, axis=1)
    vg = vg - jnp.mean(vg, axis=-1, keepdims=True)
    vn = (vg * lax.rsqrt(jnp.mean(vg * vg, axis=-1, keepdims=True) + EPS) * lng_ref[...] + lnb_ref[...]).astype(BF16)
    for p in range(tm // _L):
        for g in range(HEADS):
            rows = slice(p * _L, (p + 1) * _L)
            cols = slice(g * HEAD_DIM, (g + 1) * HEAD_DIM)
            mixed = _dot(wsd_ref[g], vn[rows, cols]) + bs_ref[:, g:g + 1]
            ys_s[rows, cols] = (u[rows, cols] * mixed).astype(BF16)
    gg = _sigmoid(_dot(hn, wrest_ref[:, 4 * D_MODEL:5 * D_MODEL]))
    s_ref[...] = (gg * _dot(ys_s[...], wbs_ref[...])).astype(BF16)


def _projection(x, shift, scale, g1, wqkv, wg, wrest, bg, cw, lng, lnb, wsd, bs2, wbs, *, tm):
    t = x.shape[0]
    nt = t // tm
    hb = tm // _HALO
    nhb = t // _HALO
    tile = lambda w: pl.BlockSpec((tm, w), lambda i: (i, 0))
    in_specs = [
        tile(D_MODEL),
        pl.BlockSpec((_HALO, D_MODEL), lambda i: (jnp.maximum(i * hb - 1, 0), 0)),
        pl.BlockSpec((_HALO, D_MODEL), lambda i: (jnp.minimum((i + 1) * hb, nhb - 1), 0)),
    ] + [_const_spec(a.shape) for a in (shift, scale, g1, wqkv, wg, wrest, bg, cw, lng, lnb, wsd, bs2, wbs)]
    gm_rows = 8 * (tm // _L)
    out_specs = [tile(D_MODEL), pl.BlockSpec((D_MODEL, tm), lambda i: (0, i)), tile(D_MODEL), tile(_TAB_W),
                 pl.BlockSpec((2 * _N_STATES, tm), lambda i: (0, i)),
                 pl.BlockSpec((gm_rows, _GATE_LANES), lambda i: (i, 0)),
                 tile(D_MODEL), tile(D_MODEL), tile(D_MODEL)]
    act = jax.ShapeDtypeStruct((t, D_MODEL), BF16)
    out_shape = [act, jax.ShapeDtypeStruct((D_MODEL, t), BF16), act, jax.ShapeDtypeStruct((t, _TAB_W), F32),
                 jax.ShapeDtypeStruct((2 * _N_STATES, t), F32), jax.ShapeDtypeStruct((8 * (t // _L), _GATE_LANES), F32),
                 act, act, act]
    return pl.pallas_call(
        functools.partial(_proj_kernel, tm=tm),
        grid=(nt,),
        in_specs=in_specs,
        out_specs=out_specs,
        out_shape=out_shape,
        scratch_shapes=[pltpu.VMEM((tm + 2 * _HALO, D_MODEL), BF16),
                        pltpu.VMEM((2, tm + 2 * _HALO, D_MODEL), F32),
                        pltpu.VMEM((tm, D_MODEL), BF16)],
        compiler_params=pltpu.CompilerParams(dimension_semantics=("parallel",), vmem_limit_bytes=_VMEM_LIMIT_V7X),
        name="projection",
    )(x, x, x, shift, scale, g1, wqkv, wg, wrest, bg, cw, lng, lnb, wsd, bs2, wbs)


def _mlstm_kernel(qf_ref, ktf_ref, vf_ref, tabf_ref, grf_ref, gmf_ref, qb_ref, ktb_ref, vb_ref, tabb_ref, grb_ref, gmb_ref,
                  c0_ref, m0_ref, hf_ref, hb_ref, c_ref, m_ref):
    @pl.when(pl.program_id(0) == 0)
    def _():
        c_ref[...] = c0_ref[...]
        m_ref[...] = m0_ref[...]

    row = lax.broadcasted_iota(jnp.int32, (_L, _L), 0)
    col = lax.broadcasted_iota(jnp.int32, (_L, _L), 1)
    ones = jnp.ones((_L, _GATE_LANES), BF16)
    m_prev = m_ref[...]
    m_next = []
    dirs = ((qf_ref, ktf_ref, vf_ref, tabf_ref, grf_ref, gmf_ref, hf_ref, col <= row),
            (qb_ref, ktb_ref, vb_ref, tabb_ref, grb_ref, gmb_ref, hb_ref, col >= row))
    for d, (q_ref, kt_ref, v_ref, tab_ref, gr_ref, gm_ref, h_ref, mask) in enumerate(dirs):
        cum = tab_ref[:, 0:_GATE_LANES]
        m_row = jnp.maximum(cum + m_prev, tab_ref[:, _GATE_LANES:2 * _GATE_LANES])
        e = cum - m_row
        total, a_max = gm_ref[0:1, :], gm_ref[1:2, :]
        m_new = jnp.maximum(total + m_prev, a_max)
        a_old = jnp.exp(total + m_prev - m_new)
        m_next.append(m_new)
        gr = gr_ref[...]
        for hd in range(HEADS):
            j = d * HEADS + hd
            cols = slice(hd * HEAD_DIM, (hd + 1) * HEAD_DIM)
            qh, kth = q_ref[:, cols], kt_ref[cols, :]
            v_ext = jnp.concatenate([v_ref[:, cols], ones], axis=1)
            e_b = jnp.broadcast_to(e[:, j:j + 1], (_L, _GATE_LANES))
            m_row_b = jnp.broadcast_to(m_row[:, j:j + 1], (_L, _GATE_LANES))
            c_row, a_row = gr[j:j + 1, :], gr[_N_STATES + j:_N_STATES + j + 1, :]
            p = jnp.exp(jnp.where(mask, jnp.concatenate([e_b, e_b], axis=1) + c_row, -jnp.inf))
            s = (_dot(qh, kth) * p).astype(BF16)
            c_prev = c_ref[j]
            w_inter = jnp.exp(e_b + m_prev[:, j:j + 1])
            num = _dot(s, v_ext) + jnp.concatenate([w_inter] * 3, axis=1) * _dot(qh, c_prev.astype(BF16))
            inv = 1.0 / jnp.maximum(jnp.abs(num[:, HEAD_DIM:]), jnp.exp(-m_row_b))
            h_ref[:, cols] = (num[:, :HEAD_DIM] * jnp.concatenate([inv, inv], axis=1)).astype(BF16)
            kw_t = (kth.astype(F32) * jnp.exp(a_row - m_new[:, j:j + 1])).astype(BF16)
            c_ref[j] = a_old[:, j:j + 1] * c_prev + _dot(kw_t, v_ext)
    lane = lax.broadcasted_iota(jnp.int32, (1, _GATE_LANES), 1)
    m_ref[...] = jnp.where(lane < HEADS, m_next[0], m_next[1])


def _mlstm(q, kt, v, tab, gr, gm, c0, m0):
    t = q.shape[0]
    nc = t // _L
    fwd_i = lambda i: i
    bwd_i = lambda i: nc - 1 - i

    def specs(ix):
        rows = lambda w: pl.BlockSpec((_L, w), lambda i: (ix(i), 0))
        return [rows(D_MODEL), pl.BlockSpec((D_MODEL, _L), lambda i: (0, ix(i))), rows(D_MODEL), rows(_TAB_W),
                pl.BlockSpec((2 * _N_STATES, _L), lambda i: (0, ix(i))),
                pl.BlockSpec((8, _GATE_LANES), lambda i: (ix(i), 0))]

    out_rows = lambda ix: pl.BlockSpec((_L, D_MODEL), lambda i: (ix(i), 0))
    state_specs = [pl.BlockSpec(a.shape, lambda i, nd=a.ndim: (0,) * nd) for a in (c0, m0)]
    act = jax.ShapeDtypeStruct((t, D_MODEL), BF16)
    return pl.pallas_call(
        _mlstm_kernel,
        grid=(nc,),
        in_specs=specs(fwd_i) + specs(bwd_i) + [_const_spec(c0.shape), _const_spec(m0.shape)],
        out_specs=[out_rows(fwd_i), out_rows(bwd_i)] + state_specs,
        out_shape=[act, act] + [jax.ShapeDtypeStruct(a.shape, F32) for a in (c0, m0)],
        compiler_params=pltpu.CompilerParams(dimension_semantics=("arbitrary",), vmem_limit_bytes=_VMEM_LIMIT_V7X),
        name="mlstm",
    )(q, kt, v, tab, gr, gm, q, kt, v, tab, gr, gm, c0, m0)


def _post_kernel(hf_ref, hb_ref, so_ref, sgm_ref, s_ref, x_ref, hg_ref, wbm_ref, wout_ref, gate_ref,
                 g2_ref, shift_ref, scale_ref, hx_ref, hn2_ref, ym_s, y_s):
    tm = x_ref.shape[0]
    edges = tuple(range(0, tm + 1, tm // _POST_PIECES))

    def gate(p):
        rows = slice(edges[p], edges[p + 1])
        hm = hf_ref[rows, :].astype(F32) + hb_ref[rows, :].astype(F32)
        for hd in range(HEADS):
            cols = slice(hd * HEAD_DIM, (hd + 1) * HEAD_DIM)
            ym = _rms(hm[:, cols]) * hg_ref[:, cols]
            ym_s[rows, cols] = (so_ref[rows, cols].astype(F32) * ym).astype(BF16)

    def merge(p):
        rows = slice(edges[p], edges[p + 1])
        y = sgm_ref[rows, :].astype(F32) * _dot(ym_s[rows, :], wbm_ref[...]) + s_ref[rows, :].astype(F32)
        y_s[rows, :] = y.astype(BF16)

    def out(p):
        rows = slice(edges[p], edges[p + 1])
        hx = x_ref[rows, :] + gate_ref[...] * _dot(y_s[rows, :], wout_ref[...])
        hx_ref[rows, :] = hx
        hn2_ref[rows, :] = ((_rms(hx) * g2_ref[...]) * (1.0 + scale_ref[...]) + shift_ref[...]).astype(BF16)

    gate(0)
    for p in range(_POST_PIECES):
        if p + 1 < _POST_PIECES:
            gate(p + 1)
        merge(p)
        if p >= 1:
            out(p - 1)
    out(_POST_PIECES - 1)


def _post(hf, hb, so, sgm, s, x, hg, wbm, wout, gate, g2, shift, scale, *, tm):
    t = x.shape[0]
    tile = lambda: pl.BlockSpec((tm, D_MODEL), lambda i: (i, 0))
    return pl.pallas_call(
        _post_kernel,
        grid=(t // tm,),
        in_specs=[tile() for _ in range(6)] + [_const_spec(a.shape) for a in (hg, wbm, wout, gate, g2, shift, scale)],
        out_specs=[tile(), tile()],
        out_shape=[jax.ShapeDtypeStruct((t, D_MODEL), F32), jax.ShapeDtypeStruct((t, D_MODEL), BF16)],
        scratch_shapes=[pltpu.VMEM((tm, D_MODEL), BF16), pltpu.VMEM((tm, D_MODEL), BF16)],
        compiler_params=pltpu.CompilerParams(dimension_semantics=("parallel",), vmem_limit_bytes=_VMEM_LIMIT_V7X),
        name="post_mixer",
    )(hf, hb, so, sgm, s, x, hg, wbm, wout, gate, g2, shift, scale)


def _ffn_kernel(hn_ref, hnp_ref, hnn_ref, hx_ref, wab_ref, cw_ref, wdn_ref, gate_ref, fg_ref, o_ref,
                hne_s, a_s, b_s, act_s, acc_s, *, tm):
    i = pl.program_id(0)
    has_above = i > 0
    has_below = i < pl.num_programs(0) - 1
    ext = tm + 2 * GRID_W
    hne_s[0:GRID_W, :] = hnp_ref[...]
    hne_s[GRID_W:GRID_W + tm, :] = hn_ref[...]
    hne_s[GRID_W + tm:, :] = hnn_ref[...]
    for slot in range(2):
        a_s[slot, 0:_PAD, :] = jnp.zeros((_PAD, _FF_CHUNK), F32)
        a_s[slot, _PAD + ext:, :] = jnp.zeros((_PAD, _FF_CHUNK), F32)
    up_rows = ext // _FFN_PIECES
    mix_edges = tuple(min(j * up_rows, tm) for j in range(_FFN_PIECES + 1))
    down_edges = tuple(range(0, tm + 1, tm // _FFN_DOWN_PIECES))

    def up_piece(c, j):
        slot = c % 2
        m0, m1 = j * up_rows, (j + 1) * up_rows
        ab = _dot(hne_s[m0:m1, :], wab_ref[:, 2 * c * _FF_CHUNK:2 * (c + 1) * _FF_CHUNK])
        a = ab[:, :_FF_CHUNK]
        lo, hi = max(m0, GRID_W), min(m1, GRID_W + tm)
        if m0 < GRID_W:
            a_s[slot, _PAD + m0:_PAD + GRID_W, :] = jnp.where(has_above, a[:GRID_W - m0, :], 0.0)
        a_s[slot, _PAD + lo:_PAD + hi, :] = a[lo - m0:hi - m0, :]
        if m1 > GRID_W + tm:
            a_s[slot, _PAD + GRID_W + tm:_PAD + m1, :] = jnp.where(has_below, a[GRID_W + tm - m0:, :], 0.0)
        b_s[slot, lo - GRID_W:hi - GRID_W, :] = ab[lo - m0:hi - m0, _FF_CHUNK:]

    def mix_piece(c, p):
        slot = c % 2
        r0, r1 = mix_edges[p], mix_edges[p + 1]
        n = r1 - r0 + 2 * GRID_W
        gcol = lax.broadcasted_iota(jnp.int32, (n, 1), 0) % GRID_W
        taps = (jnp.where(gcol >= 1, a_s[slot, _PAD - 1 + r0:_PAD - 1 + r0 + n, :], 0.0),
                a_s[slot, _PAD + r0:_PAD + r0 + n, :],
                jnp.where(gcol <= GRID_W - 2, a_s[slot, _PAD + 1 + r0:_PAD + 1 + r0 + n, :], 0.0))
        cw = cw_ref[:, c * _FF_CHUNK:(c + 1) * _FF_CHUNK]
        conv = None
        for dr in range(3):
            for dc in range(3):
                term = cw[3 * dr + dc:3 * dr + dc + 1, :] * taps[dc][dr * GRID_W:dr * GRID_W + r1 - r0, :]
                conv = term if conv is None else conv + term
        act_s[slot, r0:r1, :] = (_silu(conv) * b_s[slot, r0:r1, :]).astype(BF16)

    def down_piece(c, p):
        r0, r1 = down_edges[p], down_edges[p + 1]
        part = _dot(act_s[c % 2, r0:r1, :], wdn_ref[c * _FF_CHUNK:(c + 1) * _FF_CHUNK, :])
        if c == 0:
            acc_s[r0:r1, :] = part
        else:
            acc_s[r0:r1, :] += part

    for step in range(_N_FF_CHUNKS + 2):
        for j in range(_FFN_PIECES):
            if step < _N_FF_CHUNKS:
                up_piece(step, j)
            if 0 <= step - 1 < _N_FF_CHUNKS:
                mix_piece(step - 1, j)
        if 0 <= step - 2:
            for p in range(_FFN_DOWN_PIECES):
                down_piece(step - 2, p)
    h2 = hx_ref[...] + gate_ref[...] * acc_s[...]
    o_ref[...] = _rms(h2) * fg_ref[...]


def _ffn(hn2, hx, wab, cwf, wdn, gate, fg, *, tm):
    t = hx.shape[0]
    rb = tm // GRID_W
    nrb = t // GRID_W
    tile = lambda: pl.BlockSpec((tm, D_MODEL), lambda i: (i, 0))
    ext = tm + 2 * GRID_W
    return pl.pallas_call(
        functools.partial(_ffn_kernel, tm=tm),
        grid=(t // tm,),
        in_specs=[tile(),
                  pl.BlockSpec((GRID_W, D_MODEL), lambda i: (jnp.maximum(i * rb - 1, 0), 0)),
                  pl.BlockSpec((GRID_W, D_MODEL), lambda i: (jnp.minimum((i + 1) * rb, nrb - 1), 0)),
                  tile()] + [_const_spec(a.shape) for a in (wab, cwf, wdn, gate, fg)],
        out_specs=tile(),
        out_shape=jax.ShapeDtypeStruct((t, D_MODEL), F32),
        scratch_shapes=[pltpu.VMEM((ext, D_MODEL), BF16),
                        pltpu.VMEM((2, ext + 2 * _PAD, _FF_CHUNK), F32),
                        pltpu.VMEM((2, tm, _FF_CHUNK), F32),
                        pltpu.VMEM((2, tm, _FF_CHUNK), BF16),
                        pltpu.VMEM((tm, D_MODEL), F32)],
        compiler_params=pltpu.CompilerParams(dimension_semantics=("parallel",), vmem_limit_bytes=_VMEM_LIMIT_V7X),
        name="conv_ffn",
    )(hn2, hn2, hn2, hx, wab, cwf, wdn, gate, fg)


def _gate_columns(wg):
    idx_i = [2 * d * HEADS + h for d in range(2) for h in range(HEADS)]
    idx_f = [(2 * d + 1) * HEADS + h for d in range(2) for h in range(HEADS)]
    pad = [(0, 0)] * (wg.ndim - 1) + [(0, _GATE_LANES - _N_STATES)]
    return jnp.concatenate([jnp.pad(wg[..., jnp.array(idx)], pad) for idx in (idx_i, idx_f)], axis=-1)


def _tile_size(t, want):
    tm = min(want, t)
    assert t % tm == 0 and tm % _L == 0, (t, tm)
    return tm


def kernel(x, c, ctx, c_ctx, w_mod, b_mod, norm1_g, w_in, b_gate, conv_qk, head_norm_g, sgu_ln_g, sgu_ln_b, w_s, b_s,
           w_branch_mlstm, w_branch_sgu, w_out, norm2_g, w_up, w_ffn_conv, w_down, final_g):
    assert x.shape[0] == 1 and w_mod.shape[0] == 1, "single batch element, single layer"
    d = D_MODEL
    x2, ctx2 = x[0], ctx[0]
    t = x2.shape[0]
    assert t % GRID_W == 0 and ctx2.shape[0] % _L == 0
    row = lambda a: a.reshape(1, -1)

    cc = jnp.zeros((d, 8), F32).at[:, 0].set(c[0]).at[:, 1].set(c_ctx)
    mod = _modulation(cc, w_mod[0], row(b_mod[0]))
    mx = [mod[0:1, j * d:(j + 1) * d] for j in range(N_MOD)]
    mc = [mod[1:2, j * d:(j + 1) * d] for j in range(N_MOD)]

    w_in_t = jnp.swapaxes(w_in[0], 0, 1)
    wqkv = _cast_columns(w_in_t, 0, _W_GATES)
    wg = _gate_weights(w_in_t)
    wrest = _cast_columns(w_in_t, _W_REST, w_in_t.shape[0] - _W_REST)
    bg = _gate_columns(row(b_gate[0]))
    eye2 = jnp.eye(2, dtype=F32)
    wsd = jnp.stack([jnp.kron(eye2, w_s[0, g]) for g in range(HEADS)]).astype(BF16)
    bs2 = jnp.tile(b_s[0].T, (2, 1))
    proj_w = (row(norm1_g[0]), wqkv, wg, wrest, bg, conv_qk[0],
              row(sgu_ln_g[0]), row(sgu_ln_b[0]), wsd, bs2, w_branch_sgu[0].astype(BF16))

    pc = _projection(ctx2, mc[0], mc[1], *proj_w, tm=_tile_size(ctx2.shape[0], 512))
    c0 = jnp.zeros((_N_STATES, HEAD_DIM, _C_EXT), F32)
    m0 = jnp.full((1, _GATE_LANES), M_INIT, F32)
    _, _, c1, m1 = _mlstm(*pc[:6], c0, m0)

    q, kt, v, tab, gr, gm, so, sgm, s = _projection(x2, mx[0], mx[1], *proj_w, tm=_tile_size(t, 512))
    hf, hb, _, _ = _mlstm(q, kt, v, tab, gr, gm, c1, m1)
    hx, hn2 = _post(hf, hb, so, sgm, s, x2, row(head_norm_g[0]), w_branch_mlstm[0].astype(BF16),
                    w_out[0].astype(BF16), mx[2], row(norm2_g[0]), mx[3], mx[4], tm=_tile_size(t, 1024))

    out = _ffn(hn2, hx, _regroup_up(w_up[0]), w_ffn_conv[0].reshape(9, D_FF), w_down[0].astype(BF16), mx[5],
               row(final_g), tm=_tile_size(t, 1024))
    return out[None]
```

```python
import functools

import jax
import jax.numpy as jnp
from jax import lax
from jax.experimental import pallas as pl
from jax.experimental.pallas import tpu as pltpu

F32 = jnp.float32
BF16 = jnp.bfloat16

D_MODEL = 1024
GRID_W = 64
HEADS = 4
HEAD_DIM = 256
N_MOD = 6
D_FF = 2816
EPS = 1e-6
M_INIT = -1e30

_L = 256
_HALO = 16
_FF_CHUNK = 256
_N_FF_CHUNKS = D_FF // _FF_CHUNK
_GATE_LANES = 128
_PAD = 8
_PROJ_BLOCK = 256
_FFN_PIECES = 3
_FFN_DOWN_PIECES = 2
_POST_PIECES = 2
_MLSTM_CHUNKS_PER_STEP = 4
_VMEM_LIMIT_V7X = 56 * 1024 * 1024

_N_STATES = 2 * HEADS
_TAB_W = 3 * _GATE_LANES
_C_EXT = HEAD_DIM + _GATE_LANES
_N_GATES = 4 * HEADS
_W_GATES = 3 * D_MODEL
_W_REST = _W_GATES + _N_GATES
_SUBLANES = 8
_PREP_BLOCK = 512


def _dot(a, b):
    return jnp.dot(a, b, preferred_element_type=F32)


def _split3(x):
    h1 = x.astype(BF16)
    r1 = x - h1.astype(F32)
    h2 = r1.astype(BF16)
    h3 = (r1 - h2.astype(F32)).astype(BF16)
    return h1, h2, h3


def _sigmoid(x):
    return 0.5 * jnp.tanh(0.5 * x) + 0.5


def _silu(x):
    h = 0.5 * x
    return h * jnp.tanh(h) + h


def _gelu_tanh(x):
    return 0.5 * x * (1.0 + jnp.tanh(0.7978845608028654 * (x + 0.044715 * (x * x * x))))


def _log_sigmoid(x):
    return jnp.minimum(x, 0.0) - jnp.log(1.0 + jnp.exp(-jnp.abs(x)))


def _rms(x):
    return x * lax.rsqrt(jnp.mean(x * x, axis=-1, keepdims=True) + EPS)


def _const_spec(shape):
    nd = len(shape)
    return pl.BlockSpec(shape, lambda *_: (0,) * nd, pipeline_mode=pl.Buffered(1))


def _cast_kernel(w_ref, o_ref):
    o_ref[...] = w_ref[...].astype(BF16)


def _transpose_cast_kernel(w_ref, o_ref):
    o_ref[...] = w_ref[...].T.astype(BF16)


def _shift_transpose_cast_kernel(w_ref, next_ref, o_ref, *, shift):
    x = jnp.concatenate([w_ref[...], next_ref[...]], axis=0)
    o_ref[...] = x[shift:shift + w_ref.shape[0], :].T.astype(BF16)


def _cast_columns(w_t, col0, ncols):
    rows = w_t.shape[1]
    shift = col0 % _PREP_BLOCK
    base = col0 - shift
    assert shift % _SUBLANES == 0 and ncols % _PREP_BLOCK == 0 and col0 + ncols <= w_t.shape[0]
    b0 = base // _PREP_BLOCK
    in_specs = [pl.BlockSpec((_PREP_BLOCK, rows), lambda k: (b0 + k, 0))]
    operands = [w_t]
    body = _transpose_cast_kernel
    if shift:
        assert _PREP_BLOCK % shift == 0
        per_block = _PREP_BLOCK // shift
        in_specs.append(pl.BlockSpec((shift, rows), lambda k: ((b0 + k + 1) * per_block, 0)))
        operands.append(w_t)
        body = functools.partial(_shift_transpose_cast_kernel, shift=shift)
    return pl.pallas_call(
        body,
        grid=(ncols // _PREP_BLOCK,),
        in_specs=in_specs,
        out_specs=pl.BlockSpec((rows, _PREP_BLOCK), lambda k: (0, k)),
        out_shape=jax.ShapeDtypeStruct((rows, ncols), BF16),
        compiler_params=pltpu.CompilerParams(dimension_semantics=("parallel",)),
        name="cast_columns",
    )(*operands)


def _regroup_up(w_up):
    rows = w_up.shape[0]
    return pl.pallas_call(
        _cast_kernel,
        grid=(2 * _N_FF_CHUNKS,),
        in_specs=[pl.BlockSpec((rows, _FF_CHUNK), lambda j: (0, (j % 2) * _N_FF_CHUNKS + j // 2))],
        out_specs=pl.BlockSpec((rows, _FF_CHUNK), lambda j: (0, j)),
        out_shape=jax.ShapeDtypeStruct((rows, 2 * D_FF), BF16),
        compiler_params=pltpu.CompilerParams(dimension_semantics=("parallel",)),
        name="regroup_up",
    )(w_up)


def _gate_cast_kernel(w_ref, sel_ref, o_ref):
    o_ref[...] = lax.dot_general(w_ref[...].astype(BF16), sel_ref[...], (((0,), (0,)), ((), ())),
                                 preferred_element_type=F32).astype(BF16)


def _gate_selector():
    sel = [[0.0] * (2 * _GATE_LANES) for _ in range(_N_GATES)]
    for d in range(2):
        for h in range(HEADS):
            j = d * HEADS + h
            sel[2 * d * HEADS + h][j] = 1.0
            sel[(2 * d + 1) * HEADS + h][_GATE_LANES + j] = 1.0
    return jnp.array(sel, BF16)


def _gate_weights(w_t):
    rows = w_t.shape[1]
    assert _W_GATES % _N_GATES == 0
    return pl.pallas_call(
        _gate_cast_kernel,
        grid=(1,),
        in_specs=[pl.BlockSpec((_N_GATES, rows), lambda k: (_W_GATES // _N_GATES, 0)),
                  pl.BlockSpec((_N_GATES, 2 * _GATE_LANES), lambda k: (0, 0))],
        out_specs=pl.BlockSpec((rows, 2 * _GATE_LANES), lambda k: (0, 0)),
        out_shape=jax.ShapeDtypeStruct((rows, 2 * _GATE_LANES), BF16),
        name="gate_weights",
    )(w_t, _gate_selector())


def _mod_kernel(cc_ref, w_ref, b_ref, o_ref):
    s = _silu(cc_ref[...])
    w = w_ref[...]
    rows = [jnp.sum(w * s[:, r:r + 1], axis=0, keepdims=True) + b_ref[...] for r in range(2)]
    o_ref[...] = jnp.concatenate(rows + [jnp.zeros((6, w.shape[1]), F32)], axis=0)


def _modulation(cc, w_mod, b_mod):
    n_out = w_mod.shape[1]
    nb = D_MODEL
    return pl.pallas_call(
        _mod_kernel,
        grid=(n_out // nb,),
        in_specs=[pl.BlockSpec((D_MODEL, 8), lambda j: (0, 0)),
                  pl.BlockSpec((D_MODEL, nb), lambda j: (0, j)),
                  pl.BlockSpec((1, nb), lambda j: (0, j))],
        out_specs=pl.BlockSpec((8, nb), lambda j: (0, j)),
        out_shape=jax.ShapeDtypeStruct((8, n_out), F32),
        compiler_params=pltpu.CompilerParams(dimension_semantics=("arbitrary",)),
        name="modulation",
    )(cc, w_mod, b_mod)


def _gate_tables(gi, lf):
    row = lax.broadcasted_iota(jnp.int32, (_L, _L), 0)
    col = lax.broadcasted_iota(jnp.int32, (_L, _L), 1)
    ltri = (col <= row).astype(BF16)
    l1, l2, l3 = _split3(lf)
    prefix = _dot(ltri, l3) + _dot(ltri, l2) + _dot(ltri, l1)
    total = prefix[_L - 1:_L, :]
    suffix = total - prefix + lf
    lane = lax.broadcasted_iota(jnp.int32, (1, _GATE_LANES), 1)
    cum = jnp.where(lane >= HEADS, suffix, prefix)
    a = total - cum + gi
    c = gi - cum
    c_rows = c.T[0:_N_STATES, :]
    a_rows = a.T[0:_N_STATES, :]
    pos = lax.broadcasted_iota(jnp.int32, (_N_STATES, _L), 1)
    fwd_max, bwd_max = c_rows, c_rows
    k = 1
    while k < _L:
        fwd_max = jnp.maximum(fwd_max, jnp.where(pos >= k, pltpu.roll(fwd_max, k, 1), -jnp.inf))
        bwd_max = jnp.maximum(bwd_max, jnp.where(pos < _L - k, pltpu.roll(bwd_max, _L - k, 1), -jnp.inf))
        k *= 2
    state = lax.broadcasted_iota(jnp.int32, (_N_STATES, _L), 0)
    run_max = jnp.where(state < HEADS, fwd_max, bwd_max)
    run_max_cols = jnp.concatenate([run_max, jnp.zeros((_GATE_LANES - _N_STATES, _L), F32)], axis=0).T
    return cum, cum + run_max_cols, a, c_rows, a_rows, total, jnp.max(a, axis=0, keepdims=True)


def _proj_kernel(x_ref, xp_ref, xn_ref, shift_ref, scale_ref, g1_ref, wqkv_ref, wg_ref, wrest_ref, bg_ref,
                 cw_ref, lng_ref, lnb_ref, wsd_ref, bs_ref, wbs_ref,
                 q_ref, kt_ref, v_ref, tab_ref, gr_ref, gm_ref, so_ref, sgm_ref, s_ref,
                 hn_s, z_s, ys_s, *, tm):
    i = pl.program_id(0)
    has_prev = i > 0
    has_next = i < pl.num_programs(0) - 1
    scale1 = 1.0 + scale_ref[...]

    def norm_mod(xb):
        return ((_rms(xb) * g1_ref[...]) * scale1 + shift_ref[...]).astype(BF16)

    hn_s[0:_HALO, :] = norm_mod(xp_ref[...])
    hn_s[_HALO:_HALO + tm, :] = norm_mod(x_ref[...])
    hn_s[_HALO + tm:, :] = norm_mod(xn_ref[...])
    hn = hn_s[_HALO:_HALO + tm, :]

    bw = _PROJ_BLOCK
    u_blocks, vg_blocks = [], []

    def qkv_cols(seg, b):
        return wqkv_ref[:, seg * D_MODEL + b * bw:seg * D_MODEL + (b + 1) * bw]

    def rest_cols(seg, b):
        return wrest_ref[:, seg * D_MODEL + b * bw:seg * D_MODEL + (b + 1) * bw]

    def qk_block(c, b):
        lanes = slice(b * bw, (b + 1) * bw)
        z = _dot(hn_s[...], qkv_cols(c, b))
        z_s[c, 0:_HALO, lanes] = jnp.where(has_prev, z[0:_HALO, :], 0.0)
        z_s[c, _HALO:_HALO + tm, lanes] = z[_HALO:_HALO + tm, :]
        z_s[c, _HALO + tm:, lanes] = jnp.where(has_next, z[_HALO + tm:, :], 0.0)
        cw = cw_ref[:, c * D_MODEL + b * bw:c * D_MODEL + (b + 1) * bw]
        y = (cw[0:1, :] * z_s[c, _HALO - 1:_HALO - 1 + tm, lanes] + cw[1:2, :] * z_s[c, _HALO:_HALO + tm, lanes]
             + cw[2:3, :] * z_s[c, _HALO + 1:_HALO + 1 + tm, lanes])
        y = _silu(y)
        if c == 0:
            q_ref[:, lanes] = (y * (HEAD_DIM ** -0.5)).astype(BF16)
        else:
            kt_ref[lanes, :] = y.T.astype(BF16)

    for b in range(D_MODEL // bw):
        lanes = slice(b * bw, (b + 1) * bw)
        qk_block(0, b)
        v_ref[:, lanes] = _dot(hn, qkv_cols(2, b)).astype(BF16)
        qk_block(1, b)
        so_ref[:, lanes] = _sigmoid(_dot(hn, rest_cols(0, b))).astype(BF16)
        u_blocks.append(_gelu_tanh(_dot(hn, rest_cols(1, b))))
        sgm_ref[:, lanes] = _sigmoid(_dot(hn, rest_cols(3, b))).astype(BF16)
        vg_blocks.append(_gelu_tanh(_dot(hn, rest_cols(2, b))))

    gates = _dot(hn, wg_ref[...]) + bg_ref[...]
    gi = gates[:, 0:_GATE_LANES]
    lf = _log_sigmoid(gates[:, _GATE_LANES:])
    for j in range(tm // _L):
        rows = slice(j * _L, (j + 1) * _L)
        cum, mi, a, c_rows, a_rows, total, a_max = _gate_tables(gi[rows, :], lf[rows, :])
        tab_ref[rows, 0:_GATE_LANES] = cum
        tab_ref[rows, _GATE_LANES:2 * _GATE_LANES] = mi
        tab_ref[rows, 2 * _GATE_LANES:] = a
        gr_ref[0:_N_STATES, rows] = c_rows
        gr_ref[_N_STATES:, rows] = a_rows
        gm_ref[j * 8:(j + 1) * 8, :] = jnp.concatenate([total, a_max, jnp.zeros((6, _GATE_LANES), F32)], axis=0)

    vg = jnp.concatenate(vg_blocks, axis=1)
    u = jnp.concatenate(u_blocks, axis=1)
    vg = vg - jnp.mean(vg, axis=-1, keepdims=True)
    vn = (vg * lax.rsqrt(jnp.mean(vg * vg, axis=-1, keepdims=True) + EPS) * lng_ref[...] + lnb_ref[...]).astype(BF16)
    for p in range(tm // _L):
        for g in range(HEADS):
            rows = slice(p * _L, (p + 1) * _L)
            cols = slice(g * HEAD_DIM, (g + 1) * HEAD_DIM)
            mixed = _dot(wsd_ref[g], vn[rows, cols]) + bs_ref[:, g:g + 1]
            ys_s[rows, cols] = (u[rows, cols] * mixed).astype(BF16)
    gg = _sigmoid(_dot(hn, wrest_ref[:, 4 * D_MODEL:5 * D_MODEL]))
    s_ref[...] = (gg * _dot(ys_s[...], wbs_ref[...])).astype(BF16)


def _projection(x, shift, scale, g1, wqkv, wg, wrest, bg, cw, lng, lnb, wsd, bs2, wbs, *, tm):
    t = x.shape[0]
    nt = t // tm
    hb = tm // _HALO
    nhb = t // _HALO
    tile = lambda w: pl.BlockSpec((tm, w), lambda i: (i, 0))
    in_specs = [
        tile(D_MODEL),
        pl.BlockSpec((_HALO, D_MODEL), lambda i: (jnp.maximum(i * hb - 1, 0), 0)),
        pl.BlockSpec((_HALO, D_MODEL), lambda i: (jnp.minimum((i + 1) * hb, nhb - 1), 0)),
    ] + [_const_spec(a.shape) for a in (shift, scale, g1, wqkv, wg, wrest, bg, cw, lng, lnb, wsd, bs2, wbs)]
    gm_rows = 8 * (tm // _L)
    out_specs = [tile(D_MODEL), pl.BlockSpec((D_MODEL, tm), lambda i: (0, i)), tile(D_MODEL), tile(_TAB_W),
                 pl.BlockSpec((2 * _N_STATES, tm), lambda i: (0, i)),
                 pl.BlockSpec((gm_rows, _GATE_LANES), lambda i: (i, 0)),
                 tile(D_MODEL), tile(D_MODEL), tile(D_MODEL)]
    act = jax.ShapeDtypeStruct((t, D_MODEL), BF16)
    out_shape = [act, jax.ShapeDtypeStruct((D_MODEL, t), BF16), act, jax.ShapeDtypeStruct((t, _TAB_W), F32),
                 jax.ShapeDtypeStruct((2 * _N_STATES, t), F32), jax.ShapeDtypeStruct((8 * (t // _L), _GATE_LANES), F32),
                 act, act, act]
    return pl.pallas_call(
        functools.partial(_proj_kernel, tm=tm),
        grid=(nt,),
        in_specs=in_specs,
        out_specs=out_specs,
        out_shape=out_shape,
        scratch_shapes=[pltpu.VMEM((tm + 2 * _HALO, D_MODEL), BF16),
                        pltpu.VMEM((2, tm + 2 * _HALO, D_MODEL), F32),
                        pltpu.VMEM((tm, D_MODEL), BF16)],
        compiler_params=pltpu.CompilerParams(dimension_semantics=("parallel",), vmem_limit_bytes=_VMEM_LIMIT_V7X),
        name="projection",
    )(x, x, x, shift, scale, g1, wqkv, wg, wrest, bg, cw, lng, lnb, wsd, bs2, wbs)


def _mlstm_kernel(qf_ref, ktf_ref, vf_ref, tabf_ref, grf_ref, gmf_ref, qb_ref, ktb_ref, vb_ref, tabb_ref, grb_ref, gmb_ref,
                  c0_ref, m0_ref, hf_ref, hb_ref, c_ref, m_ref, *, per_step):
    @pl.when(pl.program_id(0) == 0)
    def _():
        c_ref[...] = c0_ref[...]
        m_ref[...] = m0_ref[...]

    row = lax.broadcasted_iota(jnp.int32, (_L, _L), 0)
    col = lax.broadcasted_iota(jnp.int32, (_L, _L), 1)
    ones = jnp.ones((_L, _GATE_LANES), BF16)
    lane = lax.broadcasted_iota(jnp.int32, (1, _GATE_LANES), 1)
    dirs = ((qf_ref, ktf_ref, vf_ref, tabf_ref, grf_ref, gmf_ref, hf_ref, col <= row),
            (qb_ref, ktb_ref, vb_ref, tabb_ref, grb_ref, gmb_ref, hb_ref, col >= row))
    for sub in range(per_step):
        m_prev = m_ref[...]
        m_next = []
        for d, (q_ref, kt_ref, v_ref, tab_ref, gr_ref, gm_ref, h_ref, mask) in enumerate(dirs):
            k = sub if d == 0 else per_step - 1 - sub
            rows = slice(k * _L, (k + 1) * _L)
            cum = tab_ref[rows, 0:_GATE_LANES]
            m_row = jnp.maximum(cum + m_prev, tab_ref[rows, _GATE_LANES:2 * _GATE_LANES])
            e = cum - m_row
            total, a_max = gm_ref[8 * k:8 * k + 1, :], gm_ref[8 * k + 1:8 * k + 2, :]
            m_new = jnp.maximum(total + m_prev, a_max)
            a_old = jnp.exp(total + m_prev - m_new)
            m_next.append(m_new)
            gr = gr_ref[:, rows]
            for hd in range(HEADS):
                j = d * HEADS + hd
                cols = slice(hd * HEAD_DIM, (hd + 1) * HEAD_DIM)
                qh, kth = q_ref[rows, cols], kt_ref[cols, rows]
                v_ext = jnp.concatenate([v_ref[rows, cols], ones], axis=1)
                e_b = jnp.broadcast_to(e[:, j:j + 1], (_L, _GATE_LANES))
                m_row_b = jnp.broadcast_to(m_row[:, j:j + 1], (_L, _GATE_LANES))
                c_row, a_row = gr[j:j + 1, :], gr[_N_STATES + j:_N_STATES + j + 1, :]
                p = jnp.exp(jnp.where(mask, jnp.concatenate([e_b, e_b], axis=1) + c_row, -jnp.inf))
                s = (_dot(qh, kth) * p).astype(BF16)
                c_prev = c_ref[j]
                w_inter = jnp.exp(e_b + m_prev[:, j:j + 1])
                num = _dot(s, v_ext) + jnp.concatenate([w_inter] * 3, axis=1) * _dot(qh, c_prev.astype(BF16))
                inv = 1.0 / jnp.maximum(jnp.abs(num[:, HEAD_DIM:]), jnp.exp(-m_row_b))
                h_ref[rows, cols] = (num[:, :HEAD_DIM] * jnp.concatenate([inv, inv], axis=1)).astype(BF16)
                kw_t = (kth.astype(F32) * jnp.exp(a_row - m_new[:, j:j + 1])).astype(BF16)
                c_ref[j] = a_old[:, j:j + 1] * c_prev + _dot(kw_t, v_ext)
        m_ref[...] = jnp.where(lane < HEADS, m_next[0], m_next[1])


def _mlstm(q, kt, v, tab, gr, gm, c0, m0):
    t = q.shape[0]
    per_step = min(_MLSTM_CHUNKS_PER_STEP, t // _L)
    blk = per_step * _L
    assert t % blk == 0
    ns = t // blk
    fwd_i = lambda i: i
    bwd_i = lambda i: ns - 1 - i

    def specs(ix):
        rows = lambda w: pl.BlockSpec((blk, w), lambda i: (ix(i), 0))
        return [rows(D_MODEL), pl.BlockSpec((D_MODEL, blk), lambda i: (0, ix(i))), rows(D_MODEL), rows(_TAB_W),
                pl.BlockSpec((2 * _N_STATES, blk), lambda i: (0, ix(i))),
                pl.BlockSpec((8 * per_step, _GATE_LANES), lambda i: (ix(i), 0))]

    out_rows = lambda ix: pl.BlockSpec((blk, D_MODEL), lambda i: (ix(i), 0))
    state_specs = [pl.BlockSpec(a.shape, lambda i, nd=a.ndim: (0,) * nd) for a in (c0, m0)]
    act = jax.ShapeDtypeStruct((t, D_MODEL), BF16)
    return pl.pallas_call(
        functools.partial(_mlstm_kernel, per_step=per_step),
        grid=(ns,),
        in_specs=specs(fwd_i) + specs(bwd_i) + [_const_spec(c0.shape), _const_spec(m0.shape)],
        out_specs=[out_rows(fwd_i), out_rows(bwd_i)] + state_specs,
        out_shape=[act, act] + [jax.ShapeDtypeStruct(a.shape, F32) for a in (c0, m0)],
        compiler_params=pltpu.CompilerParams(dimension_semantics=("arbitrary",), vmem_limit_bytes=_VMEM_LIMIT_V7X),
        name="mlstm",
    )(q, kt, v, tab, gr, gm, q, kt, v, tab, gr, gm, c0, m0)


def _post_kernel(hf_ref, hb_ref, so_ref, sgm_ref, s_ref, x_ref, hg_ref, wbm_ref, wout_ref, gate_ref,
                 g2_ref, shift_ref, scale_ref, hx_ref, hn2_ref, ym_s, y_s):
    tm = x_ref.shape[0]
    edges = tuple(range(0, tm + 1, tm // _POST_PIECES))

    def gate(p):
        rows = slice(edges[p], edges[p + 1])
        hm = hf_ref[rows, :].astype(F32) + hb_ref[rows, :].astype(F32)
        for hd in range(HEADS):
            cols = slice(hd * HEAD_DIM, (hd + 1) * HEAD_DIM)
            ym = _rms(hm[:, cols]) * hg_ref[:, cols]
            ym_s[rows, cols] = (so_ref[rows, cols].astype(F32) * ym).astype(BF16)

    def merge(p):
        rows = slice(edges[p], edges[p + 1])
        y = sgm_ref[rows, :].astype(F32) * _dot(ym_s[rows, :], wbm_ref[...]) + s_ref[rows, :].astype(F32)
        y_s[rows, :] = y.astype(BF16)

    def out(p):
        rows = slice(edges[p], edges[p + 1])
        hx = x_ref[rows, :] + gate_ref[...] * _dot(y_s[rows, :], wout_ref[...])
        hx_ref[rows, :] = hx
        hn2_ref[rows, :] = ((_rms(hx) * g2_ref[...]) * (1.0 + scale_ref[...]) + shift_ref[...]).astype(BF16)

    gate(0)
    for p in range(_POST_PIECES):
        if p + 1 < _POST_PIECES:
            gate(p + 1)
        merge(p)
        if p >= 1:
            out(p - 1)
    out(_POST_PIECES - 1)


def _post(hf, hb, so, sgm, s, x, hg, wbm, wout, gate, g2, shift, scale, *, tm):
    t = x.shape[0]
    tile = lambda: pl.BlockSpec((tm, D_MODEL), lambda i: (i, 0))
    return pl.pallas_call(
        _post_kernel,
        grid=(t // tm,),
        in_specs=[tile() for _ in range(6)] + [_const_spec(a.shape) for a in (hg, wbm, wout, gate, g2, shift, scale)],
        out_specs=[tile(), tile()],
        out_shape=[jax.ShapeDtypeStruct((t, D_MODEL), F32), jax.ShapeDtypeStruct((t, D_MODEL), BF16)],
        scratch_shapes=[pltpu.VMEM((tm, D_MODEL), BF16), pltpu.VMEM((tm, D_MODEL), BF16)],
        compiler_params=pltpu.CompilerParams(dimension_semantics=("parallel",), vmem_limit_bytes=_VMEM_LIMIT_V7X),
        name="post_mixer",
    )(hf, hb, so, sgm, s, x, hg, wbm, wout, gate, g2, shift, scale)


def _ffn_kernel(hn_ref, hnp_ref, hnn_ref, hx_ref, wab_ref, cw_ref, wdn_ref, gate_ref, fg_ref, o_ref,
                hne_s, a_s, b_s, act_s, acc_s, *, tm):
    i = pl.program_id(0)
    has_above = i > 0
    has_below = i < pl.num_programs(0) - 1
    ext = tm + 2 * GRID_W
    hne_s[0:GRID_W, :] = hnp_ref[...]
    hne_s[GRID_W:GRID_W + tm, :] = hn_ref[...]
    hne_s[GRID_W + tm:, :] = hnn_ref[...]
    for slot in range(2):
        a_s[slot, 0:_PAD, :] = jnp.zeros((_PAD, _FF_CHUNK), F32)
        a_s[slot, _PAD + ext:, :] = jnp.zeros((_PAD, _FF_CHUNK), F32)
    up_rows = ext // _FFN_PIECES
    mix_edges = tuple(min(j * up_rows, tm) for j in range(_FFN_PIECES + 1))
    down_edges = tuple(range(0, tm + 1, tm // _FFN_DOWN_PIECES))

    def up_piece(c, j):
        slot = c % 2
        m0, m1 = j * up_rows, (j + 1) * up_rows
        ab = _dot(hne_s[m0:m1, :], wab_ref[:, 2 * c * _FF_CHUNK:2 * (c + 1) * _FF_CHUNK])
        a = ab[:, :_FF_CHUNK]
        lo, hi = max(m0, GRID_W), min(m1, GRID_W + tm)
        if m0 < GRID_W:
            a_s[slot, _PAD + m0:_PAD + GRID_W, :] = jnp.where(has_above, a[:GRID_W - m0, :], 0.0)
        a_s[slot, _PAD + lo:_PAD + hi, :] = a[lo - m0:hi - m0, :]
        if m1 > GRID_W + tm:
            a_s[slot, _PAD + GRID_W + tm:_PAD + m1, :] = jnp.where(has_below, a[GRID_W + tm - m0:, :], 0.0)
        b_s[slot, lo - GRID_W:hi - GRID_W, :] = ab[lo - m0:hi - m0, _FF_CHUNK:]

    def mix_piece(c, p):
        slot = c % 2
        r0, r1 = mix_edges[p], mix_edges[p + 1]
        n = r1 - r0 + 2 * GRID_W
        gcol = lax.broadcasted_iota(jnp.int32, (n, 1), 0) % GRID_W
        taps = (jnp.where(gcol >= 1, a_s[slot, _PAD - 1 + r0:_PAD - 1 + r0 + n, :], 0.0),
                a_s[slot, _PAD + r0:_PAD + r0 + n, :],
                jnp.where(gcol <= GRID_W - 2, a_s[slot, _PAD + 1 + r0:_PAD + 1 + r0 + n, :], 0.0))
        cw = cw_ref[:, c * _FF_CHUNK:(c + 1) * _FF_CHUNK]
        conv = None
        for dr in range(3):
            for dc in range(3):
                term = cw[3 * dr + dc:3 * dr + dc + 1, :] * taps[dc][dr * GRID_W:dr * GRID_W + r1 - r0, :]
                conv = term if conv is None else conv + term
        act_s[slot, r0:r1, :] = (_silu(conv) * b_s[slot, r0:r1, :]).astype(BF16)

    def down_piece(c, p):
        r0, r1 = down_edges[p], down_edges[p + 1]
        part = _dot(act_s[c % 2, r0:r1, :], wdn_ref[c * _FF_CHUNK:(c + 1) * _FF_CHUNK, :])
        if c == 0:
            acc_s[r0:r1, :] = part
        else:
            acc_s[r0:r1, :] += part

    for step in range(_N_FF_CHUNKS + 2):
        for j in range(_FFN_PIECES):
            if step < _N_FF_CHUNKS:
                up_piece(step, j)
            if 0 <= step - 1 < _N_FF_CHUNKS:
                mix_piece(step - 1, j)
        if 0 <= step - 2:
            for p in range(_FFN_DOWN_PIECES):
                down_piece(step - 2, p)
    h2 = hx_ref[...] + gate_ref[...] * acc_s[...]
    o_ref[...] = _rms(h2) * fg_ref[...]


def _ffn(hn2, hx, wab, cwf, wdn, gate, fg, *, tm):
    t = hx.shape[0]
    rb = tm // GRID_W
    nrb = t // GRID_W
    tile = lambda: pl.BlockSpec((tm, D_MODEL), lambda i: (i, 0))
    ext = tm + 2 * GRID_W
    return pl.pallas_call(
        functools.partial(_ffn_kernel, tm=tm),
        grid=(t // tm,),
        in_specs=[tile(),
                  pl.BlockSpec((GRID_W, D_MODEL), lambda i: (jnp.maximum(i * rb - 1, 0), 0)),
                  pl.BlockSpec((GRID_W, D_MODEL), lambda i: (jnp.minimum((i + 1) * rb, nrb - 1), 0)),
                  tile()] + [_const_spec(a.shape) for a in (wab, cwf, wdn, gate, fg)],
        out_specs=tile(),
        out_shape=jax.ShapeDtypeStruct((t, D_MODEL), F32),
        scratch_shapes=[pltpu.VMEM((ext, D_MODEL), BF16),
                        pltpu.VMEM((2, ext + 2 * _PAD, _FF_CHUNK), F32),
                        pltpu.VMEM((2, tm, _FF_CHUNK), F32),
                        pltpu.VMEM((2, tm, _FF_CHUNK), BF16),
                        pltpu.VMEM((tm, D_MODEL), F32)],
        compiler_params=pltpu.CompilerParams(dimension_semantics=("parallel",), vmem_limit_bytes=_VMEM_LIMIT_V7X),
        name="conv_ffn",
    )(hn2, hn2, hn2, hx, wab, cwf, wdn, gate, fg)


def _gate_columns(wg):
    idx_i = [2 * d * HEADS + h for d in range(2) for h in range(HEADS)]
    idx_f = [(2 * d + 1) * HEADS + h for d in range(2) for h in range(HEADS)]
    pad = [(0, 0)] * (wg.ndim - 1) + [(0, _GATE_LANES - _N_STATES)]
    return jnp.concatenate([jnp.pad(wg[..., jnp.array(idx)], pad) for idx in (idx_i, idx_f)], axis=-1)


def _tile_size(t, want):
    tm = min(want, t)
    assert t % tm == 0 and tm % _L == 0, (t, tm)
    return tm


def kernel(x, c, ctx, c_ctx, w_mod, b_mod, norm1_g, w_in, b_gate, conv_qk, head_norm_g, sgu_ln_g, sgu_ln_b, w_s, b_s,
           w_branch_mlstm, w_branch_sgu, w_out, norm2_g, w_up, w_ffn_conv, w_down, final_g):
    assert x.shape[0] == 1 and w_mod.shape[0] == 1, "single batch element, single layer"
    d = D_MODEL
    x2, ctx2 = x[0], ctx[0]
    t = x2.shape[0]
    assert t % GRID_W == 0 and ctx2.shape[0] % _L == 0
    row = lambda a: a.reshape(1, -1)

    cc = jnp.zeros((d, 8), F32).at[:, 0].set(c[0]).at[:, 1].set(c_ctx)
    mod = _modulation(cc, w_mod[0], row(b_mod[0]))
    mx = [mod[0:1, j * d:(j + 1) * d] for j in range(N_MOD)]
    mc = [mod[1:2, j * d:(j + 1) * d] for j in range(N_MOD)]

    w_in_t = jnp.swapaxes(w_in[0], 0, 1)
    wqkv = _cast_columns(w_in_t, 0, _W_GATES)
    wg = _gate_weights(w_in_t)
    wrest = _cast_columns(w_in_t, _W_REST, w_in_t.shape[0] - _W_REST)
    bg = _gate_columns(row(b_gate[0]))
    eye2 = jnp.eye(2, dtype=F32)
    wsd = jnp.stack([jnp.kron(eye2, w_s[0, g]) for g in range(HEADS)]).astype(BF16)
    bs2 = jnp.tile(b_s[0].T, (2, 1))
    proj_w = (row(norm1_g[0]), wqkv, wg, wrest, bg, conv_qk[0],
              row(sgu_ln_g[0]), row(sgu_ln_b[0]), wsd, bs2, w_branch_sgu[0].astype(BF16))

    pc = _projection(ctx2, mc[0], mc[1], *proj_w, tm=_tile_size(ctx2.shape[0], 512))
    c0 = jnp.zeros((_N_STATES, HEAD_DIM, _C_EXT), F32)
    m0 = jnp.full((1, _GATE_LANES), M_INIT, F32)
    _, _, c1, m1 = _mlstm(*pc[:6], c0, m0)

    q, kt, v, tab, gr, gm, so, sgm, s = _projection(x2, mx[0], mx[1], *proj_w, tm=_tile_size(t, 512))
    hf, hb, _, _ = _mlstm(q, kt, v, tab, gr, gm, c1, m1)
    hx, hn2 = _post(hf, hb, so, sgm, s, x2, row(head_norm_g[0]), w_branch_mlstm[0].astype(BF16),
                    w_out[0].astype(BF16), mx[2], row(norm2_g[0]), mx[3], mx[4], tm=_tile_size(t, 1024))

    out = _ffn(hn2, hx, _regroup_up(w_up[0]), w_ffn_conv[0].reshape(9, D_FF), w_down[0].astype(BF16), mx[5],
               row(final_g), tm=_tile_size(t, 1024))
    return out[None]
```

```python
import functools

import jax
import jax.numpy as jnp
from jax import lax
from jax.experimental import pallas as pl
from jax.experimental.pallas import tpu as pltpu

F32 = jnp.float32
BF16 = jnp.bfloat16

D_MODEL = 1024
GRID_W = 64
HEADS = 4
HEAD_DIM = 256
N_MOD = 6
D_FF = 2816
EPS = 1e-6
M_INIT = -1e30

_L = 256
_HALO = 16
_FF_CHUNK = 256
_N_FF_CHUNKS = D_FF // _FF_CHUNK
_GATE_LANES = 128
_PAD = 8
_PROJ_BLOCK = 256
_FFN_PIECES = 3
_FFN_DOWN_PIECES = 2
_POST_PIECES = 2
_MLSTM_CHUNKS_PER_STEP = 4
_VMEM_LIMIT_V7X = 56 * 1024 * 1024

_N_STATES = 2 * HEADS
_TAB_W = 3 * _GATE_LANES
_C_EXT = HEAD_DIM + _GATE_LANES
_N_GATES = 4 * HEADS
_W_GATES = 3 * D_MODEL
_W_REST = _W_GATES + _N_GATES
_SUBLANES = 8
_PREP_BLOCK = 512


def _dot(a, b):
    return jnp.dot(a, b, preferred_element_type=F32)


def _split3(x):
    h1 = x.astype(BF16)
    r1 = x - h1.astype(F32)
    h2 = r1.astype(BF16)
    h3 = (r1 - h2.astype(F32)).astype(BF16)
    return h1, h2, h3


def _sigmoid(x):
    return 0.5 * jnp.tanh(0.5 * x) + 0.5


def _silu(x):
    h = 0.5 * x
    return h * jnp.tanh(h) + h


def _gelu_tanh(x):
    return 0.5 * x * (1.0 + jnp.tanh(0.7978845608028654 * (x + 0.044715 * (x * x * x))))


def _log_sigmoid(x):
    return jnp.minimum(x, 0.0) - jnp.log(1.0 + jnp.exp(-jnp.abs(x)))


def _rms(x):
    return x * lax.rsqrt(jnp.mean(x * x, axis=-1, keepdims=True) + EPS)


def _const_spec(shape):
    nd = len(shape)
    return pl.BlockSpec(shape, lambda *_: (0,) * nd, pipeline_mode=pl.Buffered(1))


def _cast_kernel(w_ref, o_ref):
    o_ref[...] = w_ref[...].astype(BF16)


def _transpose_cast_kernel(w_ref, o_ref):
    o_ref[...] = w_ref[...].T.astype(BF16)


def _shift_transpose_cast_kernel(w_ref, next_ref, o_ref, *, shift):
    x = jnp.concatenate([w_ref[...], next_ref[...]], axis=0)
    o_ref[...] = x[shift:shift + w_ref.shape[0], :].T.astype(BF16)


def _cast_columns(w_t, col0, ncols):
    rows = w_t.shape[1]
    shift = col0 % _PREP_BLOCK
    base = col0 - shift
    assert shift % _SUBLANES == 0 and ncols % _PREP_BLOCK == 0 and col0 + ncols <= w_t.shape[0]
    b0 = base // _PREP_BLOCK
    in_specs = [pl.BlockSpec((_PREP_BLOCK, rows), lambda k: (b0 + k, 0))]
    operands = [w_t]
    body = _transpose_cast_kernel
    if shift:
        assert _PREP_BLOCK % shift == 0
        per_block = _PREP_BLOCK // shift
        in_specs.append(pl.BlockSpec((shift, rows), lambda k: ((b0 + k + 1) * per_block, 0)))
        operands.append(w_t)
        body = functools.partial(_shift_transpose_cast_kernel, shift=shift)
    return pl.pallas_call(
        body,
        grid=(ncols // _PREP_BLOCK,),
        in_specs=in_specs,
        out_specs=pl.BlockSpec((rows, _PREP_BLOCK), lambda k: (0, k)),
        out_shape=jax.ShapeDtypeStruct((rows, ncols), BF16),
        compiler_params=pltpu.CompilerParams(dimension_semantics=("parallel",)),
        name="cast_columns",
    )(*operands)


def _regroup_up(w_up):
    rows = w_up.shape[0]
    return pl.pallas_call(
        _cast_kernel,
        grid=(2 * _N_FF_CHUNKS,),
        in_specs=[pl.BlockSpec((rows, _FF_CHUNK), lambda j: (0, (j % 2) * _N_FF_CHUNKS + j // 2))],
        out_specs=pl.BlockSpec((rows, _FF_CHUNK), lambda j: (0, j)),
        out_shape=jax.ShapeDtypeStruct((rows, 2 * D_FF), BF16),
        compiler_params=pltpu.CompilerParams(dimension_semantics=("parallel",)),
        name="regroup_up",
    )(w_up)


def _gate_cast_kernel(w_ref, sel_ref, o_ref):
    o_ref[...] = lax.dot_general(w_ref[...].astype(BF16), sel_ref[...], (((0,), (0,)), ((), ())),
                                 preferred_element_type=F32).astype(BF16)


def _gate_selector():
    sel = [[0.0] * (2 * _GATE_LANES) for _ in range(_N_GATES)]
    for d in range(2):
        for h in range(HEADS):
            j = d * HEADS + h
            sel[2 * d * HEADS + h][j] = 1.0
            sel[(2 * d + 1) * HEADS + h][_GATE_LANES + j] = 1.0
    return jnp.array(sel, BF16)


def _gate_weights(w_t):
    rows = w_t.shape[1]
    assert _W_GATES % _N_GATES == 0
    return pl.pallas_call(
        _gate_cast_kernel,
        grid=(1,),
        in_specs=[pl.BlockSpec((_N_GATES, rows), lambda k: (_W_GATES // _N_GATES, 0)),
                  pl.BlockSpec((_N_GATES, 2 * _GATE_LANES), lambda k: (0, 0))],
        out_specs=pl.BlockSpec((rows, 2 * _GATE_LANES), lambda k: (0, 0)),
        out_shape=jax.ShapeDtypeStruct((rows, 2 * _GATE_LANES), BF16),
        name="gate_weights",
    )(w_t, _gate_selector())


def _mod_kernel(cc_ref, w_ref, b_ref, o_ref):
    s = _silu(cc_ref[...])
    w = w_ref[...]
    rows = [jnp.sum(w * s[:, r:r + 1], axis=0, keepdims=True) + b_ref[...] for r in range(2)]
    o_ref[...] = jnp.concatenate(rows + [jnp.zeros((6, w.shape[1]), F32)], axis=0)


def _modulation(cc, w_mod, b_mod):
    n_out = w_mod.shape[1]
    nb = D_MODEL
    return pl.pallas_call(
        _mod_kernel,
        grid=(n_out // nb,),
        in_specs=[pl.BlockSpec((D_MODEL, 8), lambda j: (0, 0)),
                  pl.BlockSpec((D_MODEL, nb), lambda j: (0, j)),
                  pl.BlockSpec((1, nb), lambda j: (0, j))],
        out_specs=pl.BlockSpec((8, nb), lambda j: (0, j)),
        out_shape=jax.ShapeDtypeStruct((8, n_out), F32),
        compiler_params=pltpu.CompilerParams(dimension_semantics=("arbitrary",)),
        name="modulation",
    )(cc, w_mod, b_mod)


def _gate_tables(gi, lf):
    row = lax.broadcasted_iota(jnp.int32, (_L, _L), 0)
    col = lax.broadcasted_iota(jnp.int32, (_L, _L), 1)
    ltri = (col <= row).astype(BF16)
    l1, l2, l3 = _split3(lf)
    prefix = _dot(ltri, l3) + _dot(ltri, l2) + _dot(ltri, l1)
    total = prefix[_L - 1:_L, :]
    suffix = total - prefix + lf
    lane = lax.broadcasted_iota(jnp.int32, (1, _GATE_LANES), 1)
    cum = jnp.where(lane >= HEADS, suffix, prefix)
    a = total - cum + gi
    c = gi - cum
    c_rows = c.T[0:_N_STATES, :]
    a_rows = a.T[0:_N_STATES, :]
    pos = lax.broadcasted_iota(jnp.int32, (_N_STATES, _L), 1)
    fwd_max, bwd_max = c_rows, c_rows
    k = 1
    while k < _L:
        fwd_max = jnp.maximum(fwd_max, jnp.where(pos >= k, pltpu.roll(fwd_max, k, 1), -jnp.inf))
        bwd_max = jnp.maximum(bwd_max, jnp.where(pos < _L - k, pltpu.roll(bwd_max, _L - k, 1), -jnp.inf))
        k *= 2
    state = lax.broadcasted_iota(jnp.int32, (_N_STATES, _L), 0)
    run_max = jnp.where(state < HEADS, fwd_max, bwd_max)
    run_max_cols = jnp.concatenate([run_max, jnp.zeros((_GATE_LANES - _N_STATES, _L), F32)], axis=0).T
    return cum, cum + run_max_cols, a, c_rows, a_rows, total, jnp.max(a, axis=0, keepdims=True)


def _proj_kernel(x_ref, xp_ref, xn_ref, shift_ref, scale_ref, g1_ref, wqkv_ref, wg_ref, wrest_ref, bg_ref,
                 cw_ref, lng_ref, lnb_ref, wsd_ref, bs_ref, wbs_ref,
                 q_ref, kt_ref, v_ref, tab_ref, gr_ref, gm_ref, so_ref, sgm_ref, s_ref,
                 hn_s, z_s, ys_s, *, tm):
    i = pl.program_id(0)
    has_prev = i > 0
    has_next = i < pl.num_programs(0) - 1
    scale1 = 1.0 + scale_ref[...]

    def norm_mod(xb):
        return ((_rms(xb) * g1_ref[...]) * scale1 + shift_ref[...]).astype(BF16)

    hn_s[0:_HALO, :] = norm_mod(xp_ref[...])
    hn_s[_HALO:_HALO + tm, :] = norm_mod(x_ref[...])
    hn_s[_HALO + tm:, :] = norm_mod(xn_ref[...])

    bw = _PROJ_BLOCK
    n_sub = tm // _L

    def qkv_cols(seg, b):
        return wqkv_ref[:, seg * D_MODEL + b * bw:seg * D_MODEL + (b + 1) * bw]

    def rest_cols(seg, b):
        return wrest_ref[:, seg * D_MODEL + b * bw:seg * D_MODEL + (b + 1) * bw]

    def first_stage(sub, between):
        r0 = sub * _L
        rows = slice(r0, r0 + _L)
        hn_ext = hn_s[r0:r0 + _L + 2 * _HALO, :]
        hn = hn_s[_HALO + r0:_HALO + r0 + _L, :]
        u_blocks, vg_blocks = [], []

        def qk_block(c, b):
            lanes = slice(b * bw, (b + 1) * bw)
            z = _dot(hn_ext, qkv_cols(c, b))
            top, bottom = z[0:_HALO, :], z[_HALO + _L:, :]
            if sub == 0:
                top = jnp.where(has_prev, top, 0.0)
            if sub == n_sub - 1:
                bottom = jnp.where(has_next, bottom, 0.0)
            z_s[c, sub, 0:_HALO, lanes] = top
            z_s[c, sub, _HALO:_HALO + _L, lanes] = z[_HALO:_HALO + _L, :]
            z_s[c, sub, _HALO + _L:, lanes] = bottom
            cw = cw_ref[:, c * D_MODEL + b * bw:c * D_MODEL + (b + 1) * bw]
            y = (cw[0:1, :] * z_s[c, sub, _HALO - 1:_HALO - 1 + _L, lanes]
                 + cw[1:2, :] * z_s[c, sub, _HALO:_HALO + _L, lanes]
                 + cw[2:3, :] * z_s[c, sub, _HALO + 1:_HALO + 1 + _L, lanes])
            y = _silu(y)
            if c == 0:
                q_ref[rows, lanes] = (y * (HEAD_DIM ** -0.5)).astype(BF16)
            else:
                kt_ref[lanes, rows] = y.T.astype(BF16)

        n_blocks = D_MODEL // bw
        per_block = -(-len(between) // n_blocks)
        for b in range(n_blocks):
            lanes = slice(b * bw, (b + 1) * bw)
            qk_block(0, b)
            v_ref[rows, lanes] = _dot(hn, qkv_cols(2, b)).astype(BF16)
            qk_block(1, b)
            so_ref[rows, lanes] = _sigmoid(_dot(hn, rest_cols(0, b))).astype(BF16)
            u_blocks.append(_gelu_tanh(_dot(hn, rest_cols(1, b))))
            sgm_ref[rows, lanes] = _sigmoid(_dot(hn, rest_cols(3, b))).astype(BF16)
            vg_blocks.append(_gelu_tanh(_dot(hn, rest_cols(2, b))))
            for piece in between[b * per_block:(b + 1) * per_block]:
                piece()
        gates = _dot(hn, wg_ref[...]) + bg_ref[...]
        return hn, gates, u_blocks, vg_blocks

    def second_stage(sub, hn, gates, u_blocks, vg_blocks):
        rows = slice(sub * _L, (sub + 1) * _L)
        st = {}

        def tables():
            cum, mi, a, c_rows, a_rows, total, a_max = _gate_tables(gates[:, 0:_GATE_LANES],
                                                                    _log_sigmoid(gates[:, _GATE_LANES:]))
            tab_ref[rows, 0:_GATE_LANES] = cum
            tab_ref[rows, _GATE_LANES:2 * _GATE_LANES] = mi
            tab_ref[rows, 2 * _GATE_LANES:] = a
            gr_ref[0:_N_STATES, rows] = c_rows
            gr_ref[_N_STATES:, rows] = a_rows
            gm_ref[sub * 8:(sub + 1) * 8, :] = jnp.concatenate([total, a_max, jnp.zeros((6, _GATE_LANES), F32)], axis=0)

        def layer_norm():
            vg = jnp.concatenate(vg_blocks, axis=1)
            vg = vg - jnp.mean(vg, axis=-1, keepdims=True)
            st["vn"] = (vg * lax.rsqrt(jnp.mean(vg * vg, axis=-1, keepdims=True) + EPS) * lng_ref[...]
                        + lnb_ref[...]).astype(BF16)

        def mix(g):
            cols = slice(g * HEAD_DIM, (g + 1) * HEAD_DIM)
            mixed = _dot(wsd_ref[g], st["vn"][:, cols]) + bs_ref[:, g:g + 1]
            ys_s[rows, cols] = (u_blocks[g] * mixed).astype(BF16)

        def merge():
            gg = _sigmoid(_dot(hn, wrest_ref[:, 4 * D_MODEL:5 * D_MODEL]))
            s_ref[rows, :] = (gg * _dot(ys_s[rows, :], wbs_ref[...])).astype(BF16)

        return [tables, layer_norm] + [functools.partial(mix, g) for g in range(HEADS)] + [merge]

    assert bw == HEAD_DIM
    pending = []
    for sub in range(n_sub):
        state = first_stage(sub, pending)
        pending = second_stage(sub, *state)
    for piece in pending:
        piece()


def _projection(x, shift, scale, g1, wqkv, wg, wrest, bg, cw, lng, lnb, wsd, bs2, wbs, *, tm):
    t = x.shape[0]
    nt = t // tm
    hb = tm // _HALO
    nhb = t // _HALO
    tile = lambda w: pl.BlockSpec((tm, w), lambda i: (i, 0))
    in_specs = [
        tile(D_MODEL),
        pl.BlockSpec((_HALO, D_MODEL), lambda i: (jnp.maximum(i * hb - 1, 0), 0)),
        pl.BlockSpec((_HALO, D_MODEL), lambda i: (jnp.minimum((i + 1) * hb, nhb - 1), 0)),
    ] + [_const_spec(a.shape) for a in (shift, scale, g1, wqkv, wg, wrest, bg, cw, lng, lnb, wsd, bs2, wbs)]
    gm_rows = 8 * (tm // _L)
    out_specs = [tile(D_MODEL), pl.BlockSpec((D_MODEL, tm), lambda i: (0, i)), tile(D_MODEL), tile(_TAB_W),
                 pl.BlockSpec((2 * _N_STATES, tm), lambda i: (0, i)),
                 pl.BlockSpec((gm_rows, _GATE_LANES), lambda i: (i, 0)),
                 tile(D_MODEL), tile(D_MODEL), tile(D_MODEL)]
    act = jax.ShapeDtypeStruct((t, D_MODEL), BF16)
    out_shape = [act, jax.ShapeDtypeStruct((D_MODEL, t), BF16), act, jax.ShapeDtypeStruct((t, _TAB_W), F32),
                 jax.ShapeDtypeStruct((2 * _N_STATES, t), F32), jax.ShapeDtypeStruct((8 * (t // _L), _GATE_LANES), F32),
                 act, act, act]
    return pl.pallas_call(
        functools.partial(_proj_kernel, tm=tm),
        grid=(nt,),
        in_specs=in_specs,
        out_specs=out_specs,
        out_shape=out_shape,
        scratch_shapes=[pltpu.VMEM((tm + 2 * _HALO, D_MODEL), BF16),
                        pltpu.VMEM((2, tm // _L, _L + 2 * _HALO, D_MODEL), F32),
                        pltpu.VMEM((tm, D_MODEL), BF16)],
        compiler_params=pltpu.CompilerParams(dimension_semantics=("parallel",), vmem_limit_bytes=_VMEM_LIMIT_V7X),
        name="projection",
    )(x, x, x, shift, scale, g1, wqkv, wg, wrest, bg, cw, lng, lnb, wsd, bs2, wbs)


def _mlstm_kernel(qf_ref, ktf_ref, vf_ref, tabf_ref, grf_ref, gmf_ref, qb_ref, ktb_ref, vb_ref, tabb_ref, grb_ref, gmb_ref,
                  c0_ref, m0_ref, hf_ref, hb_ref, c_ref, m_ref, *, per_step):
    @pl.when(pl.program_id(0) == 0)
    def _():
        c_ref[...] = c0_ref[...]
        m_ref[...] = m0_ref[...]

    row = lax.broadcasted_iota(jnp.int32, (_L, _L), 0)
    col = lax.broadcasted_iota(jnp.int32, (_L, _L), 1)
    ones = jnp.ones((_L, _GATE_LANES), BF16)
    lane = lax.broadcasted_iota(jnp.int32, (1, _GATE_LANES), 1)
    dirs = ((qf_ref, ktf_ref, vf_ref, tabf_ref, grf_ref, gmf_ref, hf_ref, col <= row),
            (qb_ref, ktb_ref, vb_ref, tabb_ref, grb_ref, gmb_ref, hb_ref, col >= row))
    for sub in range(per_step):
        m_prev = m_ref[...]
        m_next = []
        for d, (q_ref, kt_ref, v_ref, tab_ref, gr_ref, gm_ref, h_ref, mask) in enumerate(dirs):
            k = sub if d == 0 else per_step - 1 - sub
            rows = slice(k * _L, (k + 1) * _L)
            cum = tab_ref[rows, 0:_GATE_LANES]
            m_row = jnp.maximum(cum + m_prev, tab_ref[rows, _GATE_LANES:2 * _GATE_LANES])
            e = cum - m_row
            total, a_max = gm_ref[8 * k:8 * k + 1, :], gm_ref[8 * k + 1:8 * k + 2, :]
            m_new = jnp.maximum(total + m_prev, a_max)
            a_old = jnp.exp(total + m_prev - m_new)
            m_next.append(m_new)
            gr = gr_ref[:, rows]
            for hd in range(HEADS):
                j = d * HEADS + hd
                cols = slice(hd * HEAD_DIM, (hd + 1) * HEAD_DIM)
                qh, kth = q_ref[rows, cols], kt_ref[cols, rows]
                v_ext = jnp.concatenate([v_ref[rows, cols], ones], axis=1)
                e_b = jnp.broadcast_to(e[:, j:j + 1], (_L, _GATE_LANES))
                m_row_b = jnp.broadcast_to(m_row[:, j:j + 1], (_L, _GATE_LANES))
                c_row, a_row = gr[j:j + 1, :], gr[_N_STATES + j:_N_STATES + j + 1, :]
                p = jnp.exp(jnp.where(mask, jnp.concatenate([e_b, e_b], axis=1) + c_row, -jnp.inf))
                s = (_dot(qh, kth) * p).astype(BF16)
                c_prev = c_ref[j]
                w_inter = jnp.exp(e_b + m_prev[:, j:j + 1])
                num = _dot(s, v_ext) + jnp.concatenate([w_inter] * 3, axis=1) * _dot(qh, c_prev.astype(BF16))
                inv = 1.0 / jnp.maximum(jnp.abs(num[:, HEAD_DIM:]), jnp.exp(-m_row_b))
                h_ref[rows, cols] = (num[:, :HEAD_DIM] * jnp.concatenate([inv, inv], axis=1)).astype(BF16)
                kw_t = (kth.astype(F32) * jnp.exp(a_row - m_new[:, j:j + 1])).astype(BF16)
                c_ref[j] = a_old[:, j:j + 1] * c_prev + _dot(kw_t, v_ext)
        m_ref[...] = jnp.where(lane < HEADS, m_next[0], m_next[1])


def _mlstm(q, kt, v, tab, gr, gm, c0, m0):
    t = q.shape[0]
    per_step = min(_MLSTM_CHUNKS_PER_STEP, t // _L)
    blk = per_step * _L
    assert t % blk == 0
    ns = t // blk
    fwd_i = lambda i: i
    bwd_i = lambda i: ns - 1 - i

    def specs(ix):
        rows = lambda w: pl.BlockSpec((blk, w), lambda i: (ix(i), 0))
        return [rows(D_MODEL), pl.BlockSpec((D_MODEL, blk), lambda i: (0, ix(i))), rows(D_MODEL), rows(_TAB_W),
                pl.BlockSpec((2 * _N_STATES, blk), lambda i: (0, ix(i))),
                pl.BlockSpec((8 * per_step, _GATE_LANES), lambda i: (ix(i), 0))]

    out_rows = lambda ix: pl.BlockSpec((blk, D_MODEL), lambda i: (ix(i), 0))
    state_specs = [pl.BlockSpec(a.shape, lambda i, nd=a.ndim: (0,) * nd) for a in (c0, m0)]
    act = jax.ShapeDtypeStruct((t, D_MODEL), BF16)
    return pl.pallas_call(
        functools.partial(_mlstm_kernel, per_step=per_step),
        grid=(ns,),
        in_specs=specs(fwd_i) + specs(bwd_i) + [_const_spec(c0.shape), _const_spec(m0.shape)],
        out_specs=[out_rows(fwd_i), out_rows(bwd_i)] + state_specs,
        out_shape=[act, act] + [jax.ShapeDtypeStruct(a.shape, F32) for a in (c0, m0)],
        compiler_params=pltpu.CompilerParams(dimension_semantics=("arbitrary",), vmem_limit_bytes=_VMEM_LIMIT_V7X),
        name="mlstm",
    )(q, kt, v, tab, gr, gm, q, kt, v, tab, gr, gm, c0, m0)


def _post_kernel(hf_ref, hb_ref, so_ref, sgm_ref, s_ref, x_ref, hg_ref, wbm_ref, wout_ref, gate_ref,
                 g2_ref, shift_ref, scale_ref, hx_ref, hn2_ref, ym_s, y_s):
    tm = x_ref.shape[0]
    edges = tuple(range(0, tm + 1, tm // _POST_PIECES))

    def gate(p):
        rows = slice(edges[p], edges[p + 1])
        hm = hf_ref[rows, :].astype(F32) + hb_ref[rows, :].astype(F32)
        for hd in range(HEADS):
            cols = slice(hd * HEAD_DIM, (hd + 1) * HEAD_DIM)
            ym = _rms(hm[:, cols]) * hg_ref[:, cols]
            ym_s[rows, cols] = (so_ref[rows, cols].astype(F32) * ym).astype(BF16)

    def merge(p):
        rows = slice(edges[p], edges[p + 1])
        y = sgm_ref[rows, :].astype(F32) * _dot(ym_s[rows, :], wbm_ref[...]) + s_ref[rows, :].astype(F32)
        y_s[rows, :] = y.astype(BF16)

    def out(p):
        rows = slice(edges[p], edges[p + 1])
        hx = x_ref[rows, :] + gate_ref[...] * _dot(y_s[rows, :], wout_ref[...])
        hx_ref[rows, :] = hx
        hn2_ref[rows, :] = ((_rms(hx) * g2_ref[...]) * (1.0 + scale_ref[...]) + shift_ref[...]).astype(BF16)

    gate(0)
    for p in range(_POST_PIECES):
        if p + 1 < _POST_PIECES:
            gate(p + 1)
        merge(p)
        if p >= 1:
            out(p - 1)
    out(_POST_PIECES - 1)


def _post(hf, hb, so, sgm, s, x, hg, wbm, wout, gate, g2, shift, scale, *, tm):
    t = x.shape[0]
    tile = lambda: pl.BlockSpec((tm, D_MODEL), lambda i: (i, 0))
    return pl.pallas_call(
        _post_kernel,
        grid=(t // tm,),
        in_specs=[tile() for _ in range(6)] + [_const_spec(a.shape) for a in (hg, wbm, wout, gate, g2, shift, scale)],
        out_specs=[tile(), tile()],
        out_shape=[jax.ShapeDtypeStruct((t, D_MODEL), F32), jax.ShapeDtypeStruct((t, D_MODEL), BF16)],
        scratch_shapes=[pltpu.VMEM((tm, D_MODEL), BF16), pltpu.VMEM((tm, D_MODEL), BF16)],
        compiler_params=pltpu.CompilerParams(dimension_semantics=("parallel",), vmem_limit_bytes=_VMEM_LIMIT_V7X),
        name="post_mixer",
    )(hf, hb, so, sgm, s, x, hg, wbm, wout, gate, g2, shift, scale)


def _ffn_kernel(hn_ref, hnp_ref, hnn_ref, hx_ref, wab_ref, cw_ref, wdn_ref, gate_ref, fg_ref, o_ref,
                hne_s, a_s, b_s, act_s, acc_s, *, tm):
    i = pl.program_id(0)
    has_above = i > 0
    has_below = i < pl.num_programs(0) - 1
    ext = tm + 2 * GRID_W
    hne_s[0:GRID_W, :] = hnp_ref[...]
    hne_s[GRID_W:GRID_W + tm, :] = hn_ref[...]
    hne_s[GRID_W + tm:, :] = hnn_ref[...]
    for slot in range(2):
        a_s[slot, 0:_PAD, :] = jnp.zeros((_PAD, _FF_CHUNK), F32)
        a_s[slot, _PAD + ext:, :] = jnp.zeros((_PAD, _FF_CHUNK), F32)
    up_rows = ext // _FFN_PIECES
    mix_edges = tuple(min(j * up_rows, tm) for j in range(_FFN_PIECES + 1))
    down_edges = tuple(range(0, tm + 1, tm // _FFN_DOWN_PIECES))

    def up_piece(c, j):
        slot = c % 2
        m0, m1 = j * up_rows, (j + 1) * up_rows
        ab = _dot(hne_s[m0:m1, :], wab_ref[:, 2 * c * _FF_CHUNK:2 * (c + 1) * _FF_CHUNK])
        a = ab[:, :_FF_CHUNK]
        lo, hi = max(m0, GRID_W), min(m1, GRID_W + tm)
        if m0 < GRID_W:
            a_s[slot, _PAD + m0:_PAD + GRID_W, :] = jnp.where(has_above, a[:GRID_W - m0, :], 0.0)
        a_s[slot, _PAD + lo:_PAD + hi, :] = a[lo - m0:hi - m0, :]
        if m1 > GRID_W + tm:
            a_s[slot, _PAD + GRID_W + tm:_PAD + m1, :] = jnp.where(has_below, a[GRID_W + tm - m0:, :], 0.0)
        b_s[slot, lo - GRID_W:hi - GRID_W, :] = ab[lo - m0:hi - m0, _FF_CHUNK:]

    def mix_piece(c, p):
        slot = c % 2
        r0, r1 = mix_edges[p], mix_edges[p + 1]
        n = r1 - r0 + 2 * GRID_W
        gcol = lax.broadcasted_iota(jnp.int32, (n, 1), 0) % GRID_W
        taps = (jnp.where(gcol >= 1, a_s[slot, _PAD - 1 + r0:_PAD - 1 + r0 + n, :], 0.0),
                a_s[slot, _PAD + r0:_PAD + r0 + n, :],
                jnp.where(gcol <= GRID_W - 2, a_s[slot, _PAD + 1 + r0:_PAD + 1 + r0 + n, :], 0.0))
        cw = cw_ref[:, c * _FF_CHUNK:(c + 1) * _FF_CHUNK]
        conv = None
        for dr in range(3):
            for dc in range(3):
                term = cw[3 * dr + dc:3 * dr + dc + 1, :] * taps[dc][dr * GRID_W:dr * GRID_W + r1 - r0, :]
                conv = term if conv is None else conv + term
        act_s[slot, r0:r1, :] = (_silu(conv) * b_s[slot, r0:r1, :]).astype(BF16)

    def down_piece(c, p):
        r0, r1 = down_edges[p], down_edges[p + 1]
        part = _dot(act_s[c % 2, r0:r1, :], wdn_ref[c * _FF_CHUNK:(c + 1) * _FF_CHUNK, :])
        if c == 0:
            acc_s[r0:r1, :] = part
        else:
            acc_s[r0:r1, :] += part

    for step in range(_N_FF_CHUNKS + 2):
        for j in range(_FFN_PIECES):
            if step < _N_FF_CHUNKS:
                up_piece(step, j)
            if 0 <= step - 1 < _N_FF_CHUNKS:
                mix_piece(step - 1, j)
        if 0 <= step - 2:
            for p in range(_FFN_DOWN_PIECES):
                down_piece(step - 2, p)
    h2 = hx_ref[...] + gate_ref[...] * acc_s[...]
    o_ref[...] = _rms(h2) * fg_ref[...]


def _ffn(hn2, hx, wab, cwf, wdn, gate, fg, *, tm):
    t = hx.shape[0]
    rb = tm // GRID_W
    nrb = t // GRID_W
    tile = lambda: pl.BlockSpec((tm, D_MODEL), lambda i: (i, 0))
    ext = tm + 2 * GRID_W
    return pl.pallas_call(
        functools.partial(_ffn_kernel, tm=tm),
        grid=(t // tm,),
        in_specs=[tile(),
                  pl.BlockSpec((GRID_W, D_MODEL), lambda i: (jnp.maximum(i * rb - 1, 0), 0)),
                  pl.BlockSpec((GRID_W, D_MODEL), lambda i: (jnp.minimum((i + 1) * rb, nrb - 1), 0)),
                  tile()] + [_const_spec(a.shape) for a in (wab, cwf, wdn, gate, fg)],
        out_specs=tile(),
        out_shape=jax.ShapeDtypeStruct((t, D_MODEL), F32),
        scratch_shapes=[pltpu.VMEM((ext, D_MODEL), BF16),
                        pltpu.VMEM((2, ext + 2 * _PAD, _FF_CHUNK), F32),
                        pltpu.VMEM((2, tm, _FF_CHUNK), F32),
                        pltpu.VMEM((2, tm, _FF_CHUNK), BF16),
                        pltpu.VMEM((tm, D_MODEL), F32)],
        compiler_params=pltpu.CompilerParams(dimension_semantics=("parallel",), vmem_limit_bytes=_VMEM_LIMIT_V7X),
        name="conv_ffn",
    )(hn2, hn2, hn2, hx, wab, cwf, wdn, gate, fg)


def _gate_columns(wg):
    idx_i = [2 * d * HEADS + h for d in range(2) for h in range(HEADS)]
    idx_f = [(2 * d + 1) * HEADS + h for d in range(2) for h in range(HEADS)]
    pad = [(0, 0)] * (wg.ndim - 1) + [(0, _GATE_LANES - _N_STATES)]
    return jnp.concatenate([jnp.pad(wg[..., jnp.array(idx)], pad) for idx in (idx_i, idx_f)], axis=-1)


def _tile_size(t, want):
    tm = min(want, t)
    assert t % tm == 0 and tm % _L == 0, (t, tm)
    return tm


def kernel(x, c, ctx, c_ctx, w_mod, b_mod, norm1_g, w_in, b_gate, conv_qk, head_norm_g, sgu_ln_g, sgu_ln_b, w_s, b_s,
           w_branch_mlstm, w_branch_sgu, w_out, norm2_g, w_up, w_ffn_conv, w_down, final_g):
    assert x.shape[0] == 1 and w_mod.shape[0] == 1, "single batch element, single layer"
    d = D_MODEL
    x2, ctx2 = x[0], ctx[0]
    t = x2.shape[0]
    assert t % GRID_W == 0 and ctx2.shape[0] % _L == 0
    row = lambda a: a.reshape(1, -1)

    cc = jnp.zeros((d, 8), F32).at[:, 0].set(c[0]).at[:, 1].set(c_ctx)
    mod = _modulation(cc, w_mod[0], row(b_mod[0]))
    mx = [mod[0:1, j * d:(j + 1) * d] for j in range(N_MOD)]
    mc = [mod[1:2, j * d:(j + 1) * d] for j in range(N_MOD)]

    w_in_t = jnp.swapaxes(w_in[0], 0, 1)
    wqkv = _cast_columns(w_in_t, 0, _W_GATES)
    wg = _gate_weights(w_in_t)
    wrest = _cast_columns(w_in_t, _W_REST, w_in_t.shape[0] - _W_REST)
    bg = _gate_columns(row(b_gate[0]))
    eye2 = jnp.eye(2, dtype=F32)
    wsd = jnp.stack([jnp.kron(eye2, w_s[0, g]) for g in range(HEADS)]).astype(BF16)
    bs2 = jnp.tile(b_s[0].T, (2, 1))
    proj_w = (row(norm1_g[0]), wqkv, wg, wrest, bg, conv_qk[0],
              row(sgu_ln_g[0]), row(sgu_ln_b[0]), wsd, bs2, w_branch_sgu[0].astype(BF16))

    pc = _projection(ctx2, mc[0], mc[1], *proj_w, tm=_tile_size(ctx2.shape[0], 512))
    c0 = jnp.zeros((_N_STATES, HEAD_DIM, _C_EXT), F32)
    m0 = jnp.full((1, _GATE_LANES), M_INIT, F32)
    _, _, c1, m1 = _mlstm(*pc[:6], c0, m0)

    q, kt, v, tab, gr, gm, so, sgm, s = _projection(x2, mx[0], mx[1], *proj_w, tm=_tile_size(t, 512))
    hf, hb, _, _ = _mlstm(q, kt, v, tab, gr, gm, c1, m1)
    hx, hn2 = _post(hf, hb, so, sgm, s, x2, row(head_norm_g[0]), w_branch_mlstm[0].astype(BF16),
                    w_out[0].astype(BF16), mx[2], row(norm2_g[0]), mx[3], mx[4], tm=_tile_size(t, 1024))

    out = _ffn(hn2, hx, _regroup_up(w_up[0]), w_ffn_conv[0].reshape(9, D_FF), w_down[0].astype(BF16), mx[5],
               row(final_g), tm=_tile_size(t, 1024))
    return out[None]
```

```python
import functools

import jax
import jax.numpy as jnp
from jax import lax
from jax.experimental import pallas as pl
from jax.experimental.pallas import tpu as pltpu

F32 = jnp.float32
BF16 = jnp.bfloat16

D_MODEL = 1024
GRID_W = 64
HEADS = 4
HEAD_DIM = 256
N_MOD = 6
D_FF = 2816
EPS = 1e-6
M_INIT = -1e30

_L = 256
_HALO = 16
_FF_CHUNK = 256
_N_FF_CHUNKS = D_FF // _FF_CHUNK
_GATE_LANES = 128
_PAD = 8
_PROJ_BLOCK = 256
_FFN_PIECES = 3
_FFN_DOWN_PIECES = 2
_POST_PIECES = 2
_MLSTM_CHUNKS_PER_STEP = 4
_VMEM_LIMIT_V7X = 56 * 1024 * 1024

_N_STATES = 2 * HEADS
_TAB_W = 3 * _GATE_LANES
_C_EXT = HEAD_DIM + _GATE_LANES
_N_GATES = 4 * HEADS
_W_GATES = 3 * D_MODEL
_W_REST = _W_GATES + _N_GATES
_SUBLANES = 8
_PREP_BLOCK = 1024


def _dot(a, b):
    return jnp.dot(a, b, preferred_element_type=F32)


def _split3(x):
    h1 = x.astype(BF16)
    r1 = x - h1.astype(F32)
    h2 = r1.astype(BF16)
    h3 = (r1 - h2.astype(F32)).astype(BF16)
    return h1, h2, h3


def _sigmoid(x):
    return 0.5 * jnp.tanh(0.5 * x) + 0.5


def _silu(x):
    h = 0.5 * x
    return h * jnp.tanh(h) + h


def _gelu_tanh(x):
    return 0.5 * x * (1.0 + jnp.tanh(0.7978845608028654 * (x + 0.044715 * (x * x * x))))


def _log_sigmoid(x):
    return jnp.minimum(x, 0.0) - jnp.log(1.0 + jnp.exp(-jnp.abs(x)))


def _rms(x):
    return x * lax.rsqrt(jnp.mean(x * x, axis=-1, keepdims=True) + EPS)


def _const_spec(shape):
    nd = len(shape)
    return pl.BlockSpec(shape, lambda *_: (0,) * nd, pipeline_mode=pl.Buffered(1))


def _cast_kernel(w_ref, o_ref):
    o_ref[...] = w_ref[...].astype(BF16)


def _transpose_cast_kernel(w_ref, o_ref):
    o_ref[...] = w_ref[...].T.astype(BF16)


def _shift_transpose_cast_kernel(w_ref, next_ref, o_ref, *, shift):
    x = jnp.concatenate([w_ref[...], next_ref[...]], axis=0)
    o_ref[...] = x[shift:shift + w_ref.shape[0], :].T.astype(BF16)


def _cast_columns(w_t, col0, ncols):
    rows = w_t.shape[1]
    shift = col0 % _PREP_BLOCK
    base = col0 - shift
    assert shift % _SUBLANES == 0 and ncols % _PREP_BLOCK == 0 and col0 + ncols <= w_t.shape[0]
    b0 = base // _PREP_BLOCK
    in_specs = [pl.BlockSpec((_PREP_BLOCK, rows), lambda k: (b0 + k, 0))]
    operands = [w_t]
    body = _transpose_cast_kernel
    if shift:
        assert _PREP_BLOCK % shift == 0
        per_block = _PREP_BLOCK // shift
        in_specs.append(pl.BlockSpec((shift, rows), lambda k: ((b0 + k + 1) * per_block, 0)))
        operands.append(w_t)
        body = functools.partial(_shift_transpose_cast_kernel, shift=shift)
    return pl.pallas_call(
        body,
        grid=(ncols // _PREP_BLOCK,),
        in_specs=in_specs,
        out_specs=pl.BlockSpec((rows, _PREP_BLOCK), lambda k: (0, k)),
        out_shape=jax.ShapeDtypeStruct((rows, ncols), BF16),
        compiler_params=pltpu.CompilerParams(dimension_semantics=("parallel",)),
        name="cast_columns",
    )(*operands)


def _regroup_up(w_up):
    rows = w_up.shape[0]
    return pl.pallas_call(
        _cast_kernel,
        grid=(2 * _N_FF_CHUNKS,),
        in_specs=[pl.BlockSpec((rows, _FF_CHUNK), lambda j: (0, (j % 2) * _N_FF_CHUNKS + j // 2))],
        out_specs=pl.BlockSpec((rows, _FF_CHUNK), lambda j: (0, j)),
        out_shape=jax.ShapeDtypeStruct((rows, 2 * D_FF), BF16),
        compiler_params=pltpu.CompilerParams(dimension_semantics=("parallel",)),
        name="regroup_up",
    )(w_up)


def _gate_cast_kernel(w_ref, sel_ref, o_ref):
    o_ref[...] = lax.dot_general(w_ref[...].astype(BF16), sel_ref[...], (((0,), (0,)), ((), ())),
                                 preferred_element_type=F32).astype(BF16)


def _gate_selector():
    sel = [[0.0] * (2 * _GATE_LANES) for _ in range(_N_GATES)]
    for d in range(2):
        for h in range(HEADS):
            j = d * HEADS + h
            sel[2 * d * HEADS + h][j] = 1.0
            sel[(2 * d + 1) * HEADS + h][_GATE_LANES + j] = 1.0
    return jnp.array(sel, BF16)


def _gate_weights(w_t):
    rows = w_t.shape[1]
    assert _W_GATES % _N_GATES == 0
    return pl.pallas_call(
        _gate_cast_kernel,
        grid=(1,),
        in_specs=[pl.BlockSpec((_N_GATES, rows), lambda k: (_W_GATES // _N_GATES, 0)),
                  pl.BlockSpec((_N_GATES, 2 * _GATE_LANES), lambda k: (0, 0))],
        out_specs=pl.BlockSpec((rows, 2 * _GATE_LANES), lambda k: (0, 0)),
        out_shape=jax.ShapeDtypeStruct((rows, 2 * _GATE_LANES), BF16),
        name="gate_weights",
    )(w_t, _gate_selector())


def _mod_kernel(cc_ref, w_ref, b_ref, o_ref):
    s = _silu(cc_ref[...])
    w = w_ref[...]
    rows = [jnp.sum(w * s[:, r:r + 1], axis=0, keepdims=True) + b_ref[...] for r in range(2)]
    o_ref[...] = jnp.concatenate(rows + [jnp.zeros((6, w.shape[1]), F32)], axis=0)


def _modulation(cc, w_mod, b_mod):
    n_out = w_mod.shape[1]
    nb = 2 * D_MODEL
    return pl.pallas_call(
        _mod_kernel,
        grid=(n_out // nb,),
        in_specs=[pl.BlockSpec((D_MODEL, 8), lambda j: (0, 0)),
                  pl.BlockSpec((D_MODEL, nb), lambda j: (0, j)),
                  pl.BlockSpec((1, nb), lambda j: (0, j))],
        out_specs=pl.BlockSpec((8, nb), lambda j: (0, j)),
        out_shape=jax.ShapeDtypeStruct((8, n_out), F32),
        compiler_params=pltpu.CompilerParams(dimension_semantics=("arbitrary",)),
        name="modulation",
    )(cc, w_mod, b_mod)


def _gate_tables(gi, lf):
    row = lax.broadcasted_iota(jnp.int32, (_L, _L), 0)
    col = lax.broadcasted_iota(jnp.int32, (_L, _L), 1)
    ltri = (col <= row).astype(BF16)
    l1, l2, l3 = _split3(lf)
    prefix = _dot(ltri, l3) + _dot(ltri, l2) + _dot(ltri, l1)
    total = prefix[_L - 1:_L, :]
    suffix = total - prefix + lf
    lane = lax.broadcasted_iota(jnp.int32, (1, _GATE_LANES), 1)
    cum = jnp.where(lane >= HEADS, suffix, prefix)
    a = total - cum + gi
    c = gi - cum
    c_rows = c.T[0:_N_STATES, :]
    a_rows = a.T[0:_N_STATES, :]
    pos = lax.broadcasted_iota(jnp.int32, (_N_STATES, _L), 1)
    fwd_max, bwd_max = c_rows, c_rows
    k = 1
    while k < _L:
        fwd_max = jnp.maximum(fwd_max, jnp.where(pos >= k, pltpu.roll(fwd_max, k, 1), -jnp.inf))
        bwd_max = jnp.maximum(bwd_max, jnp.where(pos < _L - k, pltpu.roll(bwd_max, _L - k, 1), -jnp.inf))
        k *= 2
    state = lax.broadcasted_iota(jnp.int32, (_N_STATES, _L), 0)
    run_max = jnp.where(state < HEADS, fwd_max, bwd_max)
    run_max_cols = jnp.concatenate([run_max, jnp.zeros((_GATE_LANES - _N_STATES, _L), F32)], axis=0).T
    return cum, cum + run_max_cols, a, c_rows, a_rows, total, jnp.max(a, axis=0, keepdims=True)


def _proj_kernel(x_ref, xp_ref, xn_ref, shift_ref, scale_ref, g1_ref, wqkv_ref, wg_ref, wrest_ref, bg_ref,
                 cw_ref, lng_ref, lnb_ref, wsd_ref, bs_ref, wbs_ref,
                 q_ref, kt_ref, v_ref, tab_ref, gr_ref, gm_ref, so_ref, sgm_ref, s_ref,
                 hn_s, z_s, ys_s, *, tm):
    i = pl.program_id(0)
    has_prev = i > 0
    has_next = i < pl.num_programs(0) - 1
    scale1 = 1.0 + scale_ref[...]

    def norm_mod(xb):
        return ((_rms(xb) * g1_ref[...]) * scale1 + shift_ref[...]).astype(BF16)

    hn_s[0:_HALO, :] = norm_mod(xp_ref[...])
    hn_s[_HALO:_HALO + tm, :] = norm_mod(x_ref[...])
    hn_s[_HALO + tm:, :] = norm_mod(xn_ref[...])
    hn = hn_s[_HALO:_HALO + tm, :]

    bw = _PROJ_BLOCK
    u_blocks, vg_blocks = [], []

    def qkv_cols(seg, b):
        return wqkv_ref[:, seg * D_MODEL + b * bw:seg * D_MODEL + (b + 1) * bw]

    def rest_cols(seg, b):
        return wrest_ref[:, seg * D_MODEL + b * bw:seg * D_MODEL + (b + 1) * bw]

    def qk_block(c, b):
        lanes = slice(b * bw, (b + 1) * bw)
        z = _dot(hn_s[...], qkv_cols(c, b))
        z_s[c, 0:_HALO, lanes] = jnp.where(has_prev, z[0:_HALO, :], 0.0)
        z_s[c, _HALO:_HALO + tm, lanes] = z[_HALO:_HALO + tm, :]
        z_s[c, _HALO + tm:, lanes] = jnp.where(has_next, z[_HALO + tm:, :], 0.0)
        cw = cw_ref[:, c * D_MODEL + b * bw:c * D_MODEL + (b + 1) * bw]
        y = (cw[0:1, :] * z_s[c, _HALO - 1:_HALO - 1 + tm, lanes] + cw[1:2, :] * z_s[c, _HALO:_HALO + tm, lanes]
             + cw[2:3, :] * z_s[c, _HALO + 1:_HALO + 1 + tm, lanes])
        y = _silu(y)
        if c == 0:
            q_ref[:, lanes] = (y * (HEAD_DIM ** -0.5)).astype(BF16)
        else:
            kt_ref[lanes, :] = y.T.astype(BF16)

    for b in range(D_MODEL // bw):
        lanes = slice(b * bw, (b + 1) * bw)
        qk_block(0, b)
        v_ref[:, lanes] = _dot(hn, qkv_cols(2, b)).astype(BF16)
        qk_block(1, b)
        so_ref[:, lanes] = _sigmoid(_dot(hn, rest_cols(0, b))).astype(BF16)
        u_blocks.append(_gelu_tanh(_dot(hn, rest_cols(1, b))))
        sgm_ref[:, lanes] = _sigmoid(_dot(hn, rest_cols(3, b))).astype(BF16)
        vg_blocks.append(_gelu_tanh(_dot(hn, rest_cols(2, b))))

    gates = _dot(hn, wg_ref[...]) + bg_ref[...]
    gi = gates[:, 0:_GATE_LANES]
    lf = _log_sigmoid(gates[:, _GATE_LANES:])
    for j in range(tm // _L):
        rows = slice(j * _L, (j + 1) * _L)
        cum, mi, a, c_rows, a_rows, total, a_max = _gate_tables(gi[rows, :], lf[rows, :])
        tab_ref[rows, 0:_GATE_LANES] = cum
        tab_ref[rows, _GATE_LANES:2 * _GATE_LANES] = mi
        tab_ref[rows, 2 * _GATE_LANES:] = a
        gr_ref[0:_N_STATES, rows] = c_rows
        gr_ref[_N_STATES:, rows] = a_rows
        gm_ref[j * 8:(j + 1) * 8, :] = jnp.concatenate([total, a_max, jnp.zeros((6, _GATE_LANES), F32)], axis=0)

    vg = jnp.concatenate(vg_blocks, axis=1)
    u = jnp.concatenate(u_blocks, axis=1)
    vg = vg - jnp.mean(vg, axis=-1, keepdims=True)
    vn = (vg * lax.rsqrt(jnp.mean(vg * vg, axis=-1, keepdims=True) + EPS) * lng_ref[...] + lnb_ref[...]).astype(BF16)
    for p in range(tm // _L):
        for g in range(HEADS):
            rows = slice(p * _L, (p + 1) * _L)
            cols = slice(g * HEAD_DIM, (g + 1) * HEAD_DIM)
            mixed = _dot(wsd_ref[g], vn[rows, cols]) + bs_ref[:, g:g + 1]
            ys_s[rows, cols] = (u[rows, cols] * mixed).astype(BF16)
    gg = _sigmoid(_dot(hn, wrest_ref[:, 4 * D_MODEL:5 * D_MODEL]))
    s_ref[...] = (gg * _dot(ys_s[...], wbs_ref[...])).astype(BF16)


def _projection(x, shift, scale, g1, wqkv, wg, wrest, bg, cw, lng, lnb, wsd, bs2, wbs, *, tm):
    t = x.shape[0]
    nt = t // tm
    hb = tm // _HALO
    nhb = t // _HALO
    tile = lambda w: pl.BlockSpec((tm, w), lambda i: (i, 0))
    in_specs = [
        tile(D_MODEL),
        pl.BlockSpec((_HALO, D_MODEL), lambda i: (jnp.maximum(i * hb - 1, 0), 0)),
        pl.BlockSpec((_HALO, D_MODEL), lambda i: (jnp.minimum((i + 1) * hb, nhb - 1), 0)),
    ] + [_const_spec(a.shape) for a in (shift, scale, g1, wqkv, wg, wrest, bg, cw, lng, lnb, wsd, bs2, wbs)]
    gm_rows = 8 * (tm // _L)
    out_specs = [tile(D_MODEL), pl.BlockSpec((D_MODEL, tm), lambda i: (0, i)), tile(D_MODEL), tile(_TAB_W),
                 pl.BlockSpec((2 * _N_STATES, tm), lambda i: (0, i)),
                 pl.BlockSpec((gm_rows, _GATE_LANES), lambda i: (i, 0)),
                 tile(D_MODEL), tile(D_MODEL), tile(D_MODEL)]
    act = jax.ShapeDtypeStruct((t, D_MODEL), BF16)
    out_shape = [act, jax.ShapeDtypeStruct((D_MODEL, t), BF16), act, jax.ShapeDtypeStruct((t, _TAB_W), F32),
                 jax.ShapeDtypeStruct((2 * _N_STATES, t), F32), jax.ShapeDtypeStruct((8 * (t // _L), _GATE_LANES), F32),
                 act, act, act]
    return pl.pallas_call(
        functools.partial(_proj_kernel, tm=tm),
        grid=(nt,),
        in_specs=in_specs,
        out_specs=out_specs,
        out_shape=out_shape,
        scratch_shapes=[pltpu.VMEM((tm + 2 * _HALO, D_MODEL), BF16),
                        pltpu.VMEM((2, tm + 2 * _HALO, D_MODEL), F32),
                        pltpu.VMEM((tm, D_MODEL), BF16)],
        compiler_params=pltpu.CompilerParams(dimension_semantics=("parallel",), vmem_limit_bytes=_VMEM_LIMIT_V7X),
        name="projection",
    )(x, x, x, shift, scale, g1, wqkv, wg, wrest, bg, cw, lng, lnb, wsd, bs2, wbs)


def _mlstm_kernel(qf_ref, ktf_ref, vf_ref, tabf_ref, grf_ref, gmf_ref, qb_ref, ktb_ref, vb_ref, tabb_ref, grb_ref, gmb_ref,
                  c0_ref, m0_ref, hf_ref, hb_ref, c_ref, m_ref, *, per_step):
    @pl.when(pl.program_id(0) == 0)
    def _():
        c_ref[...] = c0_ref[...]
        m_ref[...] = m0_ref[...]

    row = lax.broadcasted_iota(jnp.int32, (_L, _L), 0)
    col = lax.broadcasted_iota(jnp.int32, (_L, _L), 1)
    ones = jnp.ones((_L, _GATE_LANES), BF16)
    lane = lax.broadcasted_iota(jnp.int32, (1, _GATE_LANES), 1)
    dirs = ((qf_ref, ktf_ref, vf_ref, tabf_ref, grf_ref, gmf_ref, hf_ref, col <= row),
            (qb_ref, ktb_ref, vb_ref, tabb_ref, grb_ref, gmb_ref, hb_ref, col >= row))
    for sub in range(per_step):
        m_prev = m_ref[...]
        m_next = []
        for d, (q_ref, kt_ref, v_ref, tab_ref, gr_ref, gm_ref, h_ref, mask) in enumerate(dirs):
            k = sub if d == 0 else per_step - 1 - sub
            rows = slice(k * _L, (k + 1) * _L)
            cum = tab_ref[rows, 0:_GATE_LANES]
            m_row = jnp.maximum(cum + m_prev, tab_ref[rows, _GATE_LANES:2 * _GATE_LANES])
            e = cum - m_row
            total, a_max = gm_ref[8 * k:8 * k + 1, :], gm_ref[8 * k + 1:8 * k + 2, :]
            m_new = jnp.maximum(total + m_prev, a_max)
            a_old = jnp.exp(total + m_prev - m_new)
            m_next.append(m_new)
            gr = gr_ref[:, rows]
            for hd in range(HEADS):
                j = d * HEADS + hd
                cols = slice(hd * HEAD_DIM, (hd + 1) * HEAD_DIM)
                qh, kth = q_ref[rows, cols], kt_ref[cols, rows]
                v_ext = jnp.concatenate([v_ref[rows, cols], ones], axis=1)
                e_b = jnp.broadcast_to(e[:, j:j + 1], (_L, _GATE_LANES))
                m_row_b = jnp.broadcast_to(m_row[:, j:j + 1], (_L, _GATE_LANES))
                c_row, a_row = gr[j:j + 1, :], gr[_N_STATES + j:_N_STATES + j + 1, :]
                p = jnp.exp(jnp.where(mask, jnp.concatenate([e_b, e_b], axis=1) + c_row, -jnp.inf))
                s = (_dot(qh, kth) * p).astype(BF16)
                c_prev = c_ref[j]
                w_inter = jnp.exp(e_b + m_prev[:, j:j + 1])
                num = _dot(s, v_ext) + jnp.concatenate([w_inter] * 3, axis=1) * _dot(qh, c_prev.astype(BF16))
                inv = 1.0 / jnp.maximum(jnp.abs(num[:, HEAD_DIM:]), jnp.exp(-m_row_b))
                h_ref[rows, cols] = (num[:, :HEAD_DIM] * jnp.concatenate([inv, inv], axis=1)).astype(BF16)
                kw_t = (kth.astype(F32) * jnp.exp(a_row - m_new[:, j:j + 1])).astype(BF16)
                c_ref[j] = a_old[:, j:j + 1] * c_prev + _dot(kw_t, v_ext)
        m_ref[...] = jnp.where(lane < HEADS, m_next[0], m_next[1])


def _mlstm(q, kt, v, tab, gr, gm, c0, m0):
    t = q.shape[0]
    per_step = min(_MLSTM_CHUNKS_PER_STEP, t // _L)
    blk = per_step * _L
    assert t % blk == 0
    ns = t // blk
    fwd_i = lambda i: i
    bwd_i = lambda i: ns - 1 - i

    def specs(ix):
        rows = lambda w: pl.BlockSpec((blk, w), lambda i: (ix(i), 0))
        return [rows(D_MODEL), pl.BlockSpec((D_MODEL, blk), lambda i: (0, ix(i))), rows(D_MODEL), rows(_TAB_W),
                pl.BlockSpec((2 * _N_STATES, blk), lambda i: (0, ix(i))),
                pl.BlockSpec((8 * per_step, _GATE_LANES), lambda i: (ix(i), 0))]

    out_rows = lambda ix: pl.BlockSpec((blk, D_MODEL), lambda i: (ix(i), 0))
    state_specs = [pl.BlockSpec(a.shape, lambda i, nd=a.ndim: (0,) * nd) for a in (c0, m0)]
    act = jax.ShapeDtypeStruct((t, D_MODEL), BF16)
    return pl.pallas_call(
        functools.partial(_mlstm_kernel, per_step=per_step),
        grid=(ns,),
        in_specs=specs(fwd_i) + specs(bwd_i) + [_const_spec(c0.shape), _const_spec(m0.shape)],
        out_specs=[out_rows(fwd_i), out_rows(bwd_i)] + state_specs,
        out_shape=[act, act] + [jax.ShapeDtypeStruct(a.shape, F32) for a in (c0, m0)],
        compiler_params=pltpu.CompilerParams(dimension_semantics=("arbitrary",), vmem_limit_bytes=_VMEM_LIMIT_V7X),
        name="mlstm",
    )(q, kt, v, tab, gr, gm, q, kt, v, tab, gr, gm, c0, m0)


def _post_kernel(hf_ref, hb_ref, so_ref, sgm_ref, s_ref, x_ref, hg_ref, wbm_ref, wout_ref, gate_ref,
                 g2_ref, shift_ref, scale_ref, hx_ref, hn2_ref, ym_s, y_s):
    tm = x_ref.shape[0]
    edges = tuple(range(0, tm + 1, tm // _POST_PIECES))

    def gate(p):
        rows = slice(edges[p], edges[p + 1])
        hm = hf_ref[rows, :].astype(F32) + hb_ref[rows, :].astype(F32)
        for hd in range(HEADS):
            cols = slice(hd * HEAD_DIM, (hd + 1) * HEAD_DIM)
            ym = _rms(hm[:, cols]) * hg_ref[:, cols]
            ym_s[rows, cols] = (so_ref[rows, cols].astype(F32) * ym).astype(BF16)

    def merge(p):
        rows = slice(edges[p], edges[p + 1])
        y = sgm_ref[rows, :].astype(F32) * _dot(ym_s[rows, :], wbm_ref[...]) + s_ref[rows, :].astype(F32)
        y_s[rows, :] = y.astype(BF16)

    def out(p):
        rows = slice(edges[p], edges[p + 1])
        hx = x_ref[rows, :] + gate_ref[...] * _dot(y_s[rows, :], wout_ref[...])
        hx_ref[rows, :] = hx
        hn2_ref[rows, :] = ((_rms(hx) * g2_ref[...]) * (1.0 + scale_ref[...]) + shift_ref[...]).astype(BF16)

    gate(0)
    for p in range(_POST_PIECES):
        if p + 1 < _POST_PIECES:
            gate(p + 1)
        merge(p)
        if p >= 1:
            out(p - 1)
    out(_POST_PIECES - 1)


def _post(hf, hb, so, sgm, s, x, hg, wbm, wout, gate, g2, shift, scale, *, tm):
    t = x.shape[0]
    tile = lambda: pl.BlockSpec((tm, D_MODEL), lambda i: (i, 0))
    return pl.pallas_call(
        _post_kernel,
        grid=(t // tm,),
        in_specs=[tile() for _ in range(6)] + [_const_spec(a.shape) for a in (hg, wbm, wout, gate, g2, shift, scale)],
        out_specs=[tile(), tile()],
        out_shape=[jax.ShapeDtypeStruct((t, D_MODEL), F32), jax.ShapeDtypeStruct((t, D_MODEL), BF16)],
        scratch_shapes=[pltpu.VMEM((tm, D_MODEL), BF16), pltpu.VMEM((tm, D_MODEL), BF16)],
        compiler_params=pltpu.CompilerParams(dimension_semantics=("parallel",), vmem_limit_bytes=_VMEM_LIMIT_V7X),
        name="post_mixer",
    )(hf, hb, so, sgm, s, x, hg, wbm, wout, gate, g2, shift, scale)


def _ffn_kernel(hn_ref, hnp_ref, hnn_ref, hx_ref, wab_ref, cw_ref, wdn_ref, gate_ref, fg_ref, o_ref,
                hne_s, a_s, b_s, act_s, acc_s, *, tm):
    i = pl.program_id(0)
    has_above = i > 0
    has_below = i < pl.num_programs(0) - 1
    ext = tm + 2 * GRID_W
    hne_s[0:GRID_W, :] = hnp_ref[...]
    hne_s[GRID_W:GRID_W + tm, :] = hn_ref[...]
    hne_s[GRID_W + tm:, :] = hnn_ref[...]
    for slot in range(2):
        a_s[slot, 0:_PAD, :] = jnp.zeros((_PAD, _FF_CHUNK), F32)
        a_s[slot, _PAD + ext:, :] = jnp.zeros((_PAD, _FF_CHUNK), F32)
    up_rows = ext // _FFN_PIECES
    mix_edges = tuple(min(j * up_rows, tm) for j in range(_FFN_PIECES + 1))
    down_edges = tuple(range(0, tm + 1, tm // _FFN_DOWN_PIECES))

    def up_piece(c, j):
        slot = c % 2
        m0, m1 = j * up_rows, (j + 1) * up_rows
        ab = _dot(hne_s[m0:m1, :], wab_ref[:, 2 * c * _FF_CHUNK:2 * (c + 1) * _FF_CHUNK])
        a = ab[:, :_FF_CHUNK]
        lo, hi = max(m0, GRID_W), min(m1, GRID_W + tm)
        if m0 < GRID_W:
            a_s[slot, _PAD + m0:_PAD + GRID_W, :] = jnp.where(has_above, a[:GRID_W - m0, :], 0.0)
        a_s[slot, _PAD + lo:_PAD + hi, :] = a[lo - m0:hi - m0, :]
        if m1 > GRID_W + tm:
            a_s[slot, _PAD + GRID_W + tm:_PAD + m1, :] = jnp.where(has_below, a[GRID_W + tm - m0:, :], 0.0)
        b_s[slot, lo - GRID_W:hi - GRID_W, :] = ab[lo - m0:hi - m0, _FF_CHUNK:]

    def mix_piece(c, p):
        slot = c % 2
        r0, r1 = mix_edges[p], mix_edges[p + 1]
        n = r1 - r0 + 2 * GRID_W
        gcol = lax.broadcasted_iota(jnp.int32, (n, 1), 0) % GRID_W
        taps = (jnp.where(gcol >= 1, a_s[slot, _PAD - 1 + r0:_PAD - 1 + r0 + n, :], 0.0),
                a_s[slot, _PAD + r0:_PAD + r0 + n, :],
                jnp.where(gcol <= GRID_W - 2, a_s[slot, _PAD + 1 + r0:_PAD + 1 + r0 + n, :], 0.0))
        cw = cw_ref[:, c * _FF_CHUNK:(c + 1) * _FF_CHUNK]
        conv = None
        for dr in range(3):
            for dc in range(3):
                term = cw[3 * dr + dc:3 * dr + dc + 1, :] * taps[dc][dr * GRID_W:dr * GRID_W + r1 - r0, :]
                conv = term if conv is None else conv + term
        act_s[slot, r0:r1, :] = (_silu(conv) * b_s[slot, r0:r1, :]).astype(BF16)

    def down_piece(c, p):
        r0, r1 = down_edges[p], down_edges[p + 1]
        part = _dot(act_s[c % 2, r0:r1, :], wdn_ref[c * _FF_CHUNK:(c + 1) * _FF_CHUNK, :])
        if c == 0:
            acc_s[r0:r1, :] = part
        else:
            acc_s[r0:r1, :] += part

    for step in range(_N_FF_CHUNKS + 2):
        for j in range(_FFN_PIECES):
            if step < _N_FF_CHUNKS:
                up_piece(step, j)
            if 0 <= step - 1 < _N_FF_CHUNKS:
                mix_piece(step - 1, j)
        if 0 <= step - 2:
            for p in range(_FFN_DOWN_PIECES):
                down_piece(step - 2, p)
    h2 = hx_ref[...] + gate_ref[...] * acc_s[...]
    o_ref[...] = _rms(h2) * fg_ref[...]


def _ffn(hn2, hx, wab, cwf, wdn, gate, fg, *, tm):
    t = hx.shape[0]
    rb = tm // GRID_W
    nrb = t // GRID_W
    tile = lambda: pl.BlockSpec((tm, D_MODEL), lambda i: (i, 0))
    ext = tm + 2 * GRID_W
    return pl.pallas_call(
        functools.partial(_ffn_kernel, tm=tm),
        grid=(t // tm,),
        in_specs=[tile(),
                  pl.BlockSpec((GRID_W, D_MODEL), lambda i: (jnp.maximum(i * rb - 1, 0), 0)),
                  pl.BlockSpec((GRID_W, D_MODEL), lambda i: (jnp.minimum((i + 1) * rb, nrb - 1), 0)),
                  tile()] + [_const_spec(a.shape) for a in (wab, cwf, wdn, gate, fg)],
        out_specs=tile(),
        out_shape=jax.ShapeDtypeStruct((t, D_MODEL), F32),
        scratch_shapes=[pltpu.VMEM((ext, D_MODEL), BF16),
                        pltpu.VMEM((2, ext + 2 * _PAD, _FF_CHUNK), F32),
                        pltpu.VMEM((2, tm, _FF_CHUNK), F32),
                        pltpu.VMEM((2, tm, _FF_CHUNK), BF16),
                        pltpu.VMEM((tm, D_MODEL), F32)],
        compiler_params=pltpu.CompilerParams(dimension_semantics=("parallel",), vmem_limit_bytes=_VMEM_LIMIT_V7X),
        name="conv_ffn",
    )(hn2, hn2, hn2, hx, wab, cwf, wdn, gate, fg)


def _gate_columns(wg):
    idx_i = [2 * d * HEADS + h for d in range(2) for h in range(HEADS)]
    idx_f = [(2 * d + 1) * HEADS + h for d in range(2) for h in range(HEADS)]
    pad = [(0, 0)] * (wg.ndim - 1) + [(0, _GATE_LANES - _N_STATES)]
    return jnp.concatenate([jnp.pad(wg[..., jnp.array(idx)], pad) for idx in (idx_i, idx_f)], axis=-1)


def _tile_size(t, want):
    tm = min(want, t)
    assert t % tm == 0 and tm % _L == 0, (t, tm)
    return tm


def kernel(x, c, ctx, c_ctx, w_mod, b_mod, norm1_g, w_in, b_gate, conv_qk, head_norm_g, sgu_ln_g, sgu_ln_b, w_s, b_s,
           w_branch_mlstm, w_branch_sgu, w_out, norm2_g, w_up, w_ffn_conv, w_down, final_g):
    assert x.shape[0] == 1 and w_mod.shape[0] == 1, "single batch element, single layer"
    d = D_MODEL
    x2, ctx2 = x[0], ctx[0]
    t = x2.shape[0]
    assert t % GRID_W == 0 and ctx2.shape[0] % _L == 0
    row = lambda a: a.reshape(1, -1)

    cc = jnp.zeros((d, 8), F32).at[:, 0].set(c[0]).at[:, 1].set(c_ctx)
    mod = _modulation(cc, w_mod[0], row(b_mod[0]))
    mx = [mod[0:1, j * d:(j + 1) * d] for j in range(N_MOD)]
    mc = [mod[1:2, j * d:(j + 1) * d] for j in range(N_MOD)]

    w_in_t = jnp.swapaxes(w_in[0], 0, 1)
    wqkv = _cast_columns(w_in_t, 0, _W_GATES)
    wg = _gate_weights(w_in_t)
    wrest = _cast_columns(w_in_t, _W_REST, w_in_t.shape[0] - _W_REST)
    bg = _gate_columns(row(b_gate[0]))
    eye2 = jnp.eye(2, dtype=F32)
    wsd = jnp.stack([jnp.kron(eye2, w_s[0, g]) for g in range(HEADS)]).astype(BF16)
    bs2 = jnp.tile(b_s[0].T, (2, 1))
    proj_w = (row(norm1_g[0]), wqkv, wg, wrest, bg, conv_qk[0],
              row(sgu_ln_g[0]), row(sgu_ln_b[0]), wsd, bs2, w_branch_sgu[0].astype(BF16))

    pc = _projection(ctx2, mc[0], mc[1], *proj_w, tm=_tile_size(ctx2.shape[0], 512))
    c0 = jnp.zeros((_N_STATES, HEAD_DIM, _C_EXT), F32)
    m0 = jnp.full((1, _GATE_LANES), M_INIT, F32)
    _, _, c1, m1 = _mlstm(*pc[:6], c0, m0)

    q, kt, v, tab, gr, gm, so, sgm, s = _projection(x2, mx[0], mx[1], *proj_w, tm=_tile_size(t, 512))
    hf, hb, _, _ = _mlstm(q, kt, v, tab, gr, gm, c1, m1)
    hx, hn2 = _post(hf, hb, so, sgm, s, x2, row(head_norm_g[0]), w_branch_mlstm[0].astype(BF16),
                    w_out[0].astype(BF16), mx[2], row(norm2_g[0]), mx[3], mx[4], tm=_tile_size(t, 1024))

    out = _ffn(hn2, hx, _regroup_up(w_up[0]), w_ffn_conv[0].reshape(9, D_FF), w_down[0].astype(BF16), mx[5],
               row(final_g), tm=_tile_size(t, 1024))
    return out[None]
```

```python
import functools

import jax
import jax.numpy as jnp
from jax import lax
from jax.experimental import pallas as pl
from jax.experimental.pallas import tpu as pltpu

F32 = jnp.float32
BF16 = jnp.bfloat16

D_MODEL = 1024
GRID_W = 64
HEADS = 4
HEAD_DIM = 256
N_MOD = 6
D_FF = 2816
EPS = 1e-6
M_INIT = -1e30

_L = 256
_HALO = 16
_FF_CHUNK = 256
_N_FF_CHUNKS = D_FF // _FF_CHUNK
_GATE_LANES = 128
_PAD = 8
_PROJ_BLOCK = 256
_FFN_PIECES = 3
_FFN_DOWN_PIECES = 2
_POST_PIECES = 2
_MLSTM_CHUNKS_PER_STEP = 4
_VMEM_LIMIT_V7X = 56 * 1024 * 1024

_N_STATES = 2 * HEADS
_TAB_W = 3 * _GATE_LANES
_C_EXT = HEAD_DIM + _GATE_LANES
_N_GATES = 4 * HEADS
_W_GATES = 3 * D_MODEL
_W_REST = _W_GATES + _N_GATES
_SUBLANES = 8
_PREP_BLOCK = 512


def _dot(a, b):
    return jnp.dot(a, b, preferred_element_type=F32)


def _split3(x):
    h1 = x.astype(BF16)
    r1 = x - h1.astype(F32)
    h2 = r1.astype(BF16)
    h3 = (r1 - h2.astype(F32)).astype(BF16)
    return h1, h2, h3


def _sigmoid(x):
    return 0.5 * jnp.tanh(0.5 * x) + 0.5


def _silu(x):
    h = 0.5 * x
    return h * jnp.tanh(h) + h


def _gelu_tanh(x):
    return 0.5 * x * (1.0 + jnp.tanh(0.7978845608028654 * (x + 0.044715 * (x * x * x))))


def _log_sigmoid(x):
    return jnp.minimum(x, 0.0) - jnp.log(1.0 + jnp.exp(-jnp.abs(x)))


def _rms(x):
    return x * lax.rsqrt(jnp.mean(x * x, axis=-1, keepdims=True) + EPS)


def _const_spec(shape):
    nd = len(shape)
    return pl.BlockSpec(shape, lambda *_: (0,) * nd, pipeline_mode=pl.Buffered(1))


def _cast_kernel(w_ref, o_ref):
    o_ref[...] = w_ref[...].astype(BF16)


def _transpose_cast_kernel(w_ref, o_ref):
    o_ref[...] = w_ref[...].T.astype(BF16)


def _shift_transpose_cast_kernel(w_ref, next_ref, o_ref, *, shift):
    x = jnp.concatenate([w_ref[...], next_ref[...]], axis=0)
    o_ref[...] = x[shift:shift + w_ref.shape[0], :].T.astype(BF16)


def _cast_columns(w_t, col0, ncols):
    rows = w_t.shape[1]
    shift = col0 % _PREP_BLOCK
    base = col0 - shift
    assert shift % _SUBLANES == 0 and ncols % _PREP_BLOCK == 0 and col0 + ncols <= w_t.shape[0]
    b0 = base // _PREP_BLOCK
    in_specs = [pl.BlockSpec((_PREP_BLOCK, rows), lambda k: (b0 + k, 0))]
    operands = [w_t]
    body = _transpose_cast_kernel
    if shift:
        assert _PREP_BLOCK % shift == 0
        per_block = _PREP_BLOCK // shift
        in_specs.append(pl.BlockSpec((shift, rows), lambda k: ((b0 + k + 1) * per_block, 0)))
        operands.append(w_t)
        body = functools.partial(_shift_transpose_cast_kernel, shift=shift)
    return pl.pallas_call(
        body,
        grid=(ncols // _PREP_BLOCK,),
        in_specs=in_specs,
        out_specs=pl.BlockSpec((rows, _PREP_BLOCK), lambda k: (0, k)),
        out_shape=jax.ShapeDtypeStruct((rows, ncols), BF16),
        compiler_params=pltpu.CompilerParams(dimension_semantics=("parallel",)),
        name="cast_columns",
    )(*operands)


def _regroup_up(w_up):
    rows = w_up.shape[0]
    return pl.pallas_call(
        _cast_kernel,
        grid=(2 * _N_FF_CHUNKS,),
        in_specs=[pl.BlockSpec((rows, _FF_CHUNK), lambda j: (0, (j % 2) * _N_FF_CHUNKS + j // 2))],
        out_specs=pl.BlockSpec((rows, _FF_CHUNK), lambda j: (0, j)),
        out_shape=jax.ShapeDtypeStruct((rows, 2 * D_FF), BF16),
        compiler_params=pltpu.CompilerParams(dimension_semantics=("parallel",)),
        name="regroup_up",
    )(w_up)


def _gate_cast_kernel(w_ref, sel_ref, o_ref):
    o_ref[...] = lax.dot_general(w_ref[...].astype(BF16), sel_ref[...], (((0,), (0,)), ((), ())),
                                 preferred_element_type=F32).astype(BF16)


def _gate_selector():
    sel = [[0.0] * (2 * _GATE_LANES) for _ in range(_N_GATES)]
    for d in range(2):
        for h in range(HEADS):
            j = d * HEADS + h
            sel[2 * d * HEADS + h][j] = 1.0
            sel[(2 * d + 1) * HEADS + h][_GATE_LANES + j] = 1.0
    return jnp.array(sel, BF16)


def _gate_weights(w_t):
    rows = w_t.shape[1]
    assert _W_GATES % _N_GATES == 0
    return pl.pallas_call(
        _gate_cast_kernel,
        grid=(1,),
        in_specs=[pl.BlockSpec((_N_GATES, rows), lambda k: (_W_GATES // _N_GATES, 0)),
                  pl.BlockSpec((_N_GATES, 2 * _GATE_LANES), lambda k: (0, 0))],
        out_specs=pl.BlockSpec((rows, 2 * _GATE_LANES), lambda k: (0, 0)),
        out_shape=jax.ShapeDtypeStruct((rows, 2 * _GATE_LANES), BF16),
        name="gate_weights",
    )(w_t, _gate_selector())


def _mod_kernel(cc_ref, w_ref, b_ref, o_ref):
    s = _silu(cc_ref[...])
    w = w_ref[...]
    rows = [jnp.sum(w * s[:, r:r + 1], axis=0, keepdims=True) + b_ref[...] for r in range(2)]
    o_ref[...] = jnp.concatenate(rows + [jnp.zeros((6, w.shape[1]), F32)], axis=0)


def _modulation(cc, w_mod, b_mod):
    n_out = w_mod.shape[1]
    nb = D_MODEL
    return pl.pallas_call(
        _mod_kernel,
        grid=(n_out // nb,),
        in_specs=[pl.BlockSpec((D_MODEL, 8), lambda j: (0, 0)),
                  pl.BlockSpec((D_MODEL, nb), lambda j: (0, j)),
                  pl.BlockSpec((1, nb), lambda j: (0, j))],
        out_specs=pl.BlockSpec((8, nb), lambda j: (0, j)),
        out_shape=jax.ShapeDtypeStruct((8, n_out), F32),
        compiler_params=pltpu.CompilerParams(dimension_semantics=("arbitrary",)),
        name="modulation",
    )(cc, w_mod, b_mod)


def _gate_tables(gi, lf):
    row = lax.broadcasted_iota(jnp.int32, (_L, _L), 0)
    col = lax.broadcasted_iota(jnp.int32, (_L, _L), 1)
    ltri = (col <= row).astype(BF16)
    l1, l2, l3 = _split3(lf)
    prefix = _dot(ltri, l3) + _dot(ltri, l2) + _dot(ltri, l1)
    total = prefix[_L - 1:_L, :]
    suffix = total - prefix + lf
    lane = lax.broadcasted_iota(jnp.int32, (1, _GATE_LANES), 1)
    cum = jnp.where(lane >= HEADS, suffix, prefix)
    a = total - cum + gi
    c = gi - cum
    c_rows = c.T[0:_N_STATES, :]
    a_rows = a.T[0:_N_STATES, :]
    pos = lax.broadcasted_iota(jnp.int32, (_N_STATES, _L), 1)
    fwd_max, bwd_max = c_rows, c_rows
    k = 1
    while k < _L:
        fwd_max = jnp.maximum(fwd_max, jnp.where(pos >= k, pltpu.roll(fwd_max, k, 1), -jnp.inf))
        bwd_max = jnp.maximum(bwd_max, jnp.where(pos < _L - k, pltpu.roll(bwd_max, _L - k, 1), -jnp.inf))
        k *= 2
    state = lax.broadcasted_iota(jnp.int32, (_N_STATES, _L), 0)
    run_max = jnp.where(state < HEADS, fwd_max, bwd_max)
    run_max_cols = jnp.concatenate([run_max, jnp.zeros((_GATE_LANES - _N_STATES, _L), F32)], axis=0).T
    return cum, cum + run_max_cols, a, c_rows, a_rows, total, jnp.max(a, axis=0, keepdims=True)


def _proj_kernel(x_ref, xp_ref, xn_ref, shift_ref, scale_ref, g1_ref, wqkv_ref, wg_ref, wrest_ref, bg_ref,
                 cw_ref, lng_ref, lnb_ref, wsd_ref, bs_ref, wbs_ref,
                 q_ref, kt_ref, v_ref, tab_ref, gr_ref, gm_ref, *rest_refs, tm, scan_only):
    if scan_only:
        hn_s, z_s, ys_s = rest_refs
    else:
        so_ref, sgm_ref, s_ref, hn_s, z_s, ys_s = rest_refs
    i = pl.program_id(0)
    has_prev = i > 0
    has_next = i < pl.num_programs(0) - 1
    scale1 = 1.0 + scale_ref[...]

    def norm_mod(xb):
        return ((_rms(xb) * g1_ref[...]) * scale1 + shift_ref[...]).astype(BF16)

    hn_s[0:_HALO, :] = norm_mod(xp_ref[...])
    hn_s[_HALO:_HALO + tm, :] = norm_mod(x_ref[...])
    hn_s[_HALO + tm:, :] = norm_mod(xn_ref[...])
    hn = hn_s[_HALO:_HALO + tm, :]

    bw = _PROJ_BLOCK
    u_blocks, vg_blocks = [], []

    def qkv_cols(seg, b):
        return wqkv_ref[:, seg * D_MODEL + b * bw:seg * D_MODEL + (b + 1) * bw]

    def rest_cols(seg, b):
        return wrest_ref[:, seg * D_MODEL + b * bw:seg * D_MODEL + (b + 1) * bw]

    def qk_block(c, b):
        lanes = slice(b * bw, (b + 1) * bw)
        z = _dot(hn_s[...], qkv_cols(c, b))
        z_s[c, 0:_HALO, lanes] = jnp.where(has_prev, z[0:_HALO, :], 0.0)
        z_s[c, _HALO:_HALO + tm, lanes] = z[_HALO:_HALO + tm, :]
        z_s[c, _HALO + tm:, lanes] = jnp.where(has_next, z[_HALO + tm:, :], 0.0)
        cw = cw_ref[:, c * D_MODEL + b * bw:c * D_MODEL + (b + 1) * bw]
        y = (cw[0:1, :] * z_s[c, _HALO - 1:_HALO - 1 + tm, lanes] + cw[1:2, :] * z_s[c, _HALO:_HALO + tm, lanes]
             + cw[2:3, :] * z_s[c, _HALO + 1:_HALO + 1 + tm, lanes])
        y = _silu(y)
        if c == 0:
            q_ref[:, lanes] = (y * (HEAD_DIM ** -0.5)).astype(BF16)
        else:
            kt_ref[lanes, :] = y.T.astype(BF16)

    for b in range(D_MODEL // bw):
        lanes = slice(b * bw, (b + 1) * bw)
        qk_block(0, b)
        v_ref[:, lanes] = _dot(hn, qkv_cols(2, b)).astype(BF16)
        qk_block(1, b)
        if scan_only:
            continue
        so_ref[:, lanes] = _sigmoid(_dot(hn, rest_cols(0, b))).astype(BF16)
        u_blocks.append(_gelu_tanh(_dot(hn, rest_cols(1, b))))
        sgm_ref[:, lanes] = _sigmoid(_dot(hn, rest_cols(3, b))).astype(BF16)
        vg_blocks.append(_gelu_tanh(_dot(hn, rest_cols(2, b))))

    gates = _dot(hn, wg_ref[...]) + bg_ref[...]
    gi = gates[:, 0:_GATE_LANES]
    lf = _log_sigmoid(gates[:, _GATE_LANES:])
    for j in range(tm // _L):
        rows = slice(j * _L, (j + 1) * _L)
        cum, mi, a, c_rows, a_rows, total, a_max = _gate_tables(gi[rows, :], lf[rows, :])
        tab_ref[rows, 0:_GATE_LANES] = cum
        tab_ref[rows, _GATE_LANES:2 * _GATE_LANES] = mi
        tab_ref[rows, 2 * _GATE_LANES:] = a
        gr_ref[0:_N_STATES, rows] = c_rows
        gr_ref[_N_STATES:, rows] = a_rows
        gm_ref[j * 8:(j + 1) * 8, :] = jnp.concatenate([total, a_max, jnp.zeros((6, _GATE_LANES), F32)], axis=0)

    if scan_only:
        return
    vg = jnp.concatenate(vg_blocks, axis=1)
    u = jnp.concatenate(u_blocks, axis=1)
    vg = vg - jnp.mean(vg, axis=-1, keepdims=True)
    vn = (vg * lax.rsqrt(jnp.mean(vg * vg, axis=-1, keepdims=True) + EPS) * lng_ref[...] + lnb_ref[...]).astype(BF16)
    for p in range(tm // _L):
        for g in range(HEADS):
            rows = slice(p * _L, (p + 1) * _L)
            cols = slice(g * HEAD_DIM, (g + 1) * HEAD_DIM)
            mixed = _dot(wsd_ref[g], vn[rows, cols]) + bs_ref[:, g:g + 1]
            ys_s[rows, cols] = (u[rows, cols] * mixed).astype(BF16)
    gg = _sigmoid(_dot(hn, wrest_ref[:, 4 * D_MODEL:5 * D_MODEL]))
    s_ref[...] = (gg * _dot(ys_s[...], wbs_ref[...])).astype(BF16)


def _projection(x, shift, scale, g1, wqkv, wg, wrest, bg, cw, lng, lnb, wsd, bs2, wbs, *, tm, scan_only=False):
    t = x.shape[0]
    nt = t // tm
    hb = tm // _HALO
    nhb = t // _HALO
    tile = lambda w: pl.BlockSpec((tm, w), lambda i: (i, 0))
    in_specs = [
        tile(D_MODEL),
        pl.BlockSpec((_HALO, D_MODEL), lambda i: (jnp.maximum(i * hb - 1, 0), 0)),
        pl.BlockSpec((_HALO, D_MODEL), lambda i: (jnp.minimum((i + 1) * hb, nhb - 1), 0)),
    ] + [_const_spec(a.shape) for a in (shift, scale, g1, wqkv, wg, wrest, bg, cw, lng, lnb, wsd, bs2, wbs)]
    gm_rows = 8 * (tm // _L)
    out_specs = [tile(D_MODEL), pl.BlockSpec((D_MODEL, tm), lambda i: (0, i)), tile(D_MODEL), tile(_TAB_W),
                 pl.BlockSpec((2 * _N_STATES, tm), lambda i: (0, i)),
                 pl.BlockSpec((gm_rows, _GATE_LANES), lambda i: (i, 0)),
                 tile(D_MODEL), tile(D_MODEL), tile(D_MODEL)]
    act = jax.ShapeDtypeStruct((t, D_MODEL), BF16)
    out_shape = [act, jax.ShapeDtypeStruct((D_MODEL, t), BF16), act, jax.ShapeDtypeStruct((t, _TAB_W), F32),
                 jax.ShapeDtypeStruct((2 * _N_STATES, t), F32), jax.ShapeDtypeStruct((8 * (t // _L), _GATE_LANES), F32),
                 act, act, act]
    if scan_only:
        out_specs, out_shape = out_specs[:6], out_shape[:6]
    return pl.pallas_call(
        functools.partial(_proj_kernel, tm=tm, scan_only=scan_only),
        grid=(nt,),
        in_specs=in_specs,
        out_specs=out_specs,
        out_shape=out_shape,
        scratch_shapes=[pltpu.VMEM((tm + 2 * _HALO, D_MODEL), BF16),
                        pltpu.VMEM((2, tm + 2 * _HALO, D_MODEL), F32),
                        pltpu.VMEM((tm, D_MODEL), BF16)],
        compiler_params=pltpu.CompilerParams(dimension_semantics=("parallel",), vmem_limit_bytes=_VMEM_LIMIT_V7X),
        name="projection",
    )(x, x, x, shift, scale, g1, wqkv, wg, wrest, bg, cw, lng, lnb, wsd, bs2, wbs)


def _mlstm_kernel(qf_ref, ktf_ref, vf_ref, tabf_ref, grf_ref, gmf_ref, qb_ref, ktb_ref, vb_ref, tabb_ref, grb_ref, gmb_ref,
                  c0_ref, m0_ref, hf_ref, hb_ref, c_ref, m_ref, *, per_step):
    @pl.when(pl.program_id(0) == 0)
    def _():
        c_ref[...] = c0_ref[...]
        m_ref[...] = m0_ref[...]

    row = lax.broadcasted_iota(jnp.int32, (_L, _L), 0)
    col = lax.broadcasted_iota(jnp.int32, (_L, _L), 1)
    ones = jnp.ones((_L, _GATE_LANES), BF16)
    lane = lax.broadcasted_iota(jnp.int32, (1, _GATE_LANES), 1)
    dirs = ((qf_ref, ktf_ref, vf_ref, tabf_ref, grf_ref, gmf_ref, hf_ref, col <= row),
            (qb_ref, ktb_ref, vb_ref, tabb_ref, grb_ref, gmb_ref, hb_ref, col >= row))
    for sub in range(per_step):
        m_prev = m_ref[...]
        m_next = []
        for d, (q_ref, kt_ref, v_ref, tab_ref, gr_ref, gm_ref, h_ref, mask) in enumerate(dirs):
            k = sub if d == 0 else per_step - 1 - sub
            rows = slice(k * _L, (k + 1) * _L)
            cum = tab_ref[rows, 0:_GATE_LANES]
            m_row = jnp.maximum(cum + m_prev, tab_ref[rows, _GATE_LANES:2 * _GATE_LANES])
            e = cum - m_row
            total, a_max = gm_ref[8 * k:8 * k + 1, :], gm_ref[8 * k + 1:8 * k + 2, :]
            m_new = jnp.maximum(total + m_prev, a_max)
            a_old = jnp.exp(total + m_prev - m_new)
            m_next.append(m_new)
            gr = gr_ref[:, rows]
            for hd in range(HEADS):
                j = d * HEADS + hd
                cols = slice(hd * HEAD_DIM, (hd + 1) * HEAD_DIM)
                qh, kth = q_ref[rows, cols], kt_ref[cols, rows]
                v_ext = jnp.concatenate([v_ref[rows, cols], ones], axis=1)
                e_b = jnp.broadcast_to(e[:, j:j + 1], (_L, _GATE_LANES))
                m_row_b = jnp.broadcast_to(m_row[:, j:j + 1], (_L, _GATE_LANES))
                c_row, a_row = gr[j:j + 1, :], gr[_N_STATES + j:_N_STATES + j + 1, :]
                p = jnp.exp(jnp.where(mask, jnp.concatenate([e_b, e_b], axis=1) + c_row, -jnp.inf))
                s = (_dot(qh, kth) * p).astype(BF16)
                c_prev = c_ref[j]
                w_inter = jnp.exp(e_b + m_prev[:, j:j + 1])
                num = _dot(s, v_ext) + jnp.concatenate([w_inter] * 3, axis=1) * _dot(qh, c_prev.astype(BF16))
                inv = 1.0 / jnp.maximum(jnp.abs(num[:, HEAD_DIM:]), jnp.exp(-m_row_b))
                h_ref[rows, cols] = (num[:, :HEAD_DIM] * jnp.concatenate([inv, inv], axis=1)).astype(BF16)
                kw_t = (kth.astype(F32) * jnp.exp(a_row - m_new[:, j:j + 1])).astype(BF16)
                c_ref[j] = a_old[:, j:j + 1] * c_prev + _dot(kw_t, v_ext)
        m_ref[...] = jnp.where(lane < HEADS, m_next[0], m_next[1])


def _mlstm(q, kt, v, tab, gr, gm, c0, m0):
    t = q.shape[0]
    per_step = min(_MLSTM_CHUNKS_PER_STEP, t // _L)
    blk = per_step * _L
    assert t % blk == 0
    ns = t // blk
    fwd_i = lambda i: i
    bwd_i = lambda i: ns - 1 - i

    def specs(ix):
        rows = lambda w: pl.BlockSpec((blk, w), lambda i: (ix(i), 0))
        return [rows(D_MODEL), pl.BlockSpec((D_MODEL, blk), lambda i: (0, ix(i))), rows(D_MODEL), rows(_TAB_W),
                pl.BlockSpec((2 * _N_STATES, blk), lambda i: (0, ix(i))),
                pl.BlockSpec((8 * per_step, _GATE_LANES), lambda i: (ix(i), 0))]

    out_rows = lambda ix: pl.BlockSpec((blk, D_MODEL), lambda i: (ix(i), 0))
    state_specs = [pl.BlockSpec(a.shape, lambda i, nd=a.ndim: (0,) * nd) for a in (c0, m0)]
    act = jax.ShapeDtypeStruct((t, D_MODEL), BF16)
    return pl.pallas_call(
        functools.partial(_mlstm_kernel, per_step=per_step),
        grid=(ns,),
        in_specs=specs(fwd_i) + specs(bwd_i) + [_const_spec(c0.shape), _const_spec(m0.shape)],
        out_specs=[out_rows(fwd_i), out_rows(bwd_i)] + state_specs,
        out_shape=[act, act] + [jax.ShapeDtypeStruct(a.shape, F32) for a in (c0, m0)],
        compiler_params=pltpu.CompilerParams(dimension_semantics=("arbitrary",), vmem_limit_bytes=_VMEM_LIMIT_V7X),
        name="mlstm",
    )(q, kt, v, tab, gr, gm, q, kt, v, tab, gr, gm, c0, m0)


def _post_kernel(hf_ref, hb_ref, so_ref, sgm_ref, s_ref, x_ref, hg_ref, wbm_ref, wout_ref, gate_ref,
                 g2_ref, shift_ref, scale_ref, hx_ref, hn2_ref, ym_s, y_s):
    tm = x_ref.shape[0]
    edges = tuple(range(0, tm + 1, tm // _POST_PIECES))

    def gate(p):
        rows = slice(edges[p], edges[p + 1])
        hm = hf_ref[rows, :].astype(F32) + hb_ref[rows, :].astype(F32)
        for hd in range(HEADS):
            cols = slice(hd * HEAD_DIM, (hd + 1) * HEAD_DIM)
            ym = _rms(hm[:, cols]) * hg_ref[:, cols]
            ym_s[rows, cols] = (so_ref[rows, cols].astype(F32) * ym).astype(BF16)

    def merge(p):
        rows = slice(edges[p], edges[p + 1])
        y = sgm_ref[rows, :].astype(F32) * _dot(ym_s[rows, :], wbm_ref[...]) + s_ref[rows, :].astype(F32)
        y_s[rows, :] = y.astype(BF16)

    def out(p):
        rows = slice(edges[p], edges[p + 1])
        hx = x_ref[rows, :] + gate_ref[...] * _dot(y_s[rows, :], wout_ref[...])
        hx_ref[rows, :] = hx
        hn2_ref[rows, :] = ((_rms(hx) * g2_ref[...]) * (1.0 + scale_ref[...]) + shift_ref[...]).astype(BF16)

    gate(0)
    for p in range(_POST_PIECES):
        if p + 1 < _POST_PIECES:
            gate(p + 1)
        merge(p)
        if p >= 1:
            out(p - 1)
    out(_POST_PIECES - 1)


def _post(hf, hb, so, sgm, s, x, hg, wbm, wout, gate, g2, shift, scale, *, tm):
    t = x.shape[0]
    tile = lambda: pl.BlockSpec((tm, D_MODEL), lambda i: (i, 0))
    return pl.pallas_call(
        _post_kernel,
        grid=(t // tm,),
        in_specs=[tile() for _ in range(6)] + [_const_spec(a.shape) for a in (hg, wbm, wout, gate, g2, shift, scale)],
        out_specs=[tile(), tile()],
        out_shape=[jax.ShapeDtypeStruct((t, D_MODEL), F32), jax.ShapeDtypeStruct((t, D_MODEL), BF16)],
        scratch_shapes=[pltpu.VMEM((tm, D_MODEL), BF16), pltpu.VMEM((tm, D_MODEL), BF16)],
        compiler_params=pltpu.CompilerParams(dimension_semantics=("parallel",), vmem_limit_bytes=_VMEM_LIMIT_V7X),
        name="post_mixer",
    )(hf, hb, so, sgm, s, x, hg, wbm, wout, gate, g2, shift, scale)


def _ffn_kernel(hn_ref, hnp_ref, hnn_ref, hx_ref, wab_ref, cw_ref, wdn_ref, gate_ref, fg_ref, o_ref,
                hne_s, a_s, b_s, act_s, acc_s, *, tm):
    i = pl.program_id(0)
    has_above = i > 0
    has_below = i < pl.num_programs(0) - 1
    ext = tm + 2 * GRID_W
    hne_s[0:GRID_W, :] = hnp_ref[...]
    hne_s[GRID_W:GRID_W + tm, :] = hn_ref[...]
    hne_s[GRID_W + tm:, :] = hnn_ref[...]
    for slot in range(2):
        a_s[slot, 0:_PAD, :] = jnp.zeros((_PAD, _FF_CHUNK), F32)
        a_s[slot, _PAD + ext:, :] = jnp.zeros((_PAD, _FF_CHUNK), F32)
    up_rows = ext // _FFN_PIECES
    mix_edges = tuple(min(j * up_rows, tm) for j in range(_FFN_PIECES + 1))
    down_edges = tuple(range(0, tm + 1, tm // _FFN_DOWN_PIECES))

    def up_piece(c, j):
        slot = c % 2
        m0, m1 = j * up_rows, (j + 1) * up_rows
        ab = _dot(hne_s[m0:m1, :], wab_ref[:, 2 * c * _FF_CHUNK:2 * (c + 1) * _FF_CHUNK])
        a = ab[:, :_FF_CHUNK]
        lo, hi = max(m0, GRID_W), min(m1, GRID_W + tm)
        if m0 < GRID_W:
            a_s[slot, _PAD + m0:_PAD + GRID_W, :] = jnp.where(has_above, a[:GRID_W - m0, :], 0.0)
        a_s[slot, _PAD + lo:_PAD + hi, :] = a[lo - m0:hi - m0, :]
        if m1 > GRID_W + tm:
            a_s[slot, _PAD + GRID_W + tm:_PAD + m1, :] = jnp.where(has_below, a[GRID_W + tm - m0:, :], 0.0)
        b_s[slot, lo - GRID_W:hi - GRID_W, :] = ab[lo - m0:hi - m0, _FF_CHUNK:]

    def mix_piece(c, p):
        slot = c % 2
        r0, r1 = mix_edges[p], mix_edges[p + 1]
        n = r1 - r0 + 2 * GRID_W
        gcol = lax.broadcasted_iota(jnp.int32, (n, 1), 0) % GRID_W
        taps = (jnp.where(gcol >= 1, a_s[slot, _PAD - 1 + r0:_PAD - 1 + r0 + n, :], 0.0),
                a_s[slot, _PAD + r0:_PAD + r0 + n, :],
                jnp.where(gcol <= GRID_W - 2, a_s[slot, _PAD + 1 + r0:_PAD + 1 + r0 + n, :], 0.0))
        cw = cw_ref[:, c * _FF_CHUNK:(c + 1) * _FF_CHUNK]
        conv = None
        for dr in range(3):
            for dc in range(3):
                term = cw[3 * dr + dc:3 * dr + dc + 1, :] * taps[dc][dr * GRID_W:dr * GRID_W + r1 - r0, :]
                conv = term if conv is None else conv + term
        act_s[slot, r0:r1, :] = (_silu(conv) * b_s[slot, r0:r1, :]).astype(BF16)

    def down_piece(c, p):
        r0, r1 = down_edges[p], down_edges[p + 1]
        part = _dot(act_s[c % 2, r0:r1, :], wdn_ref[c * _FF_CHUNK:(c + 1) * _FF_CHUNK, :])
        if c == 0:
            acc_s[r0:r1, :] = part
        else:
            acc_s[r0:r1, :] += part

    for step in range(_N_FF_CHUNKS + 2):
        for j in range(_FFN_PIECES):
            if step < _N_FF_CHUNKS:
                up_piece(step, j)
            if 0 <= step - 1 < _N_FF_CHUNKS:
                mix_piece(step - 1, j)
        if 0 <= step - 2:
            for p in range(_FFN_DOWN_PIECES):
                down_piece(step - 2, p)
    h2 = hx_ref[...] + gate_ref[...] * acc_s[...]
    o_ref[...] = _rms(h2) * fg_ref[...]


def _ffn(hn2, hx, wab, cwf, wdn, gate, fg, *, tm):
    t = hx.shape[0]
    rb = tm // GRID_W
    nrb = t // GRID_W
    tile = lambda: pl.BlockSpec((tm, D_MODEL), lambda i: (i, 0))
    ext = tm + 2 * GRID_W
    return pl.pallas_call(
        functools.partial(_ffn_kernel, tm=tm),
        grid=(t // tm,),
        in_specs=[tile(),
                  pl.BlockSpec((GRID_W, D_MODEL), lambda i: (jnp.maximum(i * rb - 1, 0), 0)),
                  pl.BlockSpec((GRID_W, D_MODEL), lambda i: (jnp.minimum((i + 1) * rb, nrb - 1), 0)),
                  tile()] + [_const_spec(a.shape) for a in (wab, cwf, wdn, gate, fg)],
        out_specs=tile(),
        out_shape=jax.ShapeDtypeStruct((t, D_MODEL), F32),
        scratch_shapes=[pltpu.VMEM((ext, D_MODEL), BF16),
                        pltpu.VMEM((2, ext + 2 * _PAD, _FF_CHUNK), F32),
                        pltpu.VMEM((2, tm, _FF_CHUNK), F32),
                        pltpu.VMEM((2, tm, _FF_CHUNK), BF16),
                        pltpu.VMEM((tm, D_MODEL), F32)],
        compiler_params=pltpu.CompilerParams(dimension_semantics=("parallel",), vmem_limit_bytes=_VMEM_LIMIT_V7X),
        name="conv_ffn",
    )(hn2, hn2, hn2, hx, wab, cwf, wdn, gate, fg)


def _gate_columns(wg):
    idx_i = [2 * d * HEADS + h for d in range(2) for h in range(HEADS)]
    idx_f = [(2 * d + 1) * HEADS + h for d in range(2) for h in range(HEADS)]
    pad = [(0, 0)] * (wg.ndim - 1) + [(0, _GATE_LANES - _N_STATES)]
    return jnp.concatenate([jnp.pad(wg[..., jnp.array(idx)], pad) for idx in (idx_i, idx_f)], axis=-1)


def _tile_size(t, want):
    tm = min(want, t)
    assert t % tm == 0 and tm % _L == 0, (t, tm)
    return tm


def kernel(x, c, ctx, c_ctx, w_mod, b_mod, norm1_g, w_in, b_gate, conv_qk, head_norm_g, sgu_ln_g, sgu_ln_b, w_s, b_s,
           w_branch_mlstm, w_branch_sgu, w_out, norm2_g, w_up, w_ffn_conv, w_down, final_g):
    assert x.shape[0] == 1 and w_mod.shape[0] == 1, "single batch element, single layer"
    d = D_MODEL
    x2, ctx2 = x[0], ctx[0]
    t = x2.shape[0]
    assert t % GRID_W == 0 and ctx2.shape[0] % _L == 0
    row = lambda a: a.reshape(1, -1)

    cc = jnp.zeros((d, 8), F32).at[:, 0].set(c[0]).at[:, 1].set(c_ctx)
    mod = _modulation(cc, w_mod[0], row(b_mod[0]))
    mx = [mod[0:1, j * d:(j + 1) * d] for j in range(N_MOD)]
    mc = [mod[1:2, j * d:(j + 1) * d] for j in range(N_MOD)]

    w_in_t = jnp.swapaxes(w_in[0], 0, 1)
    wqkv = _cast_columns(w_in_t, 0, _W_GATES)
    wg = _gate_weights(w_in_t)
    wrest = _cast_columns(w_in_t, _W_REST, w_in_t.shape[0] - _W_REST)
    bg = _gate_columns(row(b_gate[0]))
    eye2 = jnp.eye(2, dtype=F32)
    wsd = jnp.stack([jnp.kron(eye2, w_s[0, g]) for g in range(HEADS)]).astype(BF16)
    bs2 = jnp.tile(b_s[0].T, (2, 1))
    proj_w = (row(norm1_g[0]), wqkv, wg, wrest, bg, conv_qk[0],
              row(sgu_ln_g[0]), row(sgu_ln_b[0]), wsd, bs2, w_branch_sgu[0].astype(BF16))

    pc = _projection(ctx2, mc[0], mc[1], *proj_w, tm=_tile_size(ctx2.shape[0], 512), scan_only=True)
    c0 = jnp.zeros((_N_STATES, HEAD_DIM, _C_EXT), F32)
    m0 = jnp.full((1, _GATE_LANES), M_INIT, F32)
    _, _, c1, m1 = _mlstm(*pc, c0, m0)

    q, kt, v, tab, gr, gm, so, sgm, s = _projection(x2, mx[0], mx[1], *proj_w, tm=_tile_size(t, 512))
    hf, hb, _, _ = _mlstm(q, kt, v, tab, gr, gm, c1, m1)
    hx, hn2 = _post(hf, hb, so, sgm, s, x2, row(head_norm_g[0]), w_branch_mlstm[0].astype(BF16),
                    w_out[0].astype(BF16), mx[2], row(norm2_g[0]), mx[3], mx[4], tm=_tile_size(t, 1024))

    out = _ffn(hn2, hx, _regroup_up(w_up[0]), w_ffn_conv[0].reshape(9, D_FF), w_down[0].astype(BF16), mx[5],
               row(final_g), tm=_tile_size(t, 1024))
    return out[None]
```
